```python
import math
import functools
import jax
import jax.numpy as jnp
from jax import lax
import numpy as np

D_MODEL = 4096
BATCH = 4
SEQ = 2048
DEPTH = 2
DEC_BATCH = 8
DEC_SEQ = 8
PAST_LEN = 16384
PAGE_SIZE = 128

HEAD_DIM = 128
N_BRANCH = 4
BRANCH_WIDTH = D_MODEL // N_BRANCH
A_GROUPS = BRANCH_WIDTH // HEAD_DIM
CHUNK = 128
FOX_HEADS = BRANCH_WIDTH // HEAD_DIM
Q_BLOCK = 128
RET_HEADS = BRANCH_WIDTH // HEAD_DIM
RET_CHUNK = 128
ROPE_BASE = 10000.0
CONV_WIDTH = 3
D_FF = ((8 * D_MODEL // 3 + 255) // 256) * 256
N_MOD = 6
EPS = 1e-6
NEG_INF = -1e30
FORGET_BIAS = 5.0
CACHE_FORGET_LOGIT = 8.0

A_OFF = 0
A_COLS = 2 * BRANCH_WIDTH
B_OFF = A_OFF + A_COLS
B_COLS = 3 * BRANCH_WIDTH + FOX_HEADS
C_OFF = B_OFF + B_COLS
C_COLS = 4 * BRANCH_WIDTH
D_OFF = C_OFF + C_COLS
D_COLS = 3 * BRANCH_WIDTH
IN_COLS = D_OFF + D_COLS

kernel_name = 'hybrid_gated_branch_decoder_step'


def rms_norm(x, g):
    xf = x.astype(jnp.float32)
    y = xf * lax.rsqrt(jnp.mean(xf * xf, axis=-1, keepdims=True) + EPS)
    return (y * g.astype(jnp.float32)).astype(x.dtype)


def layer_norm(x, g):
    xf = x.astype(jnp.float32)
    xc = xf - jnp.mean(xf, axis=-1, keepdims=True)
    y = xc * lax.rsqrt(jnp.mean(xc * xc, axis=-1, keepdims=True) + EPS)
    return (y * g.astype(jnp.float32)).astype(x.dtype)


def head_group_norm(o, g):
    oc = o - jnp.mean(o, axis=-1, keepdims=True)
    y = oc * lax.rsqrt(jnp.mean(oc * oc, axis=-1, keepdims=True) + EPS)
    return y.reshape(o.shape[:2] + (-1,)) * g.astype(jnp.float32)


def causal_dwconv(x, hist, w):
    L = x.shape[1]
    xp = jnp.concatenate([hist.astype(x.dtype), x], axis=1)
    y = xp[:, 0:L] * w[0]
    for j in range(1, CONV_WIDTH):
        y = y + xp[:, j:j + L] * w[j]
    return y, xp[:, L:]


def rope(x, pos):
    half = HEAD_DIM // 2
    inv = ROPE_BASE ** (-jnp.arange(half, dtype=jnp.float32) / half)
    ang = pos.astype(jnp.float32)[:, None] * inv[None, :]
    cos = jnp.cos(ang)[None, :, None, :]
    sin = jnp.sin(ang)[None, :, None, :]
    xf = x.astype(jnp.float32)
    x1, x2 = xf[..., :half], xf[..., half:]
    return jnp.concatenate([x1 * cos - x2 * sin, x2 * cos + x1 * sin], axis=-1)


def gather_pages(pool, page_table):
    g = pool[page_table]
    return g.reshape((g.shape[0], -1) + g.shape[3:])


def chunk_gating(z_a, ln_g, w_s, b_s):
    z = jax.nn.gelu(z_a)
    u, v = jnp.split(z, 2, axis=-1)
    v = layer_norm(v, ln_g)
    Bn, L, W = v.shape
    n_chunks = -(-L // CHUNK)
    vp = jnp.pad(v, ((0, 0), (0, n_chunks * CHUNK - L), (0, 0)))
    vp = vp.reshape(Bn, n_chunks, CHUNK, A_GROUPS, HEAD_DIM)
    causal = jnp.tril(jnp.ones((CHUNK, CHUNK), dtype=bool))
    ws = jnp.where(causal[None], w_s, jnp.zeros_like(w_s))
    mixed = jnp.einsum('gts,bnsgc->bntgc', ws, vp) + b_s.T[:, :, None]
    mixed = mixed.reshape(Bn, n_chunks * CHUNK, W)[:, :L]
    tail = ((L - 1) // CHUNK) * CHUNK
    return u * mixed, v[:, tail:]


def fox_prompt(q, k, v, logf):
    Bn, L, H, hd = q.shape
    scale = HEAD_DIM ** -0.5
    F = jnp.cumsum(logf, axis=1).transpose(0, 2, 1)
    kf = k.astype(jnp.float32)
    vf = v.astype(jnp.float32)
    nb = L // Q_BLOCK
    qb = q.astype(jnp.float32).reshape(Bn, nb, Q_BLOCK, H, hd).transpose(1, 0, 2, 3, 4)
    kpos = jnp.arange(L)

    def block(args):
        i, qi = args
        s = jnp.einsum('bqhd,bkhd->bhqk', qi, kf) * scale
        Fq = lax.dynamic_slice_in_dim(F, i * Q_BLOCK, Q_BLOCK, axis=2)
        s = s + Fq[..., None] - F[:, :, None, :]
        qpos = i * Q_BLOCK + jnp.arange(Q_BLOCK)
        s = jnp.where(kpos[None, :] <= qpos[:, None], s, NEG_INF)
        p = jax.nn.softmax(s, axis=-1)
        return jnp.einsum('bhqk,bkhd->bqhd', p, vf)

    o = lax.map(block, (jnp.arange(nb), qb))
    return o.transpose(1, 0, 2, 3, 4).reshape(Bn, L, H, hd).astype(q.dtype)


def fox_sample(q, k, v, logf, k_past, v_past, logf_past):
    T = q.shape[1]
    P = k_past.shape[1]
    scale = HEAD_DIM ** -0.5
    qf = q.astype(jnp.float32)
    Fn = jnp.cumsum(logf, axis=1).transpose(0, 2, 1)
    lp = logf_past.astype(jnp.float32)
    G = (lax.cumsum(lp, axis=1, reverse=True) - lp).transpose(0, 2, 1)
    s_past = jnp.einsum('bqhd,bkhd->bhqk', qf, k_past.astype(jnp.float32)) * scale + Fn[..., None] + G[:, :, None, :]
    s_new = jnp.einsum('bqhd,bkhd->bhqk', qf, k.astype(jnp.float32)) * scale + Fn[..., None] - Fn[:, :, None, :]
    causal = jnp.tril(jnp.ones((T, T), dtype=bool))
    s_new = jnp.where(causal, s_new, NEG_INF)
    p = jax.nn.softmax(jnp.concatenate([s_past, s_new], axis=-1), axis=-1)
    o = (jnp.einsum('bhqk,bkhd->bqhd', p[..., :P], v_past.astype(jnp.float32))
         + jnp.einsum('bhqk,bkhd->bqhd', p[..., P:], v.astype(jnp.float32)))
    return o.astype(q.dtype)


def ret_log_gamma():
    return jnp.log1p(-jnp.exp2(-5.0 - jnp.arange(RET_HEADS, dtype=jnp.float32)))


def retention(q, k, v, S0):
    Bn, L, H, hd = q.shape
    C = math.gcd(L, RET_CHUNK)
    n = L // C
    lg = ret_log_gamma()
    idx = jnp.arange(C, dtype=jnp.float32)
    diff = idx[:, None] - idx[None, :]
    dmat = jnp.where(diff >= 0, jnp.exp(jnp.maximum(diff, 0.0)[None] * lg[:, None, None]), 0.0)
    q_decay = jnp.exp((idx[None, :] + 1.0) * lg[:, None])[None, :, :, None]
    k_decay = jnp.exp((C - 1.0 - idx)[None, :] * lg[:, None])[None, :, :, None]
    chunk_decay = jnp.exp(C * lg)[None, :, None, None]

    def to_chunks(t):
        return t.astype(jnp.float32).reshape(Bn, n, C, H, hd).transpose(1, 0, 3, 2, 4)

    def step(S, inp):
        qi, ki, vi = inp
        inner = jnp.einsum('bhtd,bhsd->bhts', qi, ki) * dmat[None]
        o = jnp.einsum('bhts,bhse->bhte', inner, vi) + jnp.einsum('bhtd,bhde->bhte', qi * q_decay, S)
        S = S * chunk_decay + jnp.einsum('bhsd,bhse->bhde', ki * k_decay, vi)
        return S, o

    S, o = lax.scan(step, S0.astype(jnp.float32), (to_chunks(q), to_chunks(k), to_chunks(v)))
    return o.transpose(1, 0, 3, 2, 4).reshape(Bn, L, H, hd), S


def _layer(x, c, pos, fox_attend, ret_state0, sc_hist, ffn_hist, lw):
    (w_ada, b_ada, g_pre_mix, g_post_mix, g_pre_ffn, g_post_ffn, w_in, b_forget, a_ln_g, a_ws, a_bs,
     ret_gn_g, w_sc_conv, w_branch, w_merge, b_merge, w_out, w_up, w_ffn_conv, b_ffn_conv, w_down) = lw
    Bn, L, _ = x.shape
    W = BRANCH_WIDTH

    def heads(t):
        return t.reshape(Bn, L, -1, HEAD_DIM)

    mod = jax.nn.silu(c) @ w_ada + b_ada
    sh1, sc1, g1, sh2, sc2, g2 = jnp.split(mod[:, None, :], N_MOD, axis=-1)

    h = rms_norm(x, g_pre_mix) * (1 + sc1) + sh1
    z = h @ w_in
    o_a, v_tail = chunk_gating(z[..., A_OFF:A_OFF + A_COLS], a_ln_g, a_ws, a_bs)
    zb = z[..., B_OFF:B_OFF + B_COLS]
    q_b, k_b, v_b = heads(zb[..., :W]), heads(zb[..., W:2 * W]), heads(zb[..., 2 * W:3 * W])
    logf = jax.nn.log_sigmoid((zb[..., 3 * W:] + b_forget).astype(jnp.float32))
    o_b = fox_attend(q_b, k_b, v_b, logf).reshape(Bn, L, W)
    zc = z[..., C_OFF:C_OFF + C_COLS]
    q_c = rope(heads(zc[..., :W]), pos)
    k_c = rope(heads(zc[..., W:2 * W]), pos) * HEAD_DIM ** -0.5
    v_c = heads(zc[..., 2 * W:3 * W])
    o_c, ret_state = retention(q_c, k_c, v_c, ret_state0)
    o_c = jax.nn.silu(zc[..., 3 * W:]) * head_group_norm(o_c, ret_gn_g).astype(x.dtype)
    b_g, c_g, h_d = jnp.split(z[..., D_OFF:D_OFF + D_COLS], 3, axis=-1)
    y_d, sc_hist_new = causal_dwconv(c_g * h_d, sc_hist, w_sc_conv)
    o_d = b_g * y_d
    merged = None
    for n, o_n in enumerate((o_a, o_b, o_c, o_d)):
        term = jax.nn.sigmoid(h @ w_merge[n] + b_merge[n]) * (o_n @ w_branch[n])
        merged = term if merged is None else merged + term
    x = x + g1 * rms_norm(merged @ w_out, g_post_mix)

    h2 = rms_norm(x, g_pre_ffn) * (1 + sc2) + sh2
    up = h2 @ w_up
    up_c, ffn_hist_new = causal_dwconv(up, ffn_hist, w_ffn_conv)
    a, b = jnp.split(up_c + b_ffn_conv, 2, axis=-1)
    x = x + g2 * rms_norm((jax.nn.silu(a) * b) @ w_down, g_post_ffn)
    return x, (k_b, v_b, logf, ret_state, sc_hist_new, v_tail, ffn_hist_new)


def setup_inputs(seed: int = 0) -> dict:
    key = jax.random.key(seed)
    ks = iter(jax.random.split(key, 48))

    def nrm(shape, s=1.0):
        return jax.random.normal(next(ks), shape, jnp.float32) * s

    n_pages = PAST_LEN // PAGE_SIZE
    n_pool = (DEC_BATCH * n_pages * 5) // 4
    W = BRANCH_WIDTH
    D = D_MODEL
    perm = jax.random.permutation(next(ks), n_pool)
    page_table = perm[:DEC_BATCH * n_pages].reshape(DEC_BATCH, n_pages).astype(jnp.int32)
    inp = {}
    inp['x_prompt'] = nrm((BATCH, SEQ, D))
    inp['x_sample'] = nrm((DEC_BATCH, DEC_SEQ, D))
    inp['cache_k'] = nrm((DEPTH, n_pool, PAGE_SIZE, FOX_HEADS, HEAD_DIM))
    inp['cache_v'] = nrm((DEPTH, n_pool, PAGE_SIZE, FOX_HEADS, HEAD_DIM))
    inp['cache_logf'] = jax.nn.log_sigmoid(CACHE_FORGET_LOGIT + nrm((DEPTH, n_pool, PAGE_SIZE, FOX_HEADS), 0.5))
    inp['state_ret'] = nrm((DEPTH, DEC_BATCH, RET_HEADS, HEAD_DIM, HEAD_DIM))
    inp['state_conv'] = nrm((DEPTH, DEC_BATCH, CONV_WIDTH - 1, W))
    inp['state_ffn_conv'] = nrm((DEPTH, DEC_BATCH, CONV_WIDTH - 1, 2 * D_FF))
    inp['page_table'] = page_table
    inp['c_prompt'] = nrm((BATCH, D))
    inp['c_sample'] = nrm((DEC_BATCH, D))
    inp['w_ada'] = nrm((DEPTH, D, N_MOD * D), 0.5 * D ** -0.5)
    inp['b_ada'] = nrm((DEPTH, N_MOD * D), 0.02)
    inp['g_pre_mix'] = 1.0 + nrm((DEPTH, D), 0.05)
    inp['g_post_mix'] = 1.0 + nrm((DEPTH, D), 0.05)
    inp['g_pre_ffn'] = 1.0 + nrm((DEPTH, D), 0.05)
    inp['g_post_ffn'] = 1.0 + nrm((DEPTH, D), 0.05)
    inp['w_in'] = nrm((DEPTH, D, IN_COLS), D ** -0.5)
    inp['b_forget'] = FORGET_BIAS + nrm((DEPTH, FOX_HEADS), 0.5)
    inp['a_ln_g'] = 1.0 + nrm((DEPTH, W), 0.05)
    inp['a_ws'] = nrm((DEPTH, A_GROUPS, CHUNK, CHUNK), CHUNK ** -0.5)
    inp['a_bs'] = 1.0 + nrm((DEPTH, A_GROUPS, CHUNK), 0.1)
    inp['ret_gn_g'] = 1.0 + nrm((DEPTH, W), 0.05)
    inp['w_sc_conv'] = nrm((DEPTH, CONV_WIDTH, W), CONV_WIDTH ** -0.5)
    inp['w_branch'] = nrm((DEPTH, N_BRANCH, W, D), W ** -0.5)
    inp['w_merge'] = nrm((DEPTH, N_BRANCH, D, D), D ** -0.5)
    inp['b_merge'] = nrm((DEPTH, N_BRANCH, D), 0.02)
    inp['w_out'] = nrm((DEPTH, D, D), D ** -0.5)
    inp['w_up'] = nrm((DEPTH, D, 2 * D_FF), D ** -0.5)
    inp['w_ffn_conv'] = nrm((DEPTH, CONV_WIDTH, 2 * D_FF), CONV_WIDTH ** -0.5)
    inp['b_ffn_conv'] = nrm((DEPTH, 2 * D_FF), 0.02)
    inp['w_down'] = nrm((DEPTH, D_FF, D), D_FF ** -0.5)
    return inp


def reference(x_prompt, x_sample, cache_k, cache_v, cache_logf, state_ret, state_conv, state_ffn_conv,
              page_table, c_prompt, c_sample, w_ada, b_ada, g_pre_mix, g_post_mix, g_pre_ffn, g_post_ffn,
              w_in, b_forget, a_ln_g, a_ws, a_bs, ret_gn_g, w_sc_conv, w_branch, w_merge, b_merge, w_out,
              w_up, w_ffn_conv, b_ffn_conv, w_down):
    Bp, Lp, _ = x_prompt.shape
    Ls = x_sample.shape[1]
    pos_p = jnp.arange(Lp)
    pos_s = PAST_LEN + jnp.arange(Ls)
    yp, ys = x_prompt, x_sample
    st_p, st_s = [], []
    for l in range(DEPTH):
        lw = (w_ada[l], b_ada[l], g_pre_mix[l], g_post_mix[l], g_pre_ffn[l], g_post_ffn[l], w_in[l],
              b_forget[l], a_ln_g[l], a_ws[l], a_bs[l], ret_gn_g[l], w_sc_conv[l], w_branch[l], w_merge[l],
              b_merge[l], w_out[l], w_up[l], w_ffn_conv[l], b_ffn_conv[l], w_down[l])
        yp, sp = _layer(yp, c_prompt, pos_p, fox_prompt,
                        jnp.zeros((Bp, RET_HEADS, HEAD_DIM, HEAD_DIM), jnp.float32),
                        jnp.zeros((Bp, CONV_WIDTH - 1, BRANCH_WIDTH), x_prompt.dtype),
                        jnp.zeros((Bp, CONV_WIDTH - 1, 2 * D_FF), x_prompt.dtype), lw)
        fox_s = functools.partial(fox_sample,
                                  k_past=gather_pages(cache_k[l], page_table),
                                  v_past=gather_pages(cache_v[l], page_table),
                                  logf_past=gather_pages(cache_logf[l], page_table))
        ys, ss = _layer(ys, c_sample, pos_s, fox_s, state_ret[l], state_conv[l], state_ffn_conv[l], lw)
        st_p.append(sp)
        st_s.append(ss)

    fox_k_prompt = jnp.stack([s[0] for s in st_p], axis=0)
    fox_v_prompt = jnp.stack([s[1] for s in st_p], axis=0)
    fox_logf_prompt = jnp.stack([s[2] for s in st_p], axis=0)
    fox_k_sample = jnp.stack([s[0] for s in st_s], axis=0)
    fox_v_sample = jnp.stack([s[1] for s in st_s], axis=0)
    fox_logf_sample = jnp.stack([s[2] for s in st_s], axis=0)
    ret_state_prompt = jnp.stack([s[3] for s in st_p], axis=0)
    ret_state_sample = jnp.stack([s[3] for s in st_s], axis=0)
    conv_state_prompt = jnp.stack([s[4] for s in st_p], axis=0)
    conv_state_sample = jnp.stack([s[4] for s in st_s], axis=0)
    chunk_v_prompt = jnp.stack([s[5] for s in st_p], axis=0)
    chunk_v_sample = jnp.stack([s[5] for s in st_s], axis=0)
    ffn_conv_prompt = jnp.stack([s[6] for s in st_p], axis=0)
    ffn_conv_sample = jnp.stack([s[6] for s in st_s], axis=0)
    return (yp, ys, fox_k_prompt, fox_v_prompt, fox_logf_prompt, fox_k_sample, fox_v_sample, fox_logf_sample,
            ret_state_prompt, ret_state_sample, conv_state_prompt, conv_state_sample,
            chunk_v_prompt, chunk_v_sample, ffn_conv_prompt, ffn_conv_sample)
```

```python
import functools
import math

import jax
import jax.numpy as jnp
from jax import lax
from jax.experimental import pallas as pl
from jax.experimental.pallas import tpu as pltpu

D_MODEL = 4096
HEAD_DIM = 128
N_HEADS = 8
WIDTH = N_HEADS * HEAD_DIM
CHUNK = 128
PAST_LEN = 16384
PAGE_SIZE = 128
ROPE_BASE = 10000.0
CONV_WIDTH = 3
D_FF = 11008
D_FF_PAD = 11264
N_MOD = 6
EPS = 1e-6
NEG_INF = -1e30
Z_COLS = 12 * WIDTH
FORGET_COL = 5 * WIDTH

BQ_BLK, BK_BLK, BV_BLK = 16, 24, 32
CQ_BLK, CK_BLK, CV_BLK, CG_BLK = 40, 48, 56, 64

VMEM_LIMIT_BYTES = 56 * 1024 * 1024

BF16 = jnp.bfloat16
F32 = jnp.float32


def _params(*semantics):
    return pltpu.CompilerParams(dimension_semantics=semantics, vmem_limit_bytes=VMEM_LIMIT_BYTES)


def _ada_kernel(c_ref, w_ref, b_ref, o_ref):
    c = c_ref[...]
    s = (c * jax.nn.sigmoid(c)).astype(BF16)
    o_ref[...] = jnp.dot(s, w_ref[...].astype(BF16), preferred_element_type=F32) + b_ref[...]


def ada_modulation(c_all, w_ada, b_ada):
    depth, d, n = w_ada.shape
    rows = c_all.shape[0]
    bn = 512
    return pl.pallas_call(
        _ada_kernel,
        grid=(depth, n // bn),
        in_specs=[
            pl.BlockSpec((rows, d), lambda l, j: (0, 0)),
            pl.BlockSpec((None, d, bn), lambda l, j: (l, 0, j)),
            pl.BlockSpec((None, 1, bn), lambda l, j: (l, 0, j)),
        ],
        out_specs=pl.BlockSpec((None, rows, bn), lambda l, j: (l, 0, j)),
        out_shape=jax.ShapeDtypeStruct((depth, rows, n), F32),
        compiler_params=_params("arbitrary", "arbitrary"),
        name="ada_modulation",
    )(c_all, w_ada, b_ada.reshape(depth, 1, n))


class Group:
    def __init__(self, n_seq, seq_len, row_block, mm_block):
        self.n_seq = n_seq
        self.seq_len = seq_len
        self.tokens = n_seq * seq_len
        self.row_block = row_block
        self.mm_block = mm_block
        self.per_seq_rows = row_block <= seq_len

    def mod_operand(self, mod):
        if self.per_seq_rows:
            return mod.reshape(self.n_seq, N_MOD, 1, D_MODEL)
        tok = jnp.repeat(mod.reshape(self.n_seq, N_MOD, D_MODEL), self.seq_len, axis=0)
        return tok.transpose(1, 0, 2)

    def mod_spec(self, idx):
        if self.per_seq_rows:
            per = self.seq_len // self.row_block
            return pl.BlockSpec((None, None, 1, D_MODEL), lambda i: (i // per, idx, 0, 0))
        return pl.BlockSpec((None, self.row_block, D_MODEL), lambda i: (idx, i, 0))


def _rms(x, g):
    return x * lax.rsqrt(jnp.mean(x * x, axis=-1, keepdims=True) + EPS) * g


def _prenorm_kernel(x_ref, g_ref, sc_ref, sh_ref, h_ref):
    h_ref[...] = (_rms(x_ref[...], g_ref[...]) * (1.0 + sc_ref[...]) + sh_ref[...]).astype(h_ref.dtype)


def prenorm(grp, x, g, mod_op, sh_idx, sc_idx):
    br = grp.row_block
    row = pl.BlockSpec((br, D_MODEL), lambda i: (i, 0))
    vec = pl.BlockSpec((1, D_MODEL), lambda i: (0, 0))
    return pl.pallas_call(
        _prenorm_kernel,
        grid=(grp.tokens // br,),
        in_specs=[row, vec, grp.mod_spec(sc_idx), grp.mod_spec(sh_idx)],
        out_specs=row,
        out_shape=jax.ShapeDtypeStruct((grp.tokens, D_MODEL), BF16),
        compiler_params=_params("parallel"),
        name="prenorm",
    )(x, g.reshape(1, D_MODEL), mod_op, mod_op)


def _residual_kernel(x_ref, y_ref, gpost_ref, gate_ref, xo_ref):
    xo_ref[...] = x_ref[...] + gate_ref[...] * _rms(y_ref[...], gpost_ref[...])


def _residual_prenorm_kernel(x_ref, y_ref, gpost_ref, gate_ref, gpre_ref, sc_ref, sh_ref, xo_ref, h_ref):
    xn = x_ref[...] + gate_ref[...] * _rms(y_ref[...], gpost_ref[...])
    xo_ref[...] = xn
    h_ref[...] = (_rms(xn, gpre_ref[...]) * (1.0 + sc_ref[...]) + sh_ref[...]).astype(h_ref.dtype)


def residual(grp, x, y, g_post, mod_op, gate_idx, g_pre=None, pre_mod_op=None, sh_idx=None, sc_idx=None):
    br = grp.row_block
    row = pl.BlockSpec((br, D_MODEL), lambda i: (i, 0))
    vec = pl.BlockSpec((1, D_MODEL), lambda i: (0, 0))
    x_shape = jax.ShapeDtypeStruct((grp.tokens, D_MODEL), F32)
    if g_pre is None:
        return pl.pallas_call(
            _residual_kernel,
            grid=(grp.tokens // br,),
            in_specs=[row, row, vec, grp.mod_spec(gate_idx)],
            out_specs=row,
            out_shape=x_shape,
            compiler_params=_params("parallel"),
            name="residual",
        )(x, y, g_post.reshape(1, D_MODEL), mod_op)
    return pl.pallas_call(
        _residual_prenorm_kernel,
        grid=(grp.tokens // br,),
        in_specs=[row, row, vec, grp.mod_spec(gate_idx), vec, grp.mod_spec(sc_idx), grp.mod_spec(sh_idx)],
        out_specs=[row, row],
        out_shape=[x_shape, jax.ShapeDtypeStruct((grp.tokens, D_MODEL), BF16)],
        compiler_params=_params("parallel"),
        name="residual_prenorm",
    )(x, y, g_post.reshape(1, D_MODEL), mod_op, g_pre.reshape(1, D_MODEL), pre_mod_op, pre_mod_op)


def _mm_kernel(x_ref, w_ref, o_ref):
    o_ref[...] = jnp.dot(x_ref[...].astype(BF16), w_ref[...],
                         preferred_element_type=F32).astype(o_ref.dtype)


def matmul(x, w, out_dtype, bm, bn):
    m, k = x.shape
    n = w.shape[1]
    return pl.pallas_call(
        _mm_kernel,
        grid=(m // bm, n // bn),
        in_specs=[pl.BlockSpec((bm, k), lambda i, j: (i, 0)),
                  pl.BlockSpec((k, bn), lambda i, j: (0, j))],
        out_specs=pl.BlockSpec((bm, bn), lambda i, j: (i, j)),
        out_shape=jax.ShapeDtypeStruct((m, n), out_dtype),
        compiler_params=_params("parallel", "arbitrary"),
        name="matmul",
    )(x, w)


def _mm_ksplit_kernel(x_ref, w_ref, o_ref):
    @pl.when(pl.program_id(2) == 0)
    def _():
        o_ref[...] = jnp.zeros_like(o_ref)

    o_ref[...] += jnp.dot(x_ref[...].astype(BF16), w_ref[...], preferred_element_type=F32)


def matmul_ksplit(x, w, bm, bn, bk):
    m, k = x.shape
    n = w.shape[1]
    return pl.pallas_call(
        _mm_ksplit_kernel,
        grid=(m // bm, n // bn, k // bk),
        in_specs=[pl.BlockSpec((bm, bk), lambda i, j, kk: (i, kk)),
                  pl.BlockSpec((bk, bn), lambda i, j, kk: (kk, j))],
        out_specs=pl.BlockSpec((bm, bn), lambda i, j, kk: (i, j)),
        out_shape=jax.ShapeDtypeStruct((m, n), F32),
        compiler_params=_params("parallel", "parallel", "arbitrary"),
        name="matmul_ksplit",
    )(x, w)


def _logf_kernel(h_ref, wf_ref, bf_ref, logf_ref, cum_ref, carry_ref, *, tile, seq_len, cw):
    i = pl.program_id(0)
    logits = lax.dot_general(wf_ref[...], h_ref[...], (((1,), (1,)), ((), ())),
                             preferred_element_type=F32)[:N_HEADS]
    x = logits + bf_ref[...]
    logf = jnp.minimum(x, 0.0) - jnp.log1p(jnp.exp(-jnp.abs(x)))
    logf_ref[...] = logf

    seg = min(seq_len, cw)
    src = lax.broadcasted_iota(jnp.int32, (cw, cw), 0)
    dst = lax.broadcasted_iota(jnp.int32, (cw, cw), 1)
    tri = ((src <= dst) & (src // seg == dst // seg)).astype(F32)

    if seq_len > tile:
        @pl.when((i * tile) % seq_len == 0)
        def _():
            carry_ref[...] = jnp.zeros_like(carry_ref)
        carry = carry_ref[...]
    else:
        carry = jnp.zeros((N_HEADS, 1), F32)

    for c in range(tile // cw):
        part = jnp.dot(logf[:, c * cw:(c + 1) * cw], tri, precision=lax.Precision.HIGHEST,
                       preferred_element_type=F32)
        cum = part + carry
        cum_ref[:, c * cw:(c + 1) * cw] = cum
        if seq_len > cw:
            carry = cum[:, cw - 1:cw]
    if seq_len > tile:
        carry_ref[...] = carry


def forget_logits(grp, h, wf_t, b_forget):
    tile = min(512, grp.tokens)
    cw = min(128, tile)
    kern = functools.partial(_logf_kernel, tile=tile, seq_len=grp.seq_len, cw=cw)
    out = jax.ShapeDtypeStruct((N_HEADS, grp.tokens), F32)
    return pl.pallas_call(
        kern,
        grid=(grp.tokens // tile,),
        in_specs=[pl.BlockSpec((tile, D_MODEL), lambda i: (i, 0)),
                  pl.BlockSpec((16, D_MODEL), lambda i: (0, 0)),
                  pl.BlockSpec((N_HEADS, 1), lambda i: (0, 0))],
        out_specs=[pl.BlockSpec((N_HEADS, tile), lambda i: (0, i))] * 2,
        out_shape=[out, out],
        scratch_shapes=[pltpu.VMEM((N_HEADS, 1), F32)],
        compiler_params=_params("arbitrary"),
        name="forget_logits",
    )(h, wf_t, b_forget.reshape(N_HEADS, 1))


def _gating_kernel(z_ref, lng_ref, ws_ref, bs_ref, o_ref, vt_ref, *, rows, mm_dtype):
    z = jax.nn.gelu(z_ref[...])
    u = z[:, :WIDTH]
    v = z[:, WIDTH:]
    vc = v - jnp.mean(v, axis=-1, keepdims=True)
    vn = vc * lax.rsqrt(jnp.mean(vc * vc, axis=-1, keepdims=True) + EPS) * lng_ref[...]
    vt_ref[...] = vn
    t_idx = lax.broadcasted_iota(jnp.int32, (rows, rows), 0)
    s_idx = lax.broadcasted_iota(jnp.int32, (rows, rows), 1)
    causal = s_idx <= t_idx
    for g in range(N_HEADS):
        cols = slice(g * HEAD_DIM, (g + 1) * HEAD_DIM)
        ws = jnp.where(causal, ws_ref[g], 0.0).astype(mm_dtype)
        mixed = jnp.dot(ws, vn[:, cols].astype(mm_dtype), preferred_element_type=F32) + bs_ref[g]
        o_ref[:, cols] = (u[:, cols] * mixed).astype(o_ref.dtype)


def chunk_gating(z, ln_g, ws, bs_b, n_seq, n_chunks, rows, out_dtype):
    tokens = z.shape[0]
    kern = functools.partial(_gating_kernel, rows=rows, mm_dtype=BF16 if rows >= 16 else F32)
    return pl.pallas_call(
        kern,
        grid=(n_seq, n_chunks),
        in_specs=[pl.BlockSpec((rows, 2 * WIDTH), lambda b, n: (b * n_chunks + n, 0)),
                  pl.BlockSpec((1, WIDTH), lambda b, n: (0, 0)),
                  pl.BlockSpec((N_HEADS, rows, rows), lambda b, n: (0, 0, 0)),
                  pl.BlockSpec((N_HEADS, rows, HEAD_DIM), lambda b, n: (0, 0, 0))],
        out_specs=[pl.BlockSpec((rows, WIDTH), lambda b, n: (b * n_chunks + n, 0)),
                   pl.BlockSpec((None, rows, WIDTH), lambda b, n: (b, 0, 0))],
        out_shape=[jax.ShapeDtypeStruct((tokens, WIDTH), out_dtype),
                   jax.ShapeDtypeStruct((n_seq, rows, WIDTH), F32)],
        compiler_params=_params("arbitrary", "arbitrary"),
        name="chunk_gating",
    )(z, ln_g.reshape(1, WIDTH), ws, bs_b)


def _fox_prompt_kernel(q_ref, k_ref, v_ref, fq_ref, fk_ref, o_ref, *, blk):
    qi = pl.program_id(2)
    q = (q_ref[...] * HEAD_DIM ** -0.5).astype(BF16)
    fq = fq_ref[...]
    q_pos = qi * blk + lax.broadcasted_iota(jnp.int32, (blk, blk), 0)
    k_off = lax.broadcasted_iota(jnp.int32, (blk, blk), 1)

    def body(kb, carry):
        m, l, acc = carry
        ks = pl.multiple_of(kb * blk, blk)
        k = k_ref[pl.ds(ks, blk), :].astype(BF16)
        v = v_ref[pl.ds(ks, blk), :].astype(BF16)
        s = lax.dot_general(q, k, (((1,), (1,)), ((), ())), preferred_element_type=F32)
        s = s + fq - fk_ref[kb]
        s = jnp.where(ks + k_off <= q_pos, s, NEG_INF)
        m_new = jnp.maximum(m, jnp.max(s, axis=-1, keepdims=True))
        alpha = jnp.exp(m - m_new)
        p = jnp.exp(s - m_new)
        l = alpha * l + jnp.sum(p, axis=-1, keepdims=True)
        acc = alpha * acc + jnp.dot(p.astype(BF16), v, preferred_element_type=F32)
        return m_new, l, acc

    init = (jnp.full((blk, 1), NEG_INF, F32), jnp.zeros((blk, 1), F32), jnp.zeros((blk, HEAD_DIM), F32))
    _, l, acc = lax.fori_loop(0, qi + 1, body, init)
    o_ref[...] = (acc / l).astype(o_ref.dtype)


def fox_prompt(z, cum_t, n_seq, seq_len):
    blk = 256
    nq = seq_len // blk
    tokens = z.shape[0]
    fq = cum_t.reshape(N_HEADS, tokens, 1)
    fk = cum_t.reshape(N_HEADS, tokens // blk, 1, blk)
    return pl.pallas_call(
        functools.partial(_fox_prompt_kernel, blk=blk),
        grid=(n_seq, N_HEADS, nq),
        in_specs=[pl.BlockSpec((blk, HEAD_DIM), lambda b, h, i: (b * nq + i, BQ_BLK + h)),
                  pl.BlockSpec((seq_len, HEAD_DIM), lambda b, h, i: (b, BK_BLK + h)),
                  pl.BlockSpec((seq_len, HEAD_DIM), lambda b, h, i: (b, BV_BLK + h)),
                  pl.BlockSpec((None, blk, 1), lambda b, h, i: (h, b * nq + i, 0)),
                  pl.BlockSpec((None, nq, 1, blk), lambda b, h, i: (h, b, 0, 0))],
        out_specs=pl.BlockSpec((blk, HEAD_DIM), lambda b, h, i: (b * nq + i, h)),
        out_shape=jax.ShapeDtypeStruct((tokens, WIDTH), BF16),
        compiler_params=_params("parallel", "parallel", "arbitrary"),
        name="fox_prompt",
    )(z, z, z, fq, fk)


def _fox_sample_kernel(pt_ref, qbd_ref, kn_ref, vn_ref, fcol_ref, frow_ref, *rest, pages, n_new):
    page_refs = rest[:3 * pages]
    o_ref, m_ref, l_ref, acc_ref, carry_ref = rest[3 * pages:]
    del pt_ref
    j = pl.program_id(1)
    rows = N_HEADS * n_new
    qbd = qbd_ref[...]
    fcol = fcol_ref[...]

    def update(s, v):
        m_old = m_ref[...]
        m_new = jnp.maximum(m_old, jnp.max(s, axis=-1, keepdims=True))
        alpha = jnp.exp(m_old - m_new)
        p = jnp.exp(s - m_new)
        l_ref[...] = alpha * l_ref[...] + jnp.sum(p, axis=-1, keepdims=True)
        acc_ref[...] = alpha * acc_ref[...] + jnp.dot(p.astype(BF16), v, preferred_element_type=F32)
        m_ref[...] = m_new

    @pl.when(j == 0)
    def _():
        m_ref[...] = jnp.full_like(m_ref, NEG_INF)
        l_ref[...] = jnp.zeros_like(l_ref)
        acc_ref[...] = jnp.zeros_like(acc_ref)
        carry_ref[...] = jnp.zeros_like(carry_ref)
        kn = kn_ref[...].astype(BF16)
        s = lax.dot_general(qbd, kn, (((1,), (1,)), ((), ())), preferred_element_type=F32)
        s = s + fcol - frow_ref[...]
        t_idx = lax.broadcasted_iota(jnp.int32, (rows, n_new), 0) % n_new
        s_idx = lax.broadcasted_iota(jnp.int32, (rows, n_new), 1)
        s = jnp.where(s_idx <= t_idx, s, NEG_INF)
        update(s, vn_ref[...].astype(BF16))

    lane = lax.broadcasted_iota(jnp.int32, (rows, PAGE_SIZE), 1)
    for p in range(pages):
        k_ref, v_ref, lp_ref = page_refs[3 * p:3 * p + 3]
        k = k_ref[...].astype(BF16)
        s = lax.dot_general(qbd, k, (((1,), (1,)), ((), ())), preferred_element_type=F32)
        lp = lp_ref[...]
        lp_rows = jnp.broadcast_to(lp[:, None, :], (N_HEADS, n_new, PAGE_SIZE)).reshape(rows, PAGE_SIZE)
        suf = lp_rows
        step = 1
        while step < PAGE_SIZE:
            shifted = pltpu.roll(suf, PAGE_SIZE - step, axis=1)
            suf = suf + jnp.where(lane + step < PAGE_SIZE, shifted, 0.0)
            step *= 2
        later = carry_ref[...]
        s = s + fcol + (suf - lp_rows) + later
        carry_ref[...] = later + suf[:, 0:1]
        update(s, v_ref[...].astype(BF16))

    @pl.when(j == pl.num_programs(1) - 1)
    def _():
        inv = 1.0 / l_ref[...]
        for h in range(N_HEADS):
            r = slice(h * n_new, (h + 1) * n_new)
            c = slice(h * HEAD_DIM, (h + 1) * HEAD_DIM)
            o_ref[:, c] = acc_ref[r, c] * inv[r]


def fox_sample(page_table, qbd, k_new, v_new, fcol, frow, cache_k, cache_v, cache_lp_t, layer):
    n_seq, n_pages = page_table.shape
    n_new = k_new.shape[1]
    rows = N_HEADS * n_new
    pages = 4
    steps = n_pages // pages

    def page_map(p):
        def index(b, j, pt):
            return (layer, pt[b, n_pages - 1 - (j * pages + p)], 0, 0)
        return index

    page_specs = []
    page_args = []
    for p in range(pages):
        page_specs += [pl.BlockSpec((None, None, PAGE_SIZE, WIDTH), page_map(p)),
                       pl.BlockSpec((None, None, PAGE_SIZE, WIDTH), page_map(p)),
                       pl.BlockSpec((None, None, N_HEADS, PAGE_SIZE), page_map(p))]
        page_args += [cache_k, cache_v, cache_lp_t]

    def per_seq(shape):
        return pl.BlockSpec((None,) + shape, lambda b, j, pt: (b, 0, 0))

    grid_spec = pltpu.PrefetchScalarGridSpec(
        num_scalar_prefetch=1,
        grid=(n_seq, steps),
        in_specs=[per_seq((rows, WIDTH)), per_seq((n_new, WIDTH)), per_seq((n_new, WIDTH)),
                  per_seq((rows, 1)), per_seq((rows, n_new))] + page_specs,
        out_specs=per_seq((n_new, WIDTH)),
        scratch_shapes=[pltpu.VMEM((rows, 1), F32), pltpu.VMEM((rows, 1), F32),
                        pltpu.VMEM((rows, WIDTH), F32), pltpu.VMEM((rows, 1), F32)],
    )
    return pl.pallas_call(
        functools.partial(_fox_sample_kernel, pages=pages, n_new=n_new),
        grid_spec=grid_spec,
        out_shape=jax.ShapeDtypeStruct((n_seq, n_new, WIDTH), F32),
        compiler_params=_params("arbitrary", "arbitrary"),
        name="fox_sample",
    )(page_table, qbd, k_new, v_new, fcol, frow, *page_args)


def _retention_kernel(q_ref, k_ref, v_ref, g_ref, cos_ref, sin_ref, dmat_ref, qdec_ref, kdec_ref,
                      cdec_ref, gn_ref, s0_ref, o_ref, s_ref, *, chunk, n_chunks, mm_dtype):
    dmat = dmat_ref[...]
    qdec = qdec_ref[...]
    kdec = kdec_ref[...]
    cdec = cdec_ref[...]
    gn = gn_ref[...]

    def rope(x, cos, sin):
        return x * cos + pltpu.roll(x, HEAD_DIM // 2, axis=1) * sin

    def body(i, state):
        r = pl.multiple_of(i * chunk, chunk)
        rows = pl.ds(r, chunk)
        cos = cos_ref[rows, :]
        sin = sin_ref[rows, :]
        q = rope(q_ref[rows, :], cos, sin)
        k = rope(k_ref[rows, :], cos, sin) * HEAD_DIM ** -0.5
        v = v_ref[rows, :].astype(mm_dtype)
        inner = lax.dot_general(q.astype(mm_dtype), k.astype(mm_dtype), (((1,), (1,)), ((), ())),
                                preferred_element_type=F32) * dmat
        o = (jnp.dot(inner.astype(mm_dtype), v, preferred_element_type=F32)
             + jnp.dot((q * qdec).astype(mm_dtype), state.astype(mm_dtype), preferred_element_type=F32))
        kd_t = (k * kdec).T.astype(mm_dtype)
        state = state * cdec + jnp.dot(kd_t, v, preferred_element_type=F32)
        oc = o - jnp.mean(o, axis=-1, keepdims=True)
        y = oc * lax.rsqrt(jnp.mean(oc * oc, axis=-1, keepdims=True) + EPS) * gn
        gate = g_ref[rows, :]
        o_ref[rows, :] = (gate * jax.nn.sigmoid(gate) * y).astype(o_ref.dtype)
        return state

    s_ref[...] = lax.fori_loop(0, n_chunks, body, s0_ref[...])


def retention(z, s0, tabs, gn_g, n_seq, seq_len, out_dtype):
    cos, sin, dmat, qdec, kdec, cdec = tabs
    chunk = dmat.shape[-1]
    tokens = z.shape[0]

    def col(blk):
        return pl.BlockSpec((seq_len, HEAD_DIM), lambda b, h: (b, blk + h))

    def per_head(shape):
        return pl.BlockSpec((None,) + shape, lambda b, h: (h, 0, 0))

    table = pl.BlockSpec((seq_len, HEAD_DIM), lambda b, h: (0, 0))
    state = pl.BlockSpec((None, None, HEAD_DIM, HEAD_DIM), lambda b, h: (b, h, 0, 0))
    kern = functools.partial(_retention_kernel, chunk=chunk, n_chunks=seq_len // chunk,
                             mm_dtype=BF16 if chunk >= 16 else F32)
    return pl.pallas_call(
        kern,
        grid=(n_seq, N_HEADS),
        in_specs=[col(CQ_BLK), col(CK_BLK), col(CV_BLK), col(CG_BLK), table, table,
                  per_head((chunk, chunk)), per_head((chunk, HEAD_DIM)), per_head((chunk, HEAD_DIM)),
                  per_head((1, HEAD_DIM)), pl.BlockSpec((1, HEAD_DIM), lambda b, h: (0, h)), state],
        out_specs=[pl.BlockSpec((seq_len, HEAD_DIM), lambda b, h: (b, h)), state],
        out_shape=[jax.ShapeDtypeStruct((tokens, WIDTH), out_dtype),
                   jax.ShapeDtypeStruct((n_seq, N_HEADS, HEAD_DIM, HEAD_DIM), F32)],
        compiler_params=_params("parallel", "parallel"),
        name="retention",
    )(z, z, z, z, cos, sin, dmat, qdec, kdec, cdec, gn_g.reshape(1, WIDTH), s0)


def retention_tables(pos, chunk):
    half = HEAD_DIM // 2
    inv = ROPE_BASE ** (-jnp.arange(half, dtype=F32) / half)
    ang = pos.astype(F32)[:, None] * inv[None, :]
    cos = jnp.concatenate([jnp.cos(ang), jnp.cos(ang)], axis=-1)
    sin = jnp.concatenate([-jnp.sin(ang), jnp.sin(ang)], axis=-1)
    lg = jnp.log1p(-jnp.exp2(-5.0 - jnp.arange(N_HEADS, dtype=F32)))
    idx = jnp.arange(chunk, dtype=F32)
    diff = idx[:, None] - idx[None, :]
    dmat = jnp.where(diff >= 0, jnp.exp(jnp.maximum(diff, 0.0)[None] * lg[:, None, None]), 0.0)
    qdec = jnp.exp((idx[None, :] + 1.0) * lg[:, None])
    kdec = jnp.exp((chunk - 1.0 - idx)[None, :] * lg[:, None])
    cdec = jnp.exp(chunk * lg)
    lanes = (N_HEADS, chunk, HEAD_DIM)
    return (cos, sin, dmat, jnp.broadcast_to(qdec[:, :, None], lanes),
            jnp.broadcast_to(kdec[:, :, None], lanes),
            jnp.broadcast_to(cdec[:, None, None], (N_HEADS, 1, HEAD_DIM)))


HALO = 8


def _conv3(stage_ref, cur, w_ref):
    rows = cur.shape[0]
    stage_ref[HALO:HALO + rows, :] = cur
    return (w_ref[2:3, :] * cur + w_ref[1:2, :] * stage_ref[HALO - 1:HALO - 1 + rows, :]
            + w_ref[0:1, :] * stage_ref[HALO - 2:HALO - 2 + rows, :])


def _short_conv_kernel(bg_ref, cg_ref, hd_ref, w_ref, hist_ref, o_ref, hist_out_ref, stage_ref, *, rows):
    i = pl.program_id(1)

    @pl.when(i == 0)
    def _():
        stage_ref[HALO - 2:HALO, :] = hist_ref[...]

    cur = cg_ref[...] * hd_ref[...]
    y = _conv3(stage_ref, cur, w_ref)
    o_ref[...] = (bg_ref[...] * y).astype(o_ref.dtype)
    tail = cur[rows - HALO:, :]
    stage_ref[0:HALO, :] = tail
    hist_out_ref[...] = tail[HALO - 2:, :]


def short_conv(z, w_conv, hist, n_seq, seq_len, rows, out_dtype):
    tokens = z.shape[0]
    per = seq_len // rows

    def col(blk):
        return pl.BlockSpec((rows, WIDTH), lambda b, i: (b * per + i, blk))

    hist_spec = pl.BlockSpec((None, CONV_WIDTH - 1, WIDTH), lambda b, i: (b, 0, 0))
    return pl.pallas_call(
        functools.partial(_short_conv_kernel, rows=rows),
        grid=(n_seq, per),
        in_specs=[col(9), col(10), col(11), pl.BlockSpec((CONV_WIDTH, WIDTH), lambda b, i: (0, 0)), hist_spec],
        out_specs=[pl.BlockSpec((rows, WIDTH), lambda b, i: (b * per + i, 0)), hist_spec],
        out_shape=[jax.ShapeDtypeStruct((tokens, WIDTH), out_dtype),
                   jax.ShapeDtypeStruct((n_seq, CONV_WIDTH - 1, WIDTH), F32)],
        scratch_shapes=[pltpu.VMEM((HALO + rows, WIDTH), F32)],
        compiler_params=_params("arbitrary", "arbitrary"),
        name="short_conv",
    )(z, z, z, w_conv, hist)


def _merge_kernel(h_ref, wm_ref, bm_ref, o_ref, wb_ref, out_ref, acc_ref):
    n = pl.program_id(2)
    gate = jnp.dot(h_ref[...], wm_ref[...], preferred_element_type=F32) + bm_ref[...]
    proj = jnp.dot(o_ref[...].astype(BF16), wb_ref[...], preferred_element_type=F32)
    term = jax.nn.sigmoid(gate) * proj

    @pl.when(n == 0)
    def _():
        acc_ref[...] = term

    @pl.when(n > 0)
    def _():
        acc_ref[...] += term

    @pl.when(n == pl.num_programs(2) - 1)
    def _():
        out_ref[...] = acc_ref[...].astype(out_ref.dtype)


def gated_merge(h, branches, w_merge, b_merge, w_branch, bm, bn):
    nb, tokens, _ = branches.shape
    return pl.pallas_call(
        _merge_kernel,
        grid=(tokens // bm, D_MODEL // bn, nb),
        in_specs=[pl.BlockSpec((bm, D_MODEL), lambda i, j, n: (i, 0)),
                  pl.BlockSpec((None, D_MODEL, bn), lambda i, j, n: (n, 0, j)),
                  pl.BlockSpec((None, 1, bn), lambda i, j, n: (n, 0, j)),
                  pl.BlockSpec((None, bm, WIDTH), lambda i, j, n: (n, i, 0)),
                  pl.BlockSpec((None, WIDTH, bn), lambda i, j, n: (n, 0, j))],
        out_specs=pl.BlockSpec((bm, bn), lambda i, j, n: (i, j)),
        out_shape=jax.ShapeDtypeStruct((tokens, D_MODEL), BF16),
        scratch_shapes=[pltpu.VMEM((bm, bn), F32)],
        compiler_params=_params("parallel", "arbitrary", "arbitrary"),
        name="gated_merge",
    )(h, w_merge, b_merge.reshape(nb, 1, D_MODEL), branches, w_branch)


def _ffn_up_kernel(h_ref, wa_ref, wb_ref, cwa_ref, cwb_ref, ba_ref, bb_ref, ha_ref, hb_ref,
                   act_ref, ta_ref, tb_ref, sa_ref, sb_ref, ca_ref, cb_ref, *, rows, per_seq):
    i = pl.program_id(0)
    j = pl.program_id(1)
    first = (i % per_seq) == 0
    h = h_ref[...]

    def half(w_ref, cw_ref, hist_ref, stage_ref, carry_ref, tail_ref):
        up = jnp.dot(h, w_ref[...], preferred_element_type=F32)

        @pl.when(first)
        def _():
            stage_ref[HALO - 2:HALO, :] = hist_ref[...]

        @pl.when(jnp.logical_not(first))
        def _():
            stage_ref[0:HALO, :] = carry_ref[j]

        y = _conv3(stage_ref, up, cw_ref)
        tail = up[rows - HALO:, :]
        carry_ref[j] = tail
        tail_ref[...] = tail
        return y

    a = half(wa_ref, cwa_ref, ha_ref, sa_ref, ca_ref, ta_ref) + ba_ref[...]
    b = half(wb_ref, cwb_ref, hb_ref, sb_ref, cb_ref, tb_ref) + bb_ref[...]
    act_ref[...] = (a * jax.nn.sigmoid(a) * b).astype(act_ref.dtype)


def ffn_up(h, wa, wb, cwa, cwb, ba, bb, hist_a, hist_b, n_seq, seq_len, bm, bn):
    tokens = h.shape[0]
    per_seq = seq_len // bm
    nj = D_FF_PAD // bn
    wspec = pl.BlockSpec((D_MODEL, bn), lambda i, j: (0, j))
    cspec = pl.BlockSpec((CONV_WIDTH, bn), lambda i, j: (0, j))
    bspec = pl.BlockSpec((1, bn), lambda i, j: (0, j))
    hspec = pl.BlockSpec((None, CONV_WIDTH - 1, bn), lambda i, j: (i // per_seq, 0, j))
    tspec = pl.BlockSpec((None, HALO, bn), lambda i, j: (i, 0, j))
    tail_shape = jax.ShapeDtypeStruct((tokens // bm, HALO, D_FF_PAD), F32)
    act, tail_a, tail_b = pl.pallas_call(
        functools.partial(_ffn_up_kernel, rows=bm, per_seq=per_seq),
        grid=(tokens // bm, nj),
        in_specs=[pl.BlockSpec((bm, D_MODEL), lambda i, j: (i, 0)), wspec, wspec, cspec, cspec,
                  bspec, bspec, hspec, hspec],
        out_specs=[pl.BlockSpec((bm, bn), lambda i, j: (i, j)), tspec, tspec],
        out_shape=[jax.ShapeDtypeStruct((tokens, D_FF_PAD), BF16), tail_shape, tail_shape],
        scratch_shapes=[pltpu.VMEM((HALO + bm, bn), F32), pltpu.VMEM((HALO + bm, bn), F32),
                        pltpu.VMEM((nj, HALO, bn), F32), pltpu.VMEM((nj, HALO, bn), F32)],
        compiler_params=_params("arbitrary", "arbitrary"),
        name="ffn_up",
    )(h, wa, wb, cwa, cwb, ba, bb, hist_a, hist_b)
    return act, tail_a[per_seq - 1::per_seq], tail_b[per_seq - 1::per_seq]


def _ffn_act_kernel(ua_ref, ub_ref, cwa_ref, cwb_ref, ba_ref, bb_ref, ha_ref, hb_ref, act_ref,
                    sa_ref, sb_ref):
    sa_ref[HALO - 2:HALO, :] = ha_ref[...]
    sb_ref[HALO - 2:HALO, :] = hb_ref[...]
    a = _conv3(sa_ref, ua_ref[...], cwa_ref) + ba_ref[...]
    b = _conv3(sb_ref, ub_ref[...], cwb_ref) + bb_ref[...]
    act_ref[...] = (a * jax.nn.sigmoid(a) * b).astype(act_ref.dtype)


def ffn_act(up, cwa, cwb, ba, bb, hist_a, hist_b, n_seq, seq_len, bn):
    tokens = up.shape[0]
    nj = D_FF_PAD // bn
    cspec = pl.BlockSpec((CONV_WIDTH, bn), lambda b, j: (0, j))
    bspec = pl.BlockSpec((1, bn), lambda b, j: (0, j))
    hspec = pl.BlockSpec((None, CONV_WIDTH - 1, bn), lambda b, j: (b, 0, j))
    return pl.pallas_call(
        _ffn_act_kernel,
        grid=(n_seq, nj),
        in_specs=[pl.BlockSpec((seq_len, bn), lambda b, j: (b, j)),
                  pl.BlockSpec((seq_len, bn), lambda b, j: (b, nj + j)),
                  cspec, cspec, bspec, bspec, hspec, hspec],
        out_specs=pl.BlockSpec((seq_len, bn), lambda b, j: (b, j)),
        out_shape=jax.ShapeDtypeStruct((tokens, D_FF_PAD), F32),
        scratch_shapes=[pltpu.VMEM((HALO + seq_len, bn), F32), pltpu.VMEM((HALO + seq_len, bn), F32)],
        compiler_params=_params("parallel", "parallel"),
        name="ffn_act",
    )(up, up, cwa, cwb, ba, bb, hist_a, hist_b)


def _pad_cols(a, n):
    return jnp.pad(a, ((0, 0), (0, n - a.shape[1])))


def layer_weights(l, w_in, w_merge, w_branch, w_out, w_up, w_ffn_conv, b_ffn_conv, w_down):
    w_in_l = w_in[l]
    wz = jnp.concatenate([w_in_l[:, :FORGET_COL], w_in_l[:, FORGET_COL + N_HEADS:]], axis=1).astype(BF16)
    wf_t = jnp.pad(w_in_l[:, FORGET_COL:FORGET_COL + N_HEADS].T, ((0, 16 - N_HEADS), (0, 0))).astype(BF16)
    wa = _pad_cols(w_up[l][:, :D_FF], D_FF_PAD).astype(BF16)
    wb = _pad_cols(w_up[l][:, D_FF:], D_FF_PAD).astype(BF16)
    cw = w_ffn_conv[l]
    bias = b_ffn_conv[l].reshape(1, 2 * D_FF)
    return dict(
        wz=wz, wf_t=wf_t,
        w_merge=w_merge[l].astype(BF16), w_branch=w_branch[l].astype(BF16), w_out=w_out[l].astype(BF16),
        wa=wa, wb=wb,
        cwa=_pad_cols(cw[:, :D_FF], D_FF_PAD), cwb=_pad_cols(cw[:, D_FF:], D_FF_PAD),
        ba=_pad_cols(bias[:, :D_FF], D_FF_PAD), bb=_pad_cols(bias[:, D_FF:], D_FF_PAD),
        w_down=jnp.pad(w_down[l], ((0, D_FF_PAD - D_FF), (0, 0))).astype(BF16),
    )


def _split_hist(hist):
    pad = ((0, 0), (0, 0), (0, D_FF_PAD - D_FF))
    return jnp.pad(hist[..., :D_FF], pad), jnp.pad(hist[..., D_FF:], pad)


def _join_hist(tail_a, tail_b):
    return jnp.concatenate([tail_a[..., :D_FF], tail_b[..., :D_FF]], axis=-1)


def mixer_and_ffn(grp, x, h, mod_op, lw, small, tabs, attend, ret_s0, conv_hist, ffn_hist, next_pre,
                  next_mod_op):
    n_seq, seq_len, tokens = grp.n_seq, grp.seq_len, grp.tokens
    bmm = grp.mm_block
    act_dtype = BF16 if seq_len >= 16 else F32

    z = matmul(h, lw["wz"], F32, bmm, 1024)
    logf_t, cum_t = forget_logits(grp, h, lw["wf_t"], small["b_forget"])

    chunk_rows = min(seq_len, CHUNK)
    n_chunks = seq_len // chunk_rows
    o_a, v_tail = chunk_gating(z, small["a_ln_g"], small["a_ws"][:, :chunk_rows, :chunk_rows],
                               jnp.broadcast_to(small["a_bs"][:, :chunk_rows, None],
                                                (N_HEADS, chunk_rows, HEAD_DIM)),
                               n_seq, n_chunks, chunk_rows, act_dtype)
    o_b = attend(z, cum_t)
    o_c, ret_state = retention(z, ret_s0, tabs, small["ret_gn_g"], n_seq, seq_len, act_dtype)
    conv_rows = min(seq_len, 512)
    o_d, conv_state = short_conv(z, small["w_sc_conv"], conv_hist, n_seq, seq_len, conv_rows, act_dtype)

    branches = jnp.stack([o_a, o_b.astype(act_dtype), o_c, o_d], axis=0)
    merged = gated_merge(h, branches, lw["w_merge"], small["b_merge"], lw["w_branch"], bmm, 512)
    y = matmul(merged, lw["w_out"], F32, bmm, 1024)
    x, h2 = residual(grp, x, y, small["g_post_mix"], mod_op, 2, small["g_pre_ffn"], mod_op, 3, 4)

    hist_a, hist_b = _split_hist(ffn_hist)
    if seq_len >= bmm:
        act, tail_a, tail_b = ffn_up(h2, lw["wa"], lw["wb"], lw["cwa"], lw["cwb"], lw["ba"], lw["bb"],
                                     hist_a, hist_b, n_seq, seq_len, bmm, 512)
        ffn_state = _join_hist(tail_a[:, HALO - 2:], tail_b[:, HALO - 2:])
    else:
        up = matmul(h2, jnp.concatenate([lw["wa"], lw["wb"]], axis=1), F32, bmm, 1024)
        act = ffn_act(up, lw["cwa"], lw["cwb"], lw["ba"], lw["bb"], hist_a, hist_b, n_seq, seq_len, 2816)
        up3 = up.reshape(n_seq, seq_len, 2 * D_FF_PAD)[:, seq_len - 2:]
        ffn_state = _join_hist(up3[..., :D_FF_PAD], up3[..., D_FF_PAD:])
    y2 = matmul_ksplit(act, lw["w_down"], bmm, 1024, 2816)
    if next_pre is None:
        x = residual(grp, x, y2, small["g_post_ffn"], mod_op, 5)
        h_next = None
    else:
        x, h_next = residual(grp, x, y2, small["g_post_ffn"], mod_op, 5, next_pre, next_mod_op, 0, 1)

    k_b = z[:, BK_BLK * HEAD_DIM:BV_BLK * HEAD_DIM].reshape(n_seq, seq_len, N_HEADS, HEAD_DIM)
    v_b = z[:, BV_BLK * HEAD_DIM:CQ_BLK * HEAD_DIM].reshape(n_seq, seq_len, N_HEADS, HEAD_DIM)
    logf = logf_t.T.reshape(n_seq, seq_len, N_HEADS)
    return x, h_next, (k_b, v_b, logf, ret_state, conv_state, v_tail, ffn_state)


def kernel(x_prompt, x_sample, cache_k, cache_v, cache_logf, state_ret, state_conv, state_ffn_conv,
           page_table, c_prompt, c_sample, w_ada, b_ada, g_pre_mix, g_post_mix, g_pre_ffn, g_post_ffn,
           w_in, b_forget, a_ln_g, a_ws, a_bs, ret_gn_g, w_sc_conv, w_branch, w_merge, b_merge, w_out,
           w_up, w_ffn_conv, b_ffn_conv, w_down):
    depth = w_in.shape[0]
    bp, lp, _ = x_prompt.shape
    bs, ls, _ = x_sample.shape
    grp_p = Group(bp, lp, row_block=256, mm_block=1024)
    grp_s = Group(bs, ls, row_block=bs * ls, mm_block=bs * ls)

    c_all = jnp.pad(jnp.concatenate([c_prompt, c_sample], axis=0), ((0, 16 - bp - bs), (0, 0)))
    mod = ada_modulation(c_all, w_ada, b_ada)

    tabs_p = retention_tables(jnp.arange(lp), math.gcd(lp, CHUNK))
    tabs_s = retention_tables(PAST_LEN + jnp.arange(ls), math.gcd(ls, CHUNK))

    n_pool = cache_k.shape[1]
    cache_k2 = cache_k.reshape(depth, n_pool, PAGE_SIZE, WIDTH)
    cache_v2 = cache_v.reshape(depth, n_pool, PAGE_SIZE, WIDTH)
    cache_lp_t = cache_logf.transpose(0, 1, 3, 2)

    xp = x_prompt.reshape(grp_p.tokens, D_MODEL)
    xs = x_sample.reshape(grp_s.tokens, D_MODEL)
    mods_p = [grp_p.mod_operand(mod[l, :bp]) for l in range(depth)]
    mods_s = [grp_s.mod_operand(mod[l, bp:bp + bs]) for l in range(depth)]
    hp = prenorm(grp_p, xp, g_pre_mix[0], mods_p[0], 0, 1)
    hs = prenorm(grp_s, xs, g_pre_mix[0], mods_s[0], 0, 1)

    zeros_ret = jnp.zeros((bp, N_HEADS, HEAD_DIM, HEAD_DIM), F32)
    zeros_conv = jnp.zeros((bp, CONV_WIDTH - 1, WIDTH), F32)
    zeros_ffn = jnp.zeros((bp, CONV_WIDTH - 1, 2 * D_FF), F32)
    eye = jnp.eye(N_HEADS, dtype=F32)

    st_p, st_s = [], []
    for l in range(depth):
        lw = layer_weights(l, w_in, w_merge, w_branch, w_out, w_up, w_ffn_conv, b_ffn_conv, w_down)
        small = dict(b_forget=b_forget[l], a_ln_g=a_ln_g[l], a_ws=a_ws[l], a_bs=a_bs[l],
                     ret_gn_g=ret_gn_g[l], w_sc_conv=w_sc_conv[l], b_merge=b_merge[l],
                     g_post_mix=g_post_mix[l], g_pre_ffn=g_pre_ffn[l], g_post_ffn=g_post_ffn[l])
        next_pre = g_pre_mix[l + 1] if l + 1 < depth else None

        def attend_p(z, cum_t):
            return fox_prompt(z, cum_t, bp, lp)

        def attend_s(z, cum_t, l=l):
            zs = z.reshape(bs, ls, Z_COLS)
            q = zs[..., BQ_BLK * HEAD_DIM:BK_BLK * HEAD_DIM].reshape(bs, ls, N_HEADS, HEAD_DIM)
            q = q.transpose(0, 2, 1, 3) * HEAD_DIM ** -0.5
            qbd = (q[:, :, :, None, :] * eye[None, :, None, :, None]).reshape(bs, N_HEADS * ls, WIDTH)
            k_new = zs[..., BK_BLK * HEAD_DIM:BV_BLK * HEAD_DIM]
            v_new = zs[..., BV_BLK * HEAD_DIM:CQ_BLK * HEAD_DIM]
            cum = cum_t.reshape(N_HEADS, bs, ls).transpose(1, 0, 2)
            fcol = cum.reshape(bs, N_HEADS * ls, 1)
            frow = jnp.broadcast_to(cum[:, :, None, :], (bs, N_HEADS, ls, ls)).reshape(bs, N_HEADS * ls, ls)
            o = fox_sample(page_table, qbd.astype(BF16), k_new, v_new, fcol, frow,
                           cache_k2, cache_v2, cache_lp_t, l)
            return o.reshape(bs * ls, WIDTH)

        xp, hp, sp = mixer_and_ffn(grp_p, xp, hp, mods_p[l], lw, small, tabs_p, attend_p,
                                   zeros_ret, zeros_conv, zeros_ffn, next_pre,
                                   mods_p[l + 1] if l + 1 < depth else None)
        xs, hs, ss = mixer_and_ffn(grp_s, xs, hs, mods_s[l], lw, small, tabs_s, attend_s,
                                   state_ret[l], state_conv[l], state_ffn_conv[l], next_pre,
                                   mods_s[l + 1] if l + 1 < depth else None)
        st_p.append(sp)
        st_s.append(ss)

    def stack(states, idx):
        return jnp.stack([s[idx] for s in states], axis=0)

    return (xp.reshape(bp, lp, D_MODEL), xs.reshape(bs, ls, D_MODEL),
            stack(st_p, 0), stack(st_p, 1), stack(st_p, 2),
            stack(st_s, 0), stack(st_s, 1), stack(st_s, 2),
            stack(st_p, 3), stack(st_s, 3), stack(st_p, 4), stack(st_s, 4),
            stack(st_p, 5), stack(st_s, 5), stack(st_p, 6), stack(st_s, 6))
```

```python
import functools
import math

import jax
import jax.numpy as jnp
from jax import lax
from jax.experimental import pallas as pl
from jax.experimental.pallas import tpu as pltpu

D_MODEL = 4096
HEAD_DIM = 128
N_HEADS = 8
WIDTH = N_HEADS * HEAD_DIM
CHUNK = 128
PAST_LEN = 16384
PAGE_SIZE = 128
ROPE_BASE = 10000.0
CONV_WIDTH = 3
D_FF = 11008
D_FF_PAD = 11264
N_MOD = 6
EPS = 1e-6
NEG_INF = -1e30
Z_COLS = 12 * WIDTH
FORGET_COL = 5 * WIDTH

BQ_BLK, BK_BLK, BV_BLK = 16, 24, 32
CQ_BLK, CK_BLK, CV_BLK, CG_BLK = 40, 48, 56, 64

VMEM_LIMIT_BYTES = 56 * 1024 * 1024

BF16 = jnp.bfloat16
F32 = jnp.float32


def _params(*semantics):
    return pltpu.CompilerParams(dimension_semantics=semantics, vmem_limit_bytes=VMEM_LIMIT_BYTES)


def _ada_kernel(c_ref, w_ref, b_ref, o_ref):
    c = c_ref[...]
    s = (c * jax.nn.sigmoid(c)).astype(BF16)
    o_ref[...] = jnp.dot(s, w_ref[...].astype(BF16), preferred_element_type=F32) + b_ref[...]


def ada_modulation(c_all, w_ada, b_ada):
    depth, d, n = w_ada.shape
    rows = c_all.shape[0]
    bn = 512
    return pl.pallas_call(
        _ada_kernel,
        grid=(depth, n // bn),
        in_specs=[
            pl.BlockSpec((rows, d), lambda l, j: (0, 0)),
            pl.BlockSpec((None, d, bn), lambda l, j: (l, 0, j)),
            pl.BlockSpec((None, 1, bn), lambda l, j: (l, 0, j)),
        ],
        out_specs=pl.BlockSpec((None, rows, bn), lambda l, j: (l, 0, j)),
        out_shape=jax.ShapeDtypeStruct((depth, rows, n), F32),
        compiler_params=_params("arbitrary", "arbitrary"),
        name="ada_modulation",
    )(c_all, w_ada, b_ada.reshape(depth, 1, n))


class Group:
    def __init__(self, n_seq, seq_len, row_block, mm_block):
        self.n_seq = n_seq
        self.seq_len = seq_len
        self.tokens = n_seq * seq_len
        self.row_block = row_block
        self.mm_block = mm_block
        self.per_seq_rows = row_block <= seq_len

    def mod_operand(self, mod):
        if self.per_seq_rows:
            return mod.reshape(self.n_seq, N_MOD, 1, D_MODEL)
        tok = jnp.repeat(mod.reshape(self.n_seq, N_MOD, D_MODEL), self.seq_len, axis=0)
        return tok.transpose(1, 0, 2)

    def mod_spec(self, idx):
        if self.per_seq_rows:
            per = self.seq_len // self.row_block
            return pl.BlockSpec((None, None, 1, D_MODEL), lambda i: (i // per, idx, 0, 0))
        return pl.BlockSpec((None, self.row_block, D_MODEL), lambda i: (idx, i, 0))


def _rms(x, g):
    return x * lax.rsqrt(jnp.mean(x * x, axis=-1, keepdims=True) + EPS) * g


def _prenorm_kernel(x_ref, g_ref, sc_ref, sh_ref, h_ref):
    h_ref[...] = (_rms(x_ref[...], g_ref[...]) * (1.0 + sc_ref[...]) + sh_ref[...]).astype(h_ref.dtype)


def prenorm(grp, x, g, mod_op, sh_idx, sc_idx):
    br = grp.row_block
    row = pl.BlockSpec((br, D_MODEL), lambda i: (i, 0))
    vec = pl.BlockSpec((1, D_MODEL), lambda i: (0, 0))
    return pl.pallas_call(
        _prenorm_kernel,
        grid=(grp.tokens // br,),
        in_specs=[row, vec, grp.mod_spec(sc_idx), grp.mod_spec(sh_idx)],
        out_specs=row,
        out_shape=jax.ShapeDtypeStruct((grp.tokens, D_MODEL), BF16),
        compiler_params=_params("parallel"),
        name="prenorm",
    )(x, g.reshape(1, D_MODEL), mod_op, mod_op)


def _residual_kernel(x_ref, y_ref, gpost_ref, gate_ref, xo_ref):
    xo_ref[...] = x_ref[...] + gate_ref[...] * _rms(y_ref[...], gpost_ref[...])


def _residual_prenorm_kernel(x_ref, y_ref, gpost_ref, gate_ref, gpre_ref, sc_ref, sh_ref, xo_ref, h_ref):
    xn = x_ref[...] + gate_ref[...] * _rms(y_ref[...], gpost_ref[...])
    xo_ref[...] = xn
    h_ref[...] = (_rms(xn, gpre_ref[...]) * (1.0 + sc_ref[...]) + sh_ref[...]).astype(h_ref.dtype)


def residual(grp, x, y, g_post, mod_op, gate_idx, g_pre=None, pre_mod_op=None, sh_idx=None, sc_idx=None):
    br = grp.row_block
    row = pl.BlockSpec((br, D_MODEL), lambda i: (i, 0))
    vec = pl.BlockSpec((1, D_MODEL), lambda i: (0, 0))
    x_shape = jax.ShapeDtypeStruct((grp.tokens, D_MODEL), F32)
    if g_pre is None:
        return pl.pallas_call(
            _residual_kernel,
            grid=(grp.tokens // br,),
            in_specs=[row, row, vec, grp.mod_spec(gate_idx)],
            out_specs=row,
            out_shape=x_shape,
            compiler_params=_params("parallel"),
            name="residual",
        )(x, y, g_post.reshape(1, D_MODEL), mod_op)
    return pl.pallas_call(
        _residual_prenorm_kernel,
        grid=(grp.tokens // br,),
        in_specs=[row, row, vec, grp.mod_spec(gate_idx), vec, grp.mod_spec(sc_idx), grp.mod_spec(sh_idx)],
        out_specs=[row, row],
        out_shape=[x_shape, jax.ShapeDtypeStruct((grp.tokens, D_MODEL), BF16)],
        compiler_params=_params("parallel"),
        name="residual_prenorm",
    )(x, y, g_post.reshape(1, D_MODEL), mod_op, g_pre.reshape(1, D_MODEL), pre_mod_op, pre_mod_op)


def _mm_kernel(x_ref, w_ref, o_ref):
    o_ref[...] = jnp.dot(x_ref[...].astype(BF16), w_ref[...],
                         preferred_element_type=F32).astype(o_ref.dtype)


def matmul(x, w, layer, out_dtype, bm, bn):
    m, k = x.shape
    n = w.shape[2]
    return pl.pallas_call(
        _mm_kernel,
        grid=(m // bm, n // bn),
        in_specs=[pl.BlockSpec((bm, k), lambda i, j: (i, 0)),
                  pl.BlockSpec((None, k, bn), lambda i, j: (layer, 0, j))],
        out_specs=pl.BlockSpec((bm, bn), lambda i, j: (i, j)),
        out_shape=jax.ShapeDtypeStruct((m, n), out_dtype),
        compiler_params=_params("parallel", "arbitrary"),
        name="matmul",
    )(x, w)


def _mm_ksplit_kernel(x_ref, w_ref, o_ref):
    @pl.when(pl.program_id(2) == 0)
    def _():
        o_ref[...] = jnp.zeros_like(o_ref)

    o_ref[...] += jnp.dot(x_ref[...].astype(BF16), w_ref[...], preferred_element_type=F32)


def matmul_ksplit(x, w, layer, bm, bn, bk):
    m, k = x.shape
    n = w.shape[2]
    return pl.pallas_call(
        _mm_ksplit_kernel,
        grid=(m // bm, n // bn, k // bk),
        in_specs=[pl.BlockSpec((bm, bk), lambda i, j, kk: (i, kk)),
                  pl.BlockSpec((None, bk, bn), lambda i, j, kk: (layer, kk, j))],
        out_specs=pl.BlockSpec((bm, bn), lambda i, j, kk: (i, j)),
        out_shape=jax.ShapeDtypeStruct((m, n), F32),
        compiler_params=_params("parallel", "parallel", "arbitrary"),
        name="matmul_ksplit",
    )(x, w)


def _logf_kernel(h_ref, wf_ref, bf_ref, logf_ref, cum_ref, carry_ref, *, tile, seq_len, cw):
    i = pl.program_id(0)
    logits = lax.dot_general(wf_ref[...], h_ref[...], (((1,), (1,)), ((), ())),
                             preferred_element_type=F32)[:N_HEADS]
    x = logits + bf_ref[...]
    logf = jnp.minimum(x, 0.0) - jnp.log1p(jnp.exp(-jnp.abs(x)))
    logf_ref[...] = logf

    seg = min(seq_len, cw)
    src = lax.broadcasted_iota(jnp.int32, (cw, cw), 0)
    dst = lax.broadcasted_iota(jnp.int32, (cw, cw), 1)
    tri = ((src <= dst) & (src // seg == dst // seg)).astype(F32)

    if seq_len > tile:
        @pl.when((i * tile) % seq_len == 0)
        def _():
            carry_ref[...] = jnp.zeros_like(carry_ref)
        carry = carry_ref[...]
    else:
        carry = jnp.zeros((N_HEADS, 1), F32)

    for c in range(tile // cw):
        part = jnp.dot(logf[:, c * cw:(c + 1) * cw], tri, precision=lax.Precision.HIGHEST,
                       preferred_element_type=F32)
        cum = part + carry
        cum_ref[:, c * cw:(c + 1) * cw] = cum
        if seq_len > cw:
            carry = cum[:, cw - 1:cw]
    if seq_len > tile:
        carry_ref[...] = carry


def forget_logits(grp, h, wf_t, b_forget):
    tile = min(512, grp.tokens)
    cw = min(128, tile)
    kern = functools.partial(_logf_kernel, tile=tile, seq_len=grp.seq_len, cw=cw)
    out = jax.ShapeDtypeStruct((N_HEADS, grp.tokens), F32)
    return pl.pallas_call(
        kern,
        grid=(grp.tokens // tile,),
        in_specs=[pl.BlockSpec((tile, D_MODEL), lambda i: (i, 0)),
                  pl.BlockSpec((16, D_MODEL), lambda i: (0, 0)),
                  pl.BlockSpec((N_HEADS, 1), lambda i: (0, 0))],
        out_specs=[pl.BlockSpec((N_HEADS, tile), lambda i: (0, i))] * 2,
        out_shape=[out, out],
        scratch_shapes=[pltpu.VMEM((N_HEADS, 1), F32)],
        compiler_params=_params("arbitrary"),
        name="forget_logits",
    )(h, wf_t, b_forget.reshape(N_HEADS, 1))


def _gating_kernel(z_ref, lng_ref, ws_ref, bs_ref, o_ref, vt_ref, *, rows, mm_dtype):
    z = jax.nn.gelu(z_ref[...])
    u = z[:, :WIDTH]
    v = z[:, WIDTH:]
    vc = v - jnp.mean(v, axis=-1, keepdims=True)
    vn = vc * lax.rsqrt(jnp.mean(vc * vc, axis=-1, keepdims=True) + EPS) * lng_ref[...]
    vt_ref[...] = vn
    t_idx = lax.broadcasted_iota(jnp.int32, (rows, rows), 0)
    s_idx = lax.broadcasted_iota(jnp.int32, (rows, rows), 1)
    causal = s_idx <= t_idx
    for g in range(N_HEADS):
        cols = slice(g * HEAD_DIM, (g + 1) * HEAD_DIM)
        ws = jnp.where(causal, ws_ref[g], 0.0).astype(mm_dtype)
        mixed = jnp.dot(ws, vn[:, cols].astype(mm_dtype), preferred_element_type=F32) + bs_ref[g]
        o_ref[:, cols] = (u[:, cols] * mixed).astype(o_ref.dtype)


def chunk_gating(z, ln_g, ws, bs_b, n_seq, n_chunks, rows, out_dtype):
    tokens = z.shape[0]
    kern = functools.partial(_gating_kernel, rows=rows, mm_dtype=BF16 if rows >= 16 else F32)
    return pl.pallas_call(
        kern,
        grid=(n_seq, n_chunks),
        in_specs=[pl.BlockSpec((rows, 2 * WIDTH), lambda b, n: (b * n_chunks + n, 0)),
                  pl.BlockSpec((1, WIDTH), lambda b, n: (0, 0)),
                  pl.BlockSpec((N_HEADS, rows, rows), lambda b, n: (0, 0, 0)),
                  pl.BlockSpec((N_HEADS, rows, HEAD_DIM), lambda b, n: (0, 0, 0))],
        out_specs=[pl.BlockSpec((rows, WIDTH), lambda b, n: (b * n_chunks + n, 0)),
                   pl.BlockSpec((None, rows, WIDTH), lambda b, n: (b, 0, 0))],
        out_shape=[jax.ShapeDtypeStruct((tokens, WIDTH), out_dtype),
                   jax.ShapeDtypeStruct((n_seq, rows, WIDTH), F32)],
        compiler_params=_params("arbitrary", "arbitrary"),
        name="chunk_gating",
    )(z, ln_g.reshape(1, WIDTH), ws, bs_b)


def _fox_prompt_kernel(q_ref, k_ref, v_ref, fq_ref, fk_ref, o_ref, *, blk):
    qi = pl.program_id(2)
    q = (q_ref[...] * HEAD_DIM ** -0.5).astype(BF16)
    fq = fq_ref[...]
    q_pos = qi * blk + lax.broadcasted_iota(jnp.int32, (blk, blk), 0)
    k_off = lax.broadcasted_iota(jnp.int32, (blk, blk), 1)

    def body(kb, carry):
        m, l, acc = carry
        ks = pl.multiple_of(kb * blk, blk)
        k = k_ref[pl.ds(ks, blk), :].astype(BF16)
        v = v_ref[pl.ds(ks, blk), :].astype(BF16)
        s = lax.dot_general(q, k, (((1,), (1,)), ((), ())), preferred_element_type=F32)
        s = s + fq - fk_ref[kb]
        s = jnp.where(ks + k_off <= q_pos, s, NEG_INF)
        m_new = jnp.maximum(m, jnp.max(s, axis=-1, keepdims=True))
        alpha = jnp.exp(m - m_new)
        p = jnp.exp(s - m_new)
        l = alpha * l + jnp.sum(p, axis=-1, keepdims=True)
        acc = alpha * acc + jnp.dot(p.astype(BF16), v, preferred_element_type=F32)
        return m_new, l, acc

    init = (jnp.full((blk, 1), NEG_INF, F32), jnp.zeros((blk, 1), F32), jnp.zeros((blk, HEAD_DIM), F32))
    _, l, acc = lax.fori_loop(0, qi + 1, body, init)
    o_ref[...] = (acc / l).astype(o_ref.dtype)


def fox_prompt(z, cum_t, n_seq, seq_len):
    blk = 512
    nq = seq_len // blk
    tokens = z.shape[0]
    fq = cum_t.reshape(N_HEADS, tokens, 1)
    fk = cum_t.reshape(N_HEADS, tokens // blk, 1, blk)
    return pl.pallas_call(
        functools.partial(_fox_prompt_kernel, blk=blk),
        grid=(n_seq, N_HEADS, nq),
        in_specs=[pl.BlockSpec((blk, HEAD_DIM), lambda b, h, i: (b * nq + i, BQ_BLK + h)),
                  pl.BlockSpec((seq_len, HEAD_DIM), lambda b, h, i: (b, BK_BLK + h)),
                  pl.BlockSpec((seq_len, HEAD_DIM), lambda b, h, i: (b, BV_BLK + h)),
                  pl.BlockSpec((None, blk, 1), lambda b, h, i: (h, b * nq + i, 0)),
                  pl.BlockSpec((None, nq, 1, blk), lambda b, h, i: (h, b, 0, 0))],
        out_specs=pl.BlockSpec((blk, HEAD_DIM), lambda b, h, i: (b * nq + i, h)),
        out_shape=jax.ShapeDtypeStruct((tokens, WIDTH), BF16),
        compiler_params=_params("parallel", "parallel", "arbitrary"),
        name="fox_prompt",
    )(z, z, z, fq, fk)


def _fox_sample_kernel(pt_ref, qbd_ref, kn_ref, vn_ref, fcol_ref, frow_ref, *rest, pages, n_new):
    page_refs = rest[:3 * pages]
    o_ref, kbuf_ref, vbuf_ref, m_ref, l_ref, acc_ref, carry_ref = rest[3 * pages:]
    del pt_ref
    j = pl.program_id(1)
    rows = N_HEADS * n_new
    qbd = qbd_ref[...]
    fcol = fcol_ref[...]

    def update(s, v):
        m_old = m_ref[...]
        m_new = jnp.maximum(m_old, jnp.max(s, axis=-1, keepdims=True))
        alpha = jnp.exp(m_old - m_new)
        p = jnp.exp(s - m_new)
        l_ref[...] = alpha * l_ref[...] + jnp.sum(p, axis=-1, keepdims=True)
        acc_ref[...] = alpha * acc_ref[...] + jnp.dot(p.astype(BF16), v, preferred_element_type=F32)
        m_ref[...] = m_new

    @pl.when(j == 0)
    def _():
        m_ref[...] = jnp.full_like(m_ref, NEG_INF)
        l_ref[...] = jnp.zeros_like(l_ref)
        acc_ref[...] = jnp.zeros_like(acc_ref)
        carry_ref[...] = jnp.zeros_like(carry_ref)
        kn = kn_ref[...].astype(BF16)
        s = lax.dot_general(qbd, kn, (((1,), (1,)), ((), ())), preferred_element_type=F32)
        s = s + fcol - frow_ref[...]
        t_idx = lax.broadcasted_iota(jnp.int32, (rows, n_new), 0) % n_new
        s_idx = lax.broadcasted_iota(jnp.int32, (rows, n_new), 1)
        s = jnp.where(s_idx <= t_idx, s, NEG_INF)
        update(s, vn_ref[...].astype(BF16))

    lane = lax.broadcasted_iota(jnp.int32, (N_HEADS, PAGE_SIZE), 1)
    later = carry_ref[...]
    bias_parts = []
    for p in range(pages):
        k_ref, v_ref, lp_ref = page_refs[3 * p:3 * p + 3]
        key_rows = slice(p * PAGE_SIZE, (p + 1) * PAGE_SIZE)
        for h in range(N_HEADS):
            head_rows = pl.ds(h, PAGE_SIZE, stride=N_HEADS)
            cols = slice(h * HEAD_DIM, (h + 1) * HEAD_DIM)
            kbuf_ref[key_rows, cols] = k_ref[head_rows, :].astype(BF16)
            vbuf_ref[key_rows, cols] = v_ref[head_rows, :].astype(BF16)
        lp = lp_ref[...]
        suf = lp
        step = 1
        while step < PAGE_SIZE:
            shifted = pltpu.roll(suf, PAGE_SIZE - step, axis=1)
            suf = suf + jnp.where(lane + step < PAGE_SIZE, shifted, 0.0)
            step *= 2
        bias_parts.append(suf - lp + later)
        later = later + suf[:, 0:1]
    carry_ref[...] = later
    n_keys = pages * PAGE_SIZE
    bias = jnp.concatenate(bias_parts, axis=1)
    bias = jnp.broadcast_to(bias[:, None, :], (N_HEADS, n_new, n_keys)).reshape(rows, n_keys)
    s = lax.dot_general(qbd, kbuf_ref[...], (((1,), (1,)), ((), ())), preferred_element_type=F32)
    update(s + fcol + bias, vbuf_ref[...])

    @pl.when(j == pl.num_programs(1) - 1)
    def _():
        inv = 1.0 / l_ref[...]
        for h in range(N_HEADS):
            r = slice(h * n_new, (h + 1) * n_new)
            c = slice(h * HEAD_DIM, (h + 1) * HEAD_DIM)
            o_ref[:, c] = acc_ref[r, c] * inv[r]


def fox_sample(page_table, qbd, k_new, v_new, fcol, frow, cache_k, cache_v, cache_lp_t, layer):
    n_seq, n_pages = page_table.shape
    n_new = k_new.shape[1]
    rows = N_HEADS * n_new
    pages = 8
    steps = n_pages // pages

    def page_map(p):
        def index(b, j, pt):
            return (layer, pt[b, n_pages - 1 - (j * pages + p)], 0, 0)
        return index

    page_specs = []
    page_args = []
    for p in range(pages):
        page_specs += [pl.BlockSpec((None, None, PAGE_SIZE * N_HEADS, HEAD_DIM), page_map(p)),
                       pl.BlockSpec((None, None, PAGE_SIZE * N_HEADS, HEAD_DIM), page_map(p)),
                       pl.BlockSpec((None, None, N_HEADS, PAGE_SIZE), page_map(p))]
        page_args += [cache_k, cache_v, cache_lp_t]

    def per_seq(shape):
        return pl.BlockSpec((None,) + shape, lambda b, j, pt: (b, 0, 0))

    grid_spec = pltpu.PrefetchScalarGridSpec(
        num_scalar_prefetch=1,
        grid=(n_seq, steps),
        in_specs=[per_seq((rows, WIDTH)), per_seq((n_new, WIDTH)), per_seq((n_new, WIDTH)),
                  per_seq((rows, 1)), per_seq((rows, n_new))] + page_specs,
        out_specs=per_seq((n_new, WIDTH)),
        scratch_shapes=[pltpu.VMEM((pages * PAGE_SIZE, WIDTH), BF16), pltpu.VMEM((pages * PAGE_SIZE, WIDTH), BF16),
                        pltpu.VMEM((rows, 1), F32), pltpu.VMEM((rows, 1), F32),
                        pltpu.VMEM((rows, WIDTH), F32), pltpu.VMEM((N_HEADS, 1), F32)],
    )
    return pl.pallas_call(
        functools.partial(_fox_sample_kernel, pages=pages, n_new=n_new),
        grid_spec=grid_spec,
        out_shape=jax.ShapeDtypeStruct((n_seq, n_new, WIDTH), F32),
        compiler_params=_params("arbitrary", "arbitrary"),
        name="fox_sample",
    )(page_table, qbd, k_new, v_new, fcol, frow, *page_args)


def _retention_kernel(q_ref, k_ref, v_ref, g_ref, cos_ref, sin_ref, dmat_ref, qdec_ref, kdec_ref,
                      cdec_ref, gn_ref, s0_ref, o_ref, s_ref, *, chunk, n_chunks, mm_dtype):
    dmat = dmat_ref[...]
    qdec = qdec_ref[...]
    kdec = kdec_ref[...]
    cdec = cdec_ref[...]
    gn = gn_ref[...]

    def rope(x, cos, sin):
        return x * cos + pltpu.roll(x, HEAD_DIM // 2, axis=1) * sin

    def body(i, state):
        r = pl.multiple_of(i * chunk, chunk)
        rows = pl.ds(r, chunk)
        cos = cos_ref[rows, :]
        sin = sin_ref[rows, :]
        q = rope(q_ref[rows, :], cos, sin)
        k = rope(k_ref[rows, :], cos, sin) * HEAD_DIM ** -0.5
        v = v_ref[rows, :].astype(mm_dtype)
        inner = lax.dot_general(q.astype(mm_dtype), k.astype(mm_dtype), (((1,), (1,)), ((), ())),
                                preferred_element_type=F32) * dmat
        o = (jnp.dot(inner.astype(mm_dtype), v, preferred_element_type=F32)
             + jnp.dot((q * qdec).astype(mm_dtype), state.astype(mm_dtype), preferred_element_type=F32))
        kd_t = (k * kdec).T.astype(mm_dtype)
        state = state * cdec + jnp.dot(kd_t, v, preferred_element_type=F32)
        oc = o - jnp.mean(o, axis=-1, keepdims=True)
        y = oc * lax.rsqrt(jnp.mean(oc * oc, axis=-1, keepdims=True) + EPS) * gn
        gate = g_ref[rows, :]
        o_ref[rows, :] = (gate * jax.nn.sigmoid(gate) * y).astype(o_ref.dtype)
        return state

    s_ref[...] = lax.fori_loop(0, n_chunks, body, s0_ref[...])


def retention(z, s0, tabs, gn_g, n_seq, seq_len, out_dtype):
    cos, sin, dmat, qdec, kdec, cdec = tabs
    chunk = dmat.shape[-1]
    tokens = z.shape[0]

    def col(blk):
        return pl.BlockSpec((seq_len, HEAD_DIM), lambda b, h: (b, blk + h))

    def per_head(shape):
        return pl.BlockSpec((None,) + shape, lambda b, h: (h, 0, 0))

    table = pl.BlockSpec((seq_len, HEAD_DIM), lambda b, h: (0, 0))
    state = pl.BlockSpec((None, None, HEAD_DIM, HEAD_DIM), lambda b, h: (b, h, 0, 0))
    kern = functools.partial(_retention_kernel, chunk=chunk, n_chunks=seq_len // chunk,
                             mm_dtype=BF16 if chunk >= 16 else F32)
    return pl.pallas_call(
        kern,
        grid=(n_seq, N_HEADS),
        in_specs=[col(CQ_BLK), col(CK_BLK), col(CV_BLK), col(CG_BLK), table, table,
                  per_head((chunk, chunk)), per_head((chunk, HEAD_DIM)), per_head((chunk, HEAD_DIM)),
                  per_head((1, HEAD_DIM)), pl.BlockSpec((1, HEAD_DIM), lambda b, h: (0, h)), state],
        out_specs=[pl.BlockSpec((seq_len, HEAD_DIM), lambda b, h: (b, h)), state],
        out_shape=[jax.ShapeDtypeStruct((tokens, WIDTH), out_dtype),
                   jax.ShapeDtypeStruct((n_seq, N_HEADS, HEAD_DIM, HEAD_DIM), F32)],
        compiler_params=_params("parallel", "parallel"),
        name="retention",
    )(z, z, z, z, cos, sin, dmat, qdec, kdec, cdec, gn_g.reshape(1, WIDTH), s0)


def retention_tables(pos, chunk):
    half = HEAD_DIM // 2
    inv = ROPE_BASE ** (-jnp.arange(half, dtype=F32) / half)
    ang = pos.astype(F32)[:, None] * inv[None, :]
    cos = jnp.concatenate([jnp.cos(ang), jnp.cos(ang)], axis=-1)
    sin = jnp.concatenate([-jnp.sin(ang), jnp.sin(ang)], axis=-1)
    lg = jnp.log1p(-jnp.exp2(-5.0 - jnp.arange(N_HEADS, dtype=F32)))
    idx = jnp.arange(chunk, dtype=F32)
    diff = idx[:, None] - idx[None, :]
    dmat = jnp.where(diff >= 0, jnp.exp(jnp.maximum(diff, 0.0)[None] * lg[:, None, None]), 0.0)
    qdec = jnp.exp((idx[None, :] + 1.0) * lg[:, None])
    kdec = jnp.exp((chunk - 1.0 - idx)[None, :] * lg[:, None])
    cdec = jnp.exp(chunk * lg)
    lanes = (N_HEADS, chunk, HEAD_DIM)
    return (cos, sin, dmat, jnp.broadcast_to(qdec[:, :, None], lanes),
            jnp.broadcast_to(kdec[:, :, None], lanes),
            jnp.broadcast_to(cdec[:, None, None], (N_HEADS, 1, HEAD_DIM)))


HALO = 8


def _conv3(stage_ref, w_ref, rows):
    return (w_ref[2:3, :] * stage_ref[HALO:HALO + rows, :]
            + w_ref[1:2, :] * stage_ref[HALO - 1:HALO - 1 + rows, :]
            + w_ref[0:1, :] * stage_ref[HALO - 2:HALO - 2 + rows, :])


def _short_conv_kernel(bg_ref, cg_ref, hd_ref, w_ref, hist_ref, o_ref, hist_out_ref, stage_ref, *, rows):
    i = pl.program_id(1)

    @pl.when(i == 0)
    def _():
        stage_ref[HALO - 2:HALO, :] = hist_ref[...]

    stage_ref[HALO:HALO + rows, :] = cg_ref[...] * hd_ref[...]
    y = _conv3(stage_ref, w_ref, rows)
    o_ref[...] = (bg_ref[...] * y).astype(o_ref.dtype)
    tail = stage_ref[rows:rows + HALO, :]
    stage_ref[0:HALO, :] = tail
    hist_out_ref[...] = tail[HALO - 2:, :]


def short_conv(z, w_conv, hist, n_seq, seq_len, rows, out_dtype):
    tokens = z.shape[0]
    per = seq_len // rows

    def col(blk):
        return pl.BlockSpec((rows, WIDTH), lambda b, i: (b * per + i, blk))

    hist_spec = pl.BlockSpec((None, CONV_WIDTH - 1, WIDTH), lambda b, i: (b, 0, 0))
    return pl.pallas_call(
        functools.partial(_short_conv_kernel, rows=rows),
        grid=(n_seq, per),
        in_specs=[col(9), col(10), col(11), pl.BlockSpec((CONV_WIDTH, WIDTH), lambda b, i: (0, 0)), hist_spec],
        out_specs=[pl.BlockSpec((rows, WIDTH), lambda b, i: (b * per + i, 0)), hist_spec],
        out_shape=[jax.ShapeDtypeStruct((tokens, WIDTH), out_dtype),
                   jax.ShapeDtypeStruct((n_seq, CONV_WIDTH - 1, WIDTH), F32)],
        scratch_shapes=[pltpu.VMEM((HALO + rows, WIDTH), F32)],
        compiler_params=_params("arbitrary", "arbitrary"),
        name="short_conv",
    )(z, z, z, w_conv, hist)


def _merge_kernel(h_ref, wm_ref, bm_ref, o_ref, wb_ref, out_ref, acc_ref):
    n = pl.program_id(2)
    gate = jnp.dot(h_ref[...], wm_ref[...], preferred_element_type=F32) + bm_ref[...]
    proj = jnp.dot(o_ref[...].astype(BF16), wb_ref[...], preferred_element_type=F32)
    term = jax.nn.sigmoid(gate) * proj

    @pl.when(n == 0)
    def _():
        acc_ref[...] = term

    @pl.when(n > 0)
    def _():
        acc_ref[...] += term

    @pl.when(n == pl.num_programs(2) - 1)
    def _():
        out_ref[...] = acc_ref[...].astype(out_ref.dtype)


def gated_merge(h, branches, w_merge, b_merge, w_branch, layer, bm, bn):
    nb, tokens, _ = branches.shape
    depth = w_merge.shape[0]
    return pl.pallas_call(
        _merge_kernel,
        grid=(tokens // bm, D_MODEL // bn, nb),
        in_specs=[pl.BlockSpec((bm, D_MODEL), lambda i, j, n: (i, 0)),
                  pl.BlockSpec((None, None, D_MODEL, bn), lambda i, j, n: (layer, n, 0, j)),
                  pl.BlockSpec((None, None, 1, bn), lambda i, j, n: (layer, n, 0, j)),
                  pl.BlockSpec((None, bm, WIDTH), lambda i, j, n: (n, i, 0)),
                  pl.BlockSpec((None, None, WIDTH, bn), lambda i, j, n: (layer, n, 0, j))],
        out_specs=pl.BlockSpec((bm, bn), lambda i, j, n: (i, j)),
        out_shape=jax.ShapeDtypeStruct((tokens, D_MODEL), BF16),
        scratch_shapes=[pltpu.VMEM((bm, bn), F32)],
        compiler_params=_params("parallel", "arbitrary", "arbitrary"),
        name="gated_merge",
    )(h, w_merge, b_merge.reshape(depth, nb, 1, D_MODEL), branches, w_branch)


FF_BLK = 256
FF_HALF_BLKS = D_FF // FF_BLK


def _ffn_up_kernel(h_ref, wa0_ref, wa1_ref, wb0_ref, wb1_ref, cwa_ref, cwb_ref, ba_ref, bb_ref,
                   ha_ref, hb_ref, act_ref, ta_ref, tb_ref, sa_ref, sb_ref, ca_ref, cb_ref,
                   *, rows, per_seq):
    i = pl.program_id(0)
    j = pl.program_id(1)
    first = (i % per_seq) == 0

    @pl.when(first)
    def _():
        sa_ref[HALO - 2:HALO, :] = ha_ref[...]
        sb_ref[HALO - 2:HALO, :] = hb_ref[...]

    @pl.when(jnp.logical_not(first))
    def _():
        sa_ref[0:HALO, :] = ca_ref[j]
        sb_ref[0:HALO, :] = cb_ref[j]

    h = h_ref[...]
    cur = slice(HALO, HALO + rows)
    sa_ref[cur, :FF_BLK] = jnp.dot(h, wa0_ref[...], preferred_element_type=F32)
    sb_ref[cur, :FF_BLK] = jnp.dot(h, wb0_ref[...], preferred_element_type=F32)
    sa_ref[cur, FF_BLK:] = jnp.dot(h, wa1_ref[...], preferred_element_type=F32)
    sb_ref[cur, FF_BLK:] = jnp.dot(h, wb1_ref[...], preferred_element_type=F32)
    a = _conv3(sa_ref, cwa_ref, rows) + ba_ref[...]
    b = _conv3(sb_ref, cwb_ref, rows) + bb_ref[...]
    act_ref[...] = (a * jax.nn.sigmoid(a) * b).astype(act_ref.dtype)
    tail_a = sa_ref[rows:rows + HALO, :]
    tail_b = sb_ref[rows:rows + HALO, :]
    ca_ref[j] = tail_a
    cb_ref[j] = tail_b
    ta_ref[...] = tail_a
    tb_ref[...] = tail_b


def ffn_up(h, w_up, layer, cwa, cwb, ba, bb, hist_a, hist_b, n_seq, seq_len, bm):
    tokens = h.shape[0]
    per_seq = seq_len // bm
    bn = 2 * FF_BLK
    nj = D_FF_PAD // bn
    last_blk = 2 * FF_HALF_BLKS - 1

    def wspec(first_blk, sub):
        return pl.BlockSpec((None, D_MODEL, FF_BLK),
                            lambda i, j: (layer, 0, jnp.minimum(first_blk + 2 * j + sub, last_blk)))

    cspec = pl.BlockSpec((CONV_WIDTH, bn), lambda i, j: (0, j))
    bspec = pl.BlockSpec((1, bn), lambda i, j: (0, j))
    hspec = pl.BlockSpec((None, CONV_WIDTH - 1, bn), lambda i, j: (i // per_seq, 0, j))
    tspec = pl.BlockSpec((None, HALO, bn), lambda i, j: (i, 0, j))
    tail_shape = jax.ShapeDtypeStruct((tokens // bm, HALO, D_FF_PAD), F32)
    act, tail_a, tail_b = pl.pallas_call(
        functools.partial(_ffn_up_kernel, rows=bm, per_seq=per_seq),
        grid=(tokens // bm, nj),
        in_specs=[pl.BlockSpec((bm, D_MODEL), lambda i, j: (i, 0)),
                  wspec(0, 0), wspec(0, 1), wspec(FF_HALF_BLKS, 0), wspec(FF_HALF_BLKS, 1),
                  cspec, cspec, bspec, bspec, hspec, hspec],
        out_specs=[pl.BlockSpec((bm, bn), lambda i, j: (i, j)), tspec, tspec],
        out_shape=[jax.ShapeDtypeStruct((tokens, D_FF_PAD), BF16), tail_shape, tail_shape],
        scratch_shapes=[pltpu.VMEM((HALO + bm, bn), F32), pltpu.VMEM((HALO + bm, bn), F32),
                        pltpu.VMEM((nj, HALO, bn), F32), pltpu.VMEM((nj, HALO, bn), F32)],
        compiler_params=_params("arbitrary", "arbitrary"),
        name="ffn_up",
    )(h, w_up, w_up, w_up, w_up, cwa, cwb, ba, bb, hist_a, hist_b)
    return act, tail_a[per_seq - 1::per_seq], tail_b[per_seq - 1::per_seq]


def _ffn_act_kernel(ua_ref, ub_ref, cwa_ref, cwb_ref, ba_ref, bb_ref, ha_ref, hb_ref, act_ref,
                    sa_ref, sb_ref, *, rows):
    sa_ref[HALO - 2:HALO, :] = ha_ref[...]
    sb_ref[HALO - 2:HALO, :] = hb_ref[...]
    sa_ref[HALO:HALO + rows, :] = ua_ref[...]
    sb_ref[HALO:HALO + rows, :] = ub_ref[...]
    a = _conv3(sa_ref, cwa_ref, rows) + ba_ref[...]
    b = _conv3(sb_ref, cwb_ref, rows) + bb_ref[...]
    act_ref[...] = (a * jax.nn.sigmoid(a) * b).astype(act_ref.dtype)


def ffn_act(up, w_conv, bias, hist, n_seq, seq_len):
    tokens = up.shape[0]
    bn = D_FF // 2
    cspec = [pl.BlockSpec((CONV_WIDTH, bn), lambda b, j, o=o: (0, o + j)) for o in (0, 2)]
    bspec = [pl.BlockSpec((1, bn), lambda b, j, o=o: (0, o + j)) for o in (0, 2)]
    hspec = [pl.BlockSpec((None, CONV_WIDTH - 1, bn), lambda b, j, o=o: (b, 0, o + j)) for o in (0, 2)]
    uspec = [pl.BlockSpec((seq_len, bn), lambda b, j, o=o: (b, o + j)) for o in (0, 2)]
    return pl.pallas_call(
        functools.partial(_ffn_act_kernel, rows=seq_len),
        grid=(n_seq, 2),
        in_specs=uspec + cspec + bspec + hspec,
        out_specs=pl.BlockSpec((seq_len, bn), lambda b, j: (b, j)),
        out_shape=jax.ShapeDtypeStruct((tokens, D_FF), F32),
        scratch_shapes=[pltpu.VMEM((HALO + seq_len, bn), F32), pltpu.VMEM((HALO + seq_len, bn), F32)],
        compiler_params=_params("parallel", "parallel"),
        name="ffn_act",
    )(up, up, w_conv, w_conv, bias, bias, hist, hist)


WZ_ROWS = 512


def _wz_kernel(a_ref, b_ref, o_ref):
    j = pl.program_id(2)

    @pl.when(j < FORGET_COL // WIDTH)
    def _():
        o_ref[...] = a_ref[...].astype(BF16)

    @pl.when(j >= FORGET_COL // WIDTH)
    def _():
        wide = jnp.concatenate([a_ref[...], b_ref[...]], axis=1)
        o_ref[...] = wide[:, N_HEADS:N_HEADS + WIDTH].astype(BF16)


def in_proj_weights(w_in):
    depth, d, _ = w_in.shape
    lanes = WIDTH // HEAD_DIM
    return pl.pallas_call(
        _wz_kernel,
        grid=(depth, d // WZ_ROWS, Z_COLS // WIDTH),
        in_specs=[pl.BlockSpec((None, WZ_ROWS, WIDTH), lambda l, r, j: (l, r, j)),
                  pl.BlockSpec((None, WZ_ROWS, HEAD_DIM), lambda l, r, j: (l, r, lanes * (j + 1)))],
        out_specs=pl.BlockSpec((None, WZ_ROWS, WIDTH), lambda l, r, j: (l, r, j)),
        out_shape=jax.ShapeDtypeStruct((depth, d, Z_COLS), BF16),
        compiler_params=_params("parallel", "parallel", "parallel"),
        name="in_proj_weights",
    )(w_in, w_in)


def _pad_cols(a, n):
    return jnp.pad(a, ((0, 0), (0, n - a.shape[1])))


def _split_hist(hist):
    pad = ((0, 0), (0, 0), (0, D_FF_PAD - D_FF))
    return jnp.pad(hist[..., :D_FF], pad), jnp.pad(hist[..., D_FF:], pad)


def _join_hist(tail_a, tail_b):
    return jnp.concatenate([tail_a[..., :D_FF], tail_b[..., :D_FF]], axis=-1)


def mixer_and_ffn(grp, layer, x, h, mod_op, lw, small, tabs, attend, ret_s0, conv_hist, ffn_hist, next_pre,
                  next_mod_op):
    n_seq, seq_len, tokens = grp.n_seq, grp.seq_len, grp.tokens
    bmm = grp.mm_block
    act_dtype = BF16 if seq_len >= 16 else F32

    z = matmul(h, lw["wz"], layer, F32, bmm, 1024)
    logf_t, cum_t = forget_logits(grp, h, small["wf_t"], small["b_forget"])

    chunk_rows = min(seq_len, CHUNK)
    n_chunks = seq_len // chunk_rows
    o_a, v_tail = chunk_gating(z, small["a_ln_g"], small["a_ws"][:, :chunk_rows, :chunk_rows],
                               jnp.broadcast_to(small["a_bs"][:, :chunk_rows, None],
                                                (N_HEADS, chunk_rows, HEAD_DIM)),
                               n_seq, n_chunks, chunk_rows, act_dtype)
    o_b = attend(z, cum_t)
    o_c, ret_state = retention(z, ret_s0, tabs, small["ret_gn_g"], n_seq, seq_len, act_dtype)
    conv_rows = min(seq_len, 512)
    o_d, conv_state = short_conv(z, small["w_sc_conv"], conv_hist, n_seq, seq_len, conv_rows, act_dtype)

    branches = jnp.stack([o_a, o_b.astype(act_dtype), o_c, o_d], axis=0)
    merged = gated_merge(h, branches, lw["w_merge"], lw["b_merge"], lw["w_branch"], layer, bmm, 512)
    y = matmul(merged, lw["w_out"], layer, F32, bmm, 1024)
    x, h2 = residual(grp, x, y, small["g_post_mix"], mod_op, 2, small["g_pre_ffn"], mod_op, 3, 4)

    cw = small["w_ffn_conv"]
    bias = small["b_ffn_conv"].reshape(1, 2 * D_FF)
    if seq_len >= bmm:
        hist_a, hist_b = _split_hist(ffn_hist)
        act, tail_a, tail_b = ffn_up(h2, lw["w_up"], layer,
                                     _pad_cols(cw[:, :D_FF], D_FF_PAD), _pad_cols(cw[:, D_FF:], D_FF_PAD),
                                     _pad_cols(bias[:, :D_FF], D_FF_PAD), _pad_cols(bias[:, D_FF:], D_FF_PAD),
                                     hist_a, hist_b, n_seq, seq_len, bmm)
        ffn_state = _join_hist(tail_a[:, HALO - 2:], tail_b[:, HALO - 2:])
    else:
        up = matmul(h2, lw["w_up"], layer, F32, bmm, 512)
        act = _pad_cols(ffn_act(up, cw, bias, ffn_hist, n_seq, seq_len), D_FF_PAD)
        ffn_state = up.reshape(n_seq, seq_len, 2 * D_FF)[:, seq_len - 2:]
    y2 = matmul_ksplit(act, lw["w_down"], layer, bmm, 1024, 2816)
    if next_pre is None:
        x = residual(grp, x, y2, small["g_post_ffn"], mod_op, 5)
        h_next = None
    else:
        x, h_next = residual(grp, x, y2, small["g_post_ffn"], mod_op, 5, next_pre, next_mod_op, 0, 1)

    k_b = z[:, BK_BLK * HEAD_DIM:BV_BLK * HEAD_DIM].reshape(n_seq, seq_len, N_HEADS, HEAD_DIM)
    v_b = z[:, BV_BLK * HEAD_DIM:CQ_BLK * HEAD_DIM].reshape(n_seq, seq_len, N_HEADS, HEAD_DIM)
    logf = logf_t.T.reshape(n_seq, seq_len, N_HEADS)
    return x, h_next, (k_b, v_b, logf, ret_state, conv_state, v_tail, ffn_state)


def kernel(x_prompt, x_sample, cache_k, cache_v, cache_logf, state_ret, state_conv, state_ffn_conv,
           page_table, c_prompt, c_sample, w_ada, b_ada, g_pre_mix, g_post_mix, g_pre_ffn, g_post_ffn,
           w_in, b_forget, a_ln_g, a_ws, a_bs, ret_gn_g, w_sc_conv, w_branch, w_merge, b_merge, w_out,
           w_up, w_ffn_conv, b_ffn_conv, w_down):
    depth = w_in.shape[0]
    bp, lp, _ = x_prompt.shape
    bs, ls, _ = x_sample.shape
    grp_p = Group(bp, lp, row_block=256, mm_block=1024)
    grp_s = Group(bs, ls, row_block=bs * ls, mm_block=bs * ls)

    c_all = jnp.pad(jnp.concatenate([c_prompt, c_sample], axis=0), ((0, 16 - bp - bs), (0, 0)))
    mod = ada_modulation(c_all, w_ada, b_ada)

    tabs_p = retention_tables(jnp.arange(lp), math.gcd(lp, CHUNK))
    tabs_s = retention_tables(PAST_LEN + jnp.arange(ls), math.gcd(ls, CHUNK))

    n_pool = cache_k.shape[1]
    cache_k2 = cache_k.reshape(depth, n_pool, PAGE_SIZE * N_HEADS, HEAD_DIM)
    cache_v2 = cache_v.reshape(depth, n_pool, PAGE_SIZE * N_HEADS, HEAD_DIM)
    cache_lp_t = cache_logf.transpose(0, 1, 3, 2)

    xp = x_prompt.reshape(grp_p.tokens, D_MODEL)
    xs = x_sample.reshape(grp_s.tokens, D_MODEL)
    mods_p = [grp_p.mod_operand(mod[l, :bp]) for l in range(depth)]
    mods_s = [grp_s.mod_operand(mod[l, bp:bp + bs]) for l in range(depth)]
    hp = prenorm(grp_p, xp, g_pre_mix[0], mods_p[0], 0, 1)
    hs = prenorm(grp_s, xs, g_pre_mix[0], mods_s[0], 0, 1)

    zeros_ret = jnp.zeros((bp, N_HEADS, HEAD_DIM, HEAD_DIM), F32)
    zeros_conv = jnp.zeros((bp, CONV_WIDTH - 1, WIDTH), F32)
    zeros_ffn = jnp.zeros((bp, CONV_WIDTH - 1, 2 * D_FF), F32)
    eye = jnp.eye(N_HEADS, dtype=F32)

    lw = dict(wz=in_proj_weights(w_in), w_merge=w_merge.astype(BF16), b_merge=b_merge,
              w_branch=w_branch.astype(BF16), w_out=w_out.astype(BF16), w_up=w_up.astype(BF16),
              w_down=jnp.pad(w_down.astype(BF16), ((0, 0), (0, D_FF_PAD - D_FF), (0, 0))))

    st_p, st_s = [], []
    for l in range(depth):
        wf = w_in[l, :, FORGET_COL:FORGET_COL + N_HEADS]
        small = dict(wf_t=jnp.pad(wf.T, ((0, 16 - N_HEADS), (0, 0))).astype(BF16),
                     b_forget=b_forget[l], a_ln_g=a_ln_g[l], a_ws=a_ws[l], a_bs=a_bs[l],
                     ret_gn_g=ret_gn_g[l], w_sc_conv=w_sc_conv[l],
                     w_ffn_conv=w_ffn_conv[l], b_ffn_conv=b_ffn_conv[l],
                     g_post_mix=g_post_mix[l], g_pre_ffn=g_pre_ffn[l], g_post_ffn=g_post_ffn[l])
        next_pre = g_pre_mix[l + 1] if l + 1 < depth else None

        def attend_p(z, cum_t):
            return fox_prompt(z, cum_t, bp, lp)

        def attend_s(z, cum_t, l=l):
            zs = z.reshape(bs, ls, Z_COLS)
            q = zs[..., BQ_BLK * HEAD_DIM:BK_BLK * HEAD_DIM].reshape(bs, ls, N_HEADS, HEAD_DIM)
            q = q.transpose(0, 2, 1, 3) * HEAD_DIM ** -0.5
            qbd = (q[:, :, :, None, :] * eye[None, :, None, :, None]).reshape(bs, N_HEADS * ls, WIDTH)
            k_new = zs[..., BK_BLK * HEAD_DIM:BV_BLK * HEAD_DIM]
            v_new = zs[..., BV_BLK * HEAD_DIM:CQ_BLK * HEAD_DIM]
            cum = cum_t.reshape(N_HEADS, bs, ls).transpose(1, 0, 2)
            fcol = cum.reshape(bs, N_HEADS * ls, 1)
            frow = jnp.broadcast_to(cum[:, :, None, :], (bs, N_HEADS, ls, ls)).reshape(bs, N_HEADS * ls, ls)
            o = fox_sample(page_table, qbd.astype(BF16), k_new, v_new, fcol, frow,
                           cache_k2, cache_v2, cache_lp_t, l)
            return o.reshape(bs * ls, WIDTH)

        xp, hp, sp = mixer_and_ffn(grp_p, l, xp, hp, mods_p[l], lw, small, tabs_p, attend_p,
                                   zeros_ret, zeros_conv, zeros_ffn, next_pre,
                                   mods_p[l + 1] if l + 1 < depth else None)
        xs, hs, ss = mixer_and_ffn(grp_s, l, xs, hs, mods_s[l], lw, small, tabs_s, attend_s,
                                   state_ret[l], state_conv[l], state_ffn_conv[l], next_pre,
                                   mods_s[l + 1] if l + 1 < depth else None)
        st_p.append(sp)
        st_s.append(ss)

    def stack(states, idx):
        return jnp.stack([s[idx] for s in states], axis=0)

    return (xp.reshape(bp, lp, D_MODEL), xs.reshape(bs, ls, D_MODEL),
            stack(st_p, 0), stack(st_p, 1), stack(st_p, 2),
            stack(st_s, 0), stack(st_s, 1), stack(st_s, 2),
            stack(st_p, 3), stack(st_s, 3), stack(st_p, 4), stack(st_s, 4),
            stack(st_p, 5), stack(st_s, 5), stack(st_p, 6), stack(st_s, 6))
```

```python
import functools
import math

import jax
import jax.numpy as jnp
from jax import lax
from jax.experimental import pallas as pl
from jax.experimental.pallas import tpu as pltpu

D_MODEL = 4096
HEAD_DIM = 128
N_HEADS = 8
WIDTH = N_HEADS * HEAD_DIM
CHUNK = 128
PAST_LEN = 16384
PAGE_SIZE = 128
ROPE_BASE = 10000.0
CONV_WIDTH = 3
D_FF = 11008
D_FF_PAD = 11264
N_MOD = 6
EPS = 1e-6
NEG_INF = -1e30
Z_COLS = 12 * WIDTH
FORGET_COL = 5 * WIDTH

BQ_BLK, BK_BLK, BV_BLK = 16, 24, 32
CQ_BLK, CK_BLK, CV_BLK, CG_BLK = 40, 48, 56, 64

VMEM_LIMIT_BYTES = 56 * 1024 * 1024

BF16 = jnp.bfloat16
F32 = jnp.float32


def _params(*semantics):
    return pltpu.CompilerParams(dimension_semantics=semantics, vmem_limit_bytes=VMEM_LIMIT_BYTES)


def _ada_kernel(c_ref, w_ref, b_ref, o_ref):
    c = c_ref[...]
    s = (c * jax.nn.sigmoid(c)).astype(BF16)
    o_ref[...] = jnp.dot(s, w_ref[...].astype(BF16), preferred_element_type=F32) + b_ref[...]


def ada_modulation(c_all, w_ada, b_ada):
    depth, d, n = w_ada.shape
    rows = c_all.shape[0]
    bn = 512
    return pl.pallas_call(
        _ada_kernel,
        grid=(depth, n // bn),
        in_specs=[
            pl.BlockSpec((rows, d), lambda l, j: (0, 0)),
            pl.BlockSpec((None, d, bn), lambda l, j: (l, 0, j)),
            pl.BlockSpec((None, 1, bn), lambda l, j: (l, 0, j)),
        ],
        out_specs=pl.BlockSpec((None, rows, bn), lambda l, j: (l, 0, j)),
        out_shape=jax.ShapeDtypeStruct((depth, rows, n), F32),
        compiler_params=_params("arbitrary", "arbitrary"),
        name="ada_modulation",
    )(c_all, w_ada, b_ada.reshape(depth, 1, n))


class Group:
    def __init__(self, n_seq, seq_len, row_block, mm_block):
        self.n_seq = n_seq
        self.seq_len = seq_len
        self.tokens = n_seq * seq_len
        self.row_block = row_block
        self.mm_block = mm_block
        self.per_seq_rows = row_block <= seq_len

    def mod_operand(self, mod):
        if self.per_seq_rows:
            return mod.reshape(self.n_seq, N_MOD, 1, D_MODEL)
        tok = jnp.repeat(mod.reshape(self.n_seq, N_MOD, D_MODEL), self.seq_len, axis=0)
        return tok.transpose(1, 0, 2)

    def mod_spec(self, idx):
        if self.per_seq_rows:
            per = self.seq_len // self.row_block
            return pl.BlockSpec((None, None, 1, D_MODEL), lambda i: (i // per, idx, 0, 0))
        return pl.BlockSpec((None, self.row_block, D_MODEL), lambda i: (idx, i, 0))


def _rms(x, g):
    return x * lax.rsqrt(jnp.mean(x * x, axis=-1, keepdims=True) + EPS) * g


def _prenorm_kernel(x_ref, g_ref, sc_ref, sh_ref, h_ref):
    h_ref[...] = (_rms(x_ref[...], g_ref[...]) * (1.0 + sc_ref[...]) + sh_ref[...]).astype(h_ref.dtype)


def prenorm(grp, x, g, mod_op, sh_idx, sc_idx):
    br = grp.row_block
    row = pl.BlockSpec((br, D_MODEL), lambda i: (i, 0))
    vec = pl.BlockSpec((1, D_MODEL), lambda i: (0, 0))
    return pl.pallas_call(
        _prenorm_kernel,
        grid=(grp.tokens // br,),
        in_specs=[row, vec, grp.mod_spec(sc_idx), grp.mod_spec(sh_idx)],
        out_specs=row,
        out_shape=jax.ShapeDtypeStruct((grp.tokens, D_MODEL), BF16),
        compiler_params=_params("parallel"),
        name="prenorm",
    )(x, g.reshape(1, D_MODEL), mod_op, mod_op)


def _residual_kernel(x_ref, y_ref, gpost_ref, gate_ref, xo_ref):
    xo_ref[...] = x_ref[...] + gate_ref[...] * _rms(y_ref[...], gpost_ref[...])


def _residual_prenorm_kernel(x_ref, y_ref, gpost_ref, gate_ref, gpre_ref, sc_ref, sh_ref, xo_ref, h_ref):
    xn = x_ref[...] + gate_ref[...] * _rms(y_ref[...], gpost_ref[...])
    xo_ref[...] = xn
    h_ref[...] = (_rms(xn, gpre_ref[...]) * (1.0 + sc_ref[...]) + sh_ref[...]).astype(h_ref.dtype)


def residual(grp, x, y, g_post, mod_op, gate_idx, g_pre=None, pre_mod_op=None, sh_idx=None, sc_idx=None):
    br = grp.row_block
    row = pl.BlockSpec((br, D_MODEL), lambda i: (i, 0))
    vec = pl.BlockSpec((1, D_MODEL), lambda i: (0, 0))
    x_shape = jax.ShapeDtypeStruct((grp.tokens, D_MODEL), F32)
    if g_pre is None:
        return pl.pallas_call(
            _residual_kernel,
            grid=(grp.tokens // br,),
            in_specs=[row, row, vec, grp.mod_spec(gate_idx)],
            out_specs=row,
            out_shape=x_shape,
            compiler_params=_params("parallel"),
            name="residual",
        )(x, y, g_post.reshape(1, D_MODEL), mod_op)
    return pl.pallas_call(
        _residual_prenorm_kernel,
        grid=(grp.tokens // br,),
        in_specs=[row, row, vec, grp.mod_spec(gate_idx), vec, grp.mod_spec(sc_idx), grp.mod_spec(sh_idx)],
        out_specs=[row, row],
        out_shape=[x_shape, jax.ShapeDtypeStruct((grp.tokens, D_MODEL), BF16)],
        compiler_params=_params("parallel"),
        name="residual_prenorm",
    )(x, y, g_post.reshape(1, D_MODEL), mod_op, g_pre.reshape(1, D_MODEL), pre_mod_op, pre_mod_op)


def _mm_kernel(x_ref, w_ref, o_ref):
    o_ref[...] = jnp.dot(x_ref[...].astype(BF16), w_ref[...],
                         preferred_element_type=F32).astype(o_ref.dtype)


def matmul(x, w, layer, out_dtype, bm, bn):
    m, k = x.shape
    n = w.shape[2]
    return pl.pallas_call(
        _mm_kernel,
        grid=(m // bm, n // bn),
        in_specs=[pl.BlockSpec((bm, k), lambda i, j: (i, 0)),
                  pl.BlockSpec((None, k, bn), lambda i, j: (layer, 0, j))],
        out_specs=pl.BlockSpec((bm, bn), lambda i, j: (i, j)),
        out_shape=jax.ShapeDtypeStruct((m, n), out_dtype),
        compiler_params=_params("parallel", "arbitrary"),
        name="matmul",
    )(x, w)


def _mm_ksplit_kernel(x_ref, w_ref, o_ref):
    @pl.when(pl.program_id(2) == 0)
    def _():
        o_ref[...] = jnp.zeros_like(o_ref)

    o_ref[...] += jnp.dot(x_ref[...].astype(BF16), w_ref[...], preferred_element_type=F32)


def matmul_ksplit(x, w, layer, bm, bn, bk):
    m, k = x.shape
    n = w.shape[2]
    return pl.pallas_call(
        _mm_ksplit_kernel,
        grid=(m // bm, n // bn, k // bk),
        in_specs=[pl.BlockSpec((bm, bk), lambda i, j, kk: (i, kk)),
                  pl.BlockSpec((None, bk, bn), lambda i, j, kk: (layer, kk, j))],
        out_specs=pl.BlockSpec((bm, bn), lambda i, j, kk: (i, j)),
        out_shape=jax.ShapeDtypeStruct((m, n), F32),
        compiler_params=_params("parallel", "parallel", "arbitrary"),
        name="matmul_ksplit",
    )(x, w)


def _mm_cast_kernel(x_ref, w_ref, o_ref, w16_ref):
    w = w_ref[...].astype(BF16)
    w16_ref[...] = w
    o_ref[...] = jnp.dot(x_ref[...].astype(BF16), w, preferred_element_type=F32)


def matmul_cast(x, w, layer, bn):
    m, k = x.shape
    n = w.shape[2]
    return pl.pallas_call(
        _mm_cast_kernel,
        grid=(n // bn,),
        in_specs=[pl.BlockSpec((m, k), lambda j: (0, 0)),
                  pl.BlockSpec((None, k, bn), lambda j: (layer, 0, j))],
        out_specs=[pl.BlockSpec((m, bn), lambda j: (0, j)),
                   pl.BlockSpec((None, k, bn), lambda j: (0, 0, j))],
        out_shape=[jax.ShapeDtypeStruct((m, n), F32), jax.ShapeDtypeStruct((1, k, n), BF16)],
        compiler_params=_params("parallel"),
        name="matmul_cast",
    )(x, w)


def _mm_ksplit_cast_kernel(x_ref, w_ref, o_ref, w16_ref, *, bk, k_valid):
    kk = pl.program_id(1)
    row = kk * bk + lax.broadcasted_iota(jnp.int32, w_ref.shape, 0)
    w = jnp.where(row < k_valid, w_ref[...], 0.0).astype(BF16)
    w16_ref[...] = w

    @pl.when(kk == 0)
    def _():
        o_ref[...] = jnp.zeros_like(o_ref)

    o_ref[...] += jnp.dot(x_ref[...].astype(BF16), w, preferred_element_type=F32)


def matmul_ksplit_cast(x, w, layer, bn, bk):
    m, k_pad = x.shape
    k_valid, n = w.shape[1:]
    return pl.pallas_call(
        functools.partial(_mm_ksplit_cast_kernel, bk=bk, k_valid=k_valid),
        grid=(n // bn, k_pad // bk),
        in_specs=[pl.BlockSpec((m, bk), lambda j, kk: (0, kk)),
                  pl.BlockSpec((None, bk, bn), lambda j, kk: (layer, kk, j))],
        out_specs=[pl.BlockSpec((m, bn), lambda j, kk: (0, j)),
                   pl.BlockSpec((None, bk, bn), lambda j, kk: (0, kk, j))],
        out_shape=[jax.ShapeDtypeStruct((m, n), F32), jax.ShapeDtypeStruct((1, k_pad, n), BF16)],
        compiler_params=_params("parallel", "arbitrary"),
        name="matmul_ksplit_cast",
    )(x, w)


def _logf_kernel(h_ref, wf_ref, bf_ref, logf_ref, cum_ref, carry_ref, *, tile, seq_len, cw):
    i = pl.program_id(0)
    logits = lax.dot_general(wf_ref[...].astype(BF16), h_ref[...], (((1,), (1,)), ((), ())),
                             preferred_element_type=F32)[:N_HEADS]
    x = logits + bf_ref[...]
    logf = jnp.minimum(x, 0.0) - jnp.log1p(jnp.exp(-jnp.abs(x)))
    logf_ref[...] = logf

    seg = min(seq_len, cw)
    src = lax.broadcasted_iota(jnp.int32, (cw, cw), 0)
    dst = lax.broadcasted_iota(jnp.int32, (cw, cw), 1)
    tri = ((src <= dst) & (src // seg == dst // seg)).astype(F32)

    if seq_len > tile:
        @pl.when((i * tile) % seq_len == 0)
        def _():
            carry_ref[...] = jnp.zeros_like(carry_ref)
        carry = carry_ref[...]
    else:
        carry = jnp.zeros((N_HEADS, 1), F32)

    for c in range(tile // cw):
        part = jnp.dot(logf[:, c * cw:(c + 1) * cw], tri, precision=lax.Precision.HIGHEST,
                       preferred_element_type=F32)
        cum = part + carry
        cum_ref[:, c * cw:(c + 1) * cw] = cum
        if seq_len > cw:
            carry = cum[:, cw - 1:cw]
    if seq_len > tile:
        carry_ref[...] = carry


def forget_logits(grp, h, wf_t, b_forget):
    tile = min(512, grp.tokens)
    cw = min(128, tile)
    kern = functools.partial(_logf_kernel, tile=tile, seq_len=grp.seq_len, cw=cw)
    out = jax.ShapeDtypeStruct((N_HEADS, grp.tokens), F32)
    return pl.pallas_call(
        kern,
        grid=(grp.tokens // tile,),
        in_specs=[pl.BlockSpec((tile, D_MODEL), lambda i: (i, 0)),
                  pl.BlockSpec((16, D_MODEL), lambda i: (0, 0)),
                  pl.BlockSpec((N_HEADS, 1), lambda i: (0, 0))],
        out_specs=[pl.BlockSpec((N_HEADS, tile), lambda i: (0, i))] * 2,
        out_shape=[out, out],
        scratch_shapes=[pltpu.VMEM((N_HEADS, 1), F32)],
        compiler_params=_params("arbitrary"),
        name="forget_logits",
    )(h, wf_t, b_forget.reshape(N_HEADS, 1))


def _gating_kernel(z_ref, lng_ref, ws_ref, bs_ref, o_ref, vt_ref, *, rows, mm_dtype):
    z = jax.nn.gelu(z_ref[...])
    u = z[:, :WIDTH]
    v = z[:, WIDTH:]
    vc = v - jnp.mean(v, axis=-1, keepdims=True)
    vn = vc * lax.rsqrt(jnp.mean(vc * vc, axis=-1, keepdims=True) + EPS) * lng_ref[...]
    vt_ref[...] = vn
    t_idx = lax.broadcasted_iota(jnp.int32, (rows, rows), 0)
    s_idx = lax.broadcasted_iota(jnp.int32, (rows, rows), 1)
    causal = s_idx <= t_idx
    for g in range(N_HEADS):
        cols = slice(g * HEAD_DIM, (g + 1) * HEAD_DIM)
        ws = jnp.where(causal, ws_ref[g], 0.0).astype(mm_dtype)
        mixed = jnp.dot(ws, vn[:, cols].astype(mm_dtype), preferred_element_type=F32) + bs_ref[g]
        o_ref[:, cols] = (u[:, cols] * mixed).astype(o_ref.dtype)


def chunk_gating(z, ln_g, ws, bs_b, n_seq, n_chunks, rows, out_dtype):
    tokens = z.shape[0]
    kern = functools.partial(_gating_kernel, rows=rows, mm_dtype=BF16 if rows >= 16 else F32)
    return pl.pallas_call(
        kern,
        grid=(n_seq, n_chunks),
        in_specs=[pl.BlockSpec((rows, 2 * WIDTH), lambda b, n: (b * n_chunks + n, 0)),
                  pl.BlockSpec((1, WIDTH), lambda b, n: (0, 0)),
                  pl.BlockSpec((N_HEADS, rows, rows), lambda b, n: (0, 0, 0)),
                  pl.BlockSpec((N_HEADS, rows, HEAD_DIM), lambda b, n: (0, 0, 0))],
        out_specs=[pl.BlockSpec((rows, WIDTH), lambda b, n: (b * n_chunks + n, 0)),
                   pl.BlockSpec((None, rows, WIDTH), lambda b, n: (b, 0, 0))],
        out_shape=[jax.ShapeDtypeStruct((tokens, WIDTH), out_dtype),
                   jax.ShapeDtypeStruct((n_seq, rows, WIDTH), F32)],
        compiler_params=_params("arbitrary", "arbitrary"),
        name="chunk_gating",
    )(z, ln_g.reshape(1, WIDTH), ws, bs_b)


def _fox_prompt_kernel(q_ref, k_ref, v_ref, fq_ref, fk_ref, o_ref, *, blk):
    qi = pl.program_id(2)
    q = (q_ref[...] * HEAD_DIM ** -0.5).astype(BF16)
    fq = fq_ref[...]
    q_pos = qi * blk + lax.broadcasted_iota(jnp.int32, (blk, blk), 0)
    k_off = lax.broadcasted_iota(jnp.int32, (blk, blk), 1)

    def body(kb, carry):
        m, l, acc = carry
        ks = pl.multiple_of(kb * blk, blk)
        k = k_ref[pl.ds(ks, blk), :].astype(BF16)
        v = v_ref[pl.ds(ks, blk), :].astype(BF16)
        s = lax.dot_general(q, k, (((1,), (1,)), ((), ())), preferred_element_type=F32)
        s = s + fq - fk_ref[kb]
        s = jnp.where(ks + k_off <= q_pos, s, NEG_INF)
        m_new = jnp.maximum(m, jnp.max(s, axis=-1, keepdims=True))
        alpha = jnp.exp(m - m_new)
        p = jnp.exp(s - m_new)
        l = alpha * l + jnp.sum(p, axis=-1, keepdims=True)
        acc = alpha * acc + jnp.dot(p.astype(BF16), v, preferred_element_type=F32)
        return m_new, l, acc

    init = (jnp.full((blk, 1), NEG_INF, F32), jnp.zeros((blk, 1), F32), jnp.zeros((blk, HEAD_DIM), F32))
    _, l, acc = lax.fori_loop(0, qi + 1, body, init)
    o_ref[...] = (acc / l).astype(o_ref.dtype)


def fox_prompt(z, cum_t, n_seq, seq_len):
    blk = 512
    nq = seq_len // blk
    tokens = z.shape[0]
    fq = cum_t.reshape(N_HEADS, tokens, 1)
    fk = cum_t.reshape(N_HEADS, tokens // blk, 1, blk)
    return pl.pallas_call(
        functools.partial(_fox_prompt_kernel, blk=blk),
        grid=(n_seq, N_HEADS, nq),
        in_specs=[pl.BlockSpec((blk, HEAD_DIM), lambda b, h, i: (b * nq + i, BQ_BLK + h)),
                  pl.BlockSpec((seq_len, HEAD_DIM), lambda b, h, i: (b, BK_BLK + h)),
                  pl.BlockSpec((seq_len, HEAD_DIM), lambda b, h, i: (b, BV_BLK + h)),
                  pl.BlockSpec((None, blk, 1), lambda b, h, i: (h, b * nq + i, 0)),
                  pl.BlockSpec((None, nq, 1, blk), lambda b, h, i: (h, b, 0, 0))],
        out_specs=pl.BlockSpec((blk, HEAD_DIM), lambda b, h, i: (b * nq + i, h)),
        out_shape=jax.ShapeDtypeStruct((tokens, WIDTH), BF16),
        compiler_params=_params("parallel", "parallel", "arbitrary"),
        name="fox_prompt",
    )(z, z, z, fq, fk)


def _fox_sample_kernel(pt_ref, qbd_ref, kn_ref, vn_ref, fcol_ref, frow_ref, *rest, pages, n_new):
    page_refs = rest[:3 * pages]
    o_ref, kbuf_ref, vbuf_ref, m_ref, l_ref, acc_ref, carry_ref = rest[3 * pages:]
    del pt_ref
    j = pl.program_id(1)
    rows = N_HEADS * n_new
    qbd = qbd_ref[...]
    fcol = fcol_ref[...]

    def update(s, v):
        m_old = m_ref[...]
        m_new = jnp.maximum(m_old, jnp.max(s, axis=-1, keepdims=True))
        alpha = jnp.exp(m_old - m_new)
        p = jnp.exp(s - m_new)
        l_ref[...] = alpha * l_ref[...] + jnp.sum(p, axis=-1, keepdims=True)
        acc_ref[...] = alpha * acc_ref[...] + jnp.dot(p.astype(BF16), v, preferred_element_type=F32)
        m_ref[...] = m_new

    @pl.when(j == 0)
    def _():
        m_ref[...] = jnp.full_like(m_ref, NEG_INF)
        l_ref[...] = jnp.zeros_like(l_ref)
        acc_ref[...] = jnp.zeros_like(acc_ref)
        carry_ref[...] = jnp.zeros_like(carry_ref)
        kn = kn_ref[...].astype(BF16)
        s = lax.dot_general(qbd, kn, (((1,), (1,)), ((), ())), preferred_element_type=F32)
        s = s + fcol - frow_ref[...]
        t_idx = lax.broadcasted_iota(jnp.int32, (rows, n_new), 0) % n_new
        s_idx = lax.broadcasted_iota(jnp.int32, (rows, n_new), 1)
        s = jnp.where(s_idx <= t_idx, s, NEG_INF)
        update(s, vn_ref[...].astype(BF16))

    lane = lax.broadcasted_iota(jnp.int32, (N_HEADS, PAGE_SIZE), 1)
    later = carry_ref[...]
    bias_parts = []
    for p in range(pages):
        k_ref, v_ref, lp_ref = page_refs[3 * p:3 * p + 3]
        key_rows = slice(p * PAGE_SIZE, (p + 1) * PAGE_SIZE)
        for h in range(N_HEADS):
            head_rows = pl.ds(h, PAGE_SIZE, stride=N_HEADS)
            cols = slice(h * HEAD_DIM, (h + 1) * HEAD_DIM)
            kbuf_ref[key_rows, cols] = k_ref[head_rows, :].astype(BF16)
            vbuf_ref[key_rows, cols] = v_ref[head_rows, :].astype(BF16)
        lp = lp_ref[...]
        suf = lp
        step = 1
        while step < PAGE_SIZE:
            shifted = pltpu.roll(suf, PAGE_SIZE - step, axis=1)
            suf = suf + jnp.where(lane + step < PAGE_SIZE, shifted, 0.0)
            step *= 2
        bias_parts.append(suf - lp + later)
        later = later + suf[:, 0:1]
    carry_ref[...] = later
    n_keys = pages * PAGE_SIZE
    bias = jnp.concatenate(bias_parts, axis=1)
    bias = jnp.broadcast_to(bias[:, None, :], (N_HEADS, n_new, n_keys)).reshape(rows, n_keys)
    s = lax.dot_general(qbd, kbuf_ref[...], (((1,), (1,)), ((), ())), preferred_element_type=F32)
    update(s + fcol + bias, vbuf_ref[...])

    @pl.when(j == pl.num_programs(1) - 1)
    def _():
        inv = 1.0 / l_ref[...]
        for h in range(N_HEADS):
            r = slice(h * n_new, (h + 1) * n_new)
            c = slice(h * HEAD_DIM, (h + 1) * HEAD_DIM)
            o_ref[:, c] = acc_ref[r, c] * inv[r]


def fox_sample(page_table, qbd, k_new, v_new, fcol, frow, cache_k, cache_v, cache_lp_t, layer):
    n_seq, n_pages = page_table.shape
    n_new = k_new.shape[1]
    rows = N_HEADS * n_new
    pages = 8
    steps = n_pages // pages

    def page_map(p):
        def index(b, j, pt):
            return (layer, pt[b, n_pages - 1 - (j * pages + p)], 0, 0)
        return index

    page_specs = []
    page_args = []
    for p in range(pages):
        page_specs += [pl.BlockSpec((None, None, PAGE_SIZE * N_HEADS, HEAD_DIM), page_map(p)),
                       pl.BlockSpec((None, None, PAGE_SIZE * N_HEADS, HEAD_DIM), page_map(p)),
                       pl.BlockSpec((None, None, N_HEADS, PAGE_SIZE), page_map(p))]
        page_args += [cache_k, cache_v, cache_lp_t]

    def per_seq(shape):
        return pl.BlockSpec((None,) + shape, lambda b, j, pt: (b, 0, 0))

    grid_spec = pltpu.PrefetchScalarGridSpec(
        num_scalar_prefetch=1,
        grid=(n_seq, steps),
        in_specs=[per_seq((rows, WIDTH)), per_seq((n_new, WIDTH)), per_seq((n_new, WIDTH)),
                  per_seq((rows, 1)), per_seq((rows, n_new))] + page_specs,
        out_specs=per_seq((n_new, WIDTH)),
        scratch_shapes=[pltpu.VMEM((pages * PAGE_SIZE, WIDTH), BF16), pltpu.VMEM((pages * PAGE_SIZE, WIDTH), BF16),
                        pltpu.VMEM((rows, 1), F32), pltpu.VMEM((rows, 1), F32),
                        pltpu.VMEM((rows, WIDTH), F32), pltpu.VMEM((N_HEADS, 1), F32)],
    )
    return pl.pallas_call(
        functools.partial(_fox_sample_kernel, pages=pages, n_new=n_new),
        grid_spec=grid_spec,
        out_shape=jax.ShapeDtypeStruct((n_seq, n_new, WIDTH), F32),
        compiler_params=_params("arbitrary", "arbitrary"),
        name="fox_sample",
    )(page_table, qbd, k_new, v_new, fcol, frow, *page_args)


def _retention_kernel(q_ref, k_ref, v_ref, g_ref, cos_ref, sin_ref, dmat_ref, qdec_ref, kdec_ref,
                      cdec_ref, gn_ref, s0_ref, o_ref, s_ref, *, chunk, n_chunks, mm_dtype):
    dmat = dmat_ref[...]
    qdec = qdec_ref[...]
    kdec = kdec_ref[...]
    cdec = cdec_ref[...]
    gn = gn_ref[...]

    def rope(x, cos, sin):
        return x * cos + pltpu.roll(x, HEAD_DIM // 2, axis=1) * sin

    def body(i, state):
        r = pl.multiple_of(i * chunk, chunk)
        rows = pl.ds(r, chunk)
        cos = cos_ref[rows, :]
        sin = sin_ref[rows, :]
        q = rope(q_ref[rows, :], cos, sin)
        k = rope(k_ref[rows, :], cos, sin) * HEAD_DIM ** -0.5
        v = v_ref[rows, :].astype(mm_dtype)
        inner = lax.dot_general(q.astype(mm_dtype), k.astype(mm_dtype), (((1,), (1,)), ((), ())),
                                preferred_element_type=F32) * dmat
        o = (jnp.dot(inner.astype(mm_dtype), v, preferred_element_type=F32)
             + jnp.dot((q * qdec).astype(mm_dtype), state.astype(mm_dtype), preferred_element_type=F32))
        kd_t = (k * kdec).T.astype(mm_dtype)
        state = state * cdec + jnp.dot(kd_t, v, preferred_element_type=F32)
        oc = o - jnp.mean(o, axis=-1, keepdims=True)
        y = oc * lax.rsqrt(jnp.mean(oc * oc, axis=-1, keepdims=True) + EPS) * gn
        gate = g_ref[rows, :]
        o_ref[rows, :] = (gate * jax.nn.sigmoid(gate) * y).astype(o_ref.dtype)
        return state

    s_ref[...] = lax.fori_loop(0, n_chunks, body, s0_ref[...], unroll=min(8, n_chunks))


def retention(z, s0, tabs, gn_g, n_seq, seq_len, out_dtype):
    cos, sin, dmat, qdec, kdec, cdec = tabs
    chunk = dmat.shape[-1]
    tokens = z.shape[0]

    def col(blk):
        return pl.BlockSpec((seq_len, HEAD_DIM), lambda b, h: (b, blk + h))

    def per_head(shape):
        return pl.BlockSpec((None,) + shape, lambda b, h: (h, 0, 0))

    table = pl.BlockSpec((seq_len, HEAD_DIM), lambda b, h: (0, 0))
    state = pl.BlockSpec((None, None, HEAD_DIM, HEAD_DIM), lambda b, h: (b, h, 0, 0))
    kern = functools.partial(_retention_kernel, chunk=chunk, n_chunks=seq_len // chunk,
                             mm_dtype=BF16 if chunk >= 16 else F32)
    return pl.pallas_call(
        kern,
        grid=(n_seq, N_HEADS),
        in_specs=[col(CQ_BLK), col(CK_BLK), col(CV_BLK), col(CG_BLK), table, table,
                  per_head((chunk, chunk)), per_head((chunk, HEAD_DIM)), per_head((chunk, HEAD_DIM)),
                  per_head((1, HEAD_DIM)), pl.BlockSpec((1, HEAD_DIM), lambda b, h: (0, h)), state],
        out_specs=[pl.BlockSpec((seq_len, HEAD_DIM), lambda b, h: (b, h)), state],
        out_shape=[jax.ShapeDtypeStruct((tokens, WIDTH), out_dtype),
                   jax.ShapeDtypeStruct((n_seq, N_HEADS, HEAD_DIM, HEAD_DIM), F32)],
        compiler_params=_params("parallel", "parallel"),
        name="retention",
    )(z, z, z, z, cos, sin, dmat, qdec, kdec, cdec, gn_g.reshape(1, WIDTH), s0)


def retention_tables(pos, chunk):
    half = HEAD_DIM // 2
    inv = ROPE_BASE ** (-jnp.arange(half, dtype=F32) / half)
    ang = pos.astype(F32)[:, None] * inv[None, :]
    cos = jnp.concatenate([jnp.cos(ang), jnp.cos(ang)], axis=-1)
    sin = jnp.concatenate([-jnp.sin(ang), jnp.sin(ang)], axis=-1)
    lg = jnp.log1p(-jnp.exp2(-5.0 - jnp.arange(N_HEADS, dtype=F32)))
    idx = jnp.arange(chunk, dtype=F32)
    diff = idx[:, None] - idx[None, :]
    dmat = jnp.where(diff >= 0, jnp.exp(jnp.maximum(diff, 0.0)[None] * lg[:, None, None]), 0.0)
    qdec = jnp.exp((idx[None, :] + 1.0) * lg[:, None])
    kdec = jnp.exp((chunk - 1.0 - idx)[None, :] * lg[:, None])
    cdec = jnp.exp(chunk * lg)
    lanes = (N_HEADS, chunk, HEAD_DIM)
    return (cos, sin, dmat, jnp.broadcast_to(qdec[:, :, None], lanes),
            jnp.broadcast_to(kdec[:, :, None], lanes),
            jnp.broadcast_to(cdec[:, None, None], (N_HEADS, 1, HEAD_DIM)))


HALO = 8


def _conv3(stage_ref, w_ref, rows):
    return (w_ref[2:3, :] * stage_ref[HALO:HALO + rows, :]
            + w_ref[1:2, :] * stage_ref[HALO - 1:HALO - 1 + rows, :]
            + w_ref[0:1, :] * stage_ref[HALO - 2:HALO - 2 + rows, :])


def _short_conv_kernel(bg_ref, cg_ref, hd_ref, w_ref, hist_ref, o_ref, hist_out_ref, stage_ref, *, rows):
    i = pl.program_id(1)

    @pl.when(i == 0)
    def _():
        stage_ref[HALO - 2:HALO, :] = hist_ref[...]

    stage_ref[HALO:HALO + rows, :] = cg_ref[...] * hd_ref[...]
    y = _conv3(stage_ref, w_ref, rows)
    o_ref[...] = (bg_ref[...] * y).astype(o_ref.dtype)
    tail = stage_ref[rows:rows + HALO, :]
    stage_ref[0:HALO, :] = tail
    hist_out_ref[...] = tail[HALO - 2:, :]


def short_conv(z, w_conv, hist, n_seq, seq_len, rows, out_dtype):
    tokens = z.shape[0]
    per = seq_len // rows

    def col(blk):
        return pl.BlockSpec((rows, WIDTH), lambda b, i: (b * per + i, blk))

    hist_spec = pl.BlockSpec((None, CONV_WIDTH - 1, WIDTH), lambda b, i: (b, 0, 0))
    return pl.pallas_call(
        functools.partial(_short_conv_kernel, rows=rows),
        grid=(n_seq, per),
        in_specs=[col(9), col(10), col(11), pl.BlockSpec((CONV_WIDTH, WIDTH), lambda b, i: (0, 0)), hist_spec],
        out_specs=[pl.BlockSpec((rows, WIDTH), lambda b, i: (b * per + i, 0)), hist_spec],
        out_shape=[jax.ShapeDtypeStruct((tokens, WIDTH), out_dtype),
                   jax.ShapeDtypeStruct((n_seq, CONV_WIDTH - 1, WIDTH), F32)],
        scratch_shapes=[pltpu.VMEM((HALO + rows, WIDTH), F32)],
        compiler_params=_params("arbitrary", "arbitrary"),
        name="short_conv",
    )(z, z, z, w_conv, hist)


def _merge_kernel(h_ref, wm_ref, bm_ref, o_ref, wb_ref, out_ref, acc_ref):
    n = pl.program_id(2)
    gate = jnp.dot(h_ref[...], wm_ref[...], preferred_element_type=F32) + bm_ref[...]
    proj = jnp.dot(o_ref[...].astype(BF16), wb_ref[...], preferred_element_type=F32)
    term = jax.nn.sigmoid(gate) * proj

    @pl.when(n == 0)
    def _():
        acc_ref[...] = term

    @pl.when(n > 0)
    def _():
        acc_ref[...] += term

    @pl.when(n == pl.num_programs(2) - 1)
    def _():
        out_ref[...] = acc_ref[...].astype(out_ref.dtype)


def _merge_cast_kernel(h_ref, wm_ref, bm_ref, o_ref, wb_ref, out_ref, wm16_ref, wb16_ref, acc_ref):
    n = pl.program_id(1)
    wm = wm_ref[...].astype(BF16)
    wb = wb_ref[...].astype(BF16)
    wm16_ref[...] = wm
    wb16_ref[...] = wb
    gate = jnp.dot(h_ref[...], wm, preferred_element_type=F32) + bm_ref[...]
    proj = jnp.dot(o_ref[...].astype(BF16), wb, preferred_element_type=F32)
    term = jax.nn.sigmoid(gate) * proj

    @pl.when(n == 0)
    def _():
        acc_ref[...] = term

    @pl.when(n > 0)
    def _():
        acc_ref[...] += term

    @pl.when(n == pl.num_programs(1) - 1)
    def _():
        out_ref[...] = acc_ref[...].astype(out_ref.dtype)


def gated_merge_cast(h, branches, w_merge, b_merge, w_branch, layer, bn):
    nb, tokens, _ = branches.shape
    depth = w_merge.shape[0]
    return pl.pallas_call(
        _merge_cast_kernel,
        grid=(D_MODEL // bn, nb),
        in_specs=[pl.BlockSpec((tokens, D_MODEL), lambda j, n: (0, 0)),
                  pl.BlockSpec((None, None, D_MODEL, bn), lambda j, n: (layer, n, 0, j)),
                  pl.BlockSpec((None, None, 1, bn), lambda j, n: (layer, n, 0, j)),
                  pl.BlockSpec((None, tokens, WIDTH), lambda j, n: (n, 0, 0)),
                  pl.BlockSpec((None, None, WIDTH, bn), lambda j, n: (layer, n, 0, j))],
        out_specs=[pl.BlockSpec((tokens, bn), lambda j, n: (0, j)),
                   pl.BlockSpec((None, None, D_MODEL, bn), lambda j, n: (0, n, 0, j)),
                   pl.BlockSpec((None, None, WIDTH, bn), lambda j, n: (0, n, 0, j))],
        out_shape=[jax.ShapeDtypeStruct((tokens, D_MODEL), BF16),
                   jax.ShapeDtypeStruct((1, nb, D_MODEL, D_MODEL), BF16),
                   jax.ShapeDtypeStruct((1, nb, WIDTH, D_MODEL), BF16)],
        scratch_shapes=[pltpu.VMEM((tokens, bn), F32)],
        compiler_params=_params("arbitrary", "arbitrary"),
        name="gated_merge_cast",
    )(h, w_merge, b_merge.reshape(depth, nb, 1, D_MODEL), branches, w_branch)


def gated_merge(h, branches, w_merge, b_merge, w_branch, layer, bm, bn):
    nb, tokens, _ = branches.shape
    depth = b_merge.shape[0]
    return pl.pallas_call(
        _merge_kernel,
        grid=(tokens // bm, D_MODEL // bn, nb),
        in_specs=[pl.BlockSpec((bm, D_MODEL), lambda i, j, n: (i, 0)),
                  pl.BlockSpec((None, None, D_MODEL, bn), lambda i, j, n: (0, n, 0, j)),
                  pl.BlockSpec((None, None, 1, bn), lambda i, j, n: (layer, n, 0, j)),
                  pl.BlockSpec((None, bm, WIDTH), lambda i, j, n: (n, i, 0)),
                  pl.BlockSpec((None, None, WIDTH, bn), lambda i, j, n: (0, n, 0, j))],
        out_specs=pl.BlockSpec((bm, bn), lambda i, j, n: (i, j)),
        out_shape=jax.ShapeDtypeStruct((tokens, D_MODEL), BF16),
        scratch_shapes=[pltpu.VMEM((bm, bn), F32)],
        compiler_params=_params("parallel", "arbitrary", "arbitrary"),
        name="gated_merge",
    )(h, w_merge, b_merge.reshape(depth, nb, 1, D_MODEL), branches, w_branch)


FF_BLK = 256
FF_HALF_BLKS = D_FF // FF_BLK


def _ffn_up_kernel(h_ref, wa0_ref, wa1_ref, wb0_ref, wb1_ref, cwa_ref, cwb_ref, ba_ref, bb_ref,
                   ha_ref, hb_ref, act_ref, ta_ref, tb_ref, sa_ref, sb_ref, ca_ref, cb_ref,
                   *, rows, per_seq):
    i = pl.program_id(0)
    j = pl.program_id(1)
    first = (i % per_seq) == 0

    @pl.when(first)
    def _():
        sa_ref[HALO - 2:HALO, :] = ha_ref[...]
        sb_ref[HALO - 2:HALO, :] = hb_ref[...]

    @pl.when(jnp.logical_not(first))
    def _():
        sa_ref[0:HALO, :] = ca_ref[j]
        sb_ref[0:HALO, :] = cb_ref[j]

    h = h_ref[...]
    cur = slice(HALO, HALO + rows)
    sa_ref[cur, :FF_BLK] = jnp.dot(h, wa0_ref[...], preferred_element_type=F32)
    sb_ref[cur, :FF_BLK] = jnp.dot(h, wb0_ref[...], preferred_element_type=F32)
    sa_ref[cur, FF_BLK:] = jnp.dot(h, wa1_ref[...], preferred_element_type=F32)
    sb_ref[cur, FF_BLK:] = jnp.dot(h, wb1_ref[...], preferred_element_type=F32)
    a = _conv3(sa_ref, cwa_ref, rows) + ba_ref[...]
    b = _conv3(sb_ref, cwb_ref, rows) + bb_ref[...]
    act_ref[...] = (a * jax.nn.sigmoid(a) * b).astype(act_ref.dtype)
    tail_a = sa_ref[rows:rows + HALO, :]
    tail_b = sb_ref[rows:rows + HALO, :]
    ca_ref[j] = tail_a
    cb_ref[j] = tail_b
    ta_ref[...] = tail_a
    tb_ref[...] = tail_b


def ffn_up(h, w_up, layer, cwa, cwb, ba, bb, hist_a, hist_b, n_seq, seq_len, bm):
    tokens = h.shape[0]
    per_seq = seq_len // bm
    bn = 2 * FF_BLK
    nj = D_FF_PAD // bn
    last_blk = 2 * FF_HALF_BLKS - 1

    def wspec(first_blk, sub):
        return pl.BlockSpec((None, D_MODEL, FF_BLK),
                            lambda i, j: (layer, 0, jnp.minimum(first_blk + 2 * j + sub, last_blk)))

    cspec = pl.BlockSpec((CONV_WIDTH, bn), lambda i, j: (0, j))
    bspec = pl.BlockSpec((1, bn), lambda i, j: (0, j))
    hspec = pl.BlockSpec((None, CONV_WIDTH - 1, bn), lambda i, j: (i // per_seq, 0, j))
    tspec = pl.BlockSpec((None, HALO, bn), lambda i, j: (i, 0, j))
    tail_shape = jax.ShapeDtypeStruct((tokens // bm, HALO, D_FF_PAD), F32)
    act, tail_a, tail_b = pl.pallas_call(
        functools.partial(_ffn_up_kernel, rows=bm, per_seq=per_seq),
        grid=(tokens // bm, nj),
        in_specs=[pl.BlockSpec((bm, D_MODEL), lambda i, j: (i, 0)),
                  wspec(0, 0), wspec(0, 1), wspec(FF_HALF_BLKS, 0), wspec(FF_HALF_BLKS, 1),
                  cspec, cspec, bspec, bspec, hspec, hspec],
        out_specs=[pl.BlockSpec((bm, bn), lambda i, j: (i, j)), tspec, tspec],
        out_shape=[jax.ShapeDtypeStruct((tokens, D_FF_PAD), BF16), tail_shape, tail_shape],
        scratch_shapes=[pltpu.VMEM((HALO + bm, bn), F32), pltpu.VMEM((HALO + bm, bn), F32),
                        pltpu.VMEM((nj, HALO, bn), F32), pltpu.VMEM((nj, HALO, bn), F32)],
        compiler_params=_params("arbitrary", "arbitrary"),
        name="ffn_up",
    )(h, w_up, w_up, w_up, w_up, cwa, cwb, ba, bb, hist_a, hist_b)
    return act, tail_a[per_seq - 1::per_seq], tail_b[per_seq - 1::per_seq]


def _ffn_act_kernel(ua_ref, ub_ref, cwa_ref, cwb_ref, ba_ref, bb_ref, ha_ref, hb_ref, act_ref,
                    sa_ref, sb_ref, *, rows):
    sa_ref[HALO - 2:HALO, :] = ha_ref[...]
    sb_ref[HALO - 2:HALO, :] = hb_ref[...]
    sa_ref[HALO:HALO + rows, :] = ua_ref[...]
    sb_ref[HALO:HALO + rows, :] = ub_ref[...]
    a = _conv3(sa_ref, cwa_ref, rows) + ba_ref[...]
    b = _conv3(sb_ref, cwb_ref, rows) + bb_ref[...]
    act_ref[...] = (a * jax.nn.sigmoid(a) * b).astype(act_ref.dtype)


def ffn_act(up, w_conv, bias, hist, n_seq, seq_len):
    tokens = up.shape[0]
    bn = D_FF // 2
    cspec = [pl.BlockSpec((CONV_WIDTH, bn), lambda b, j, o=o: (0, o + j)) for o in (0, 2)]
    bspec = [pl.BlockSpec((1, bn), lambda b, j, o=o: (0, o + j)) for o in (0, 2)]
    hspec = [pl.BlockSpec((None, CONV_WIDTH - 1, bn), lambda b, j, o=o: (b, 0, o + j)) for o in (0, 2)]
    uspec = [pl.BlockSpec((seq_len, bn), lambda b, j, o=o: (b, o + j)) for o in (0, 2)]
    return pl.pallas_call(
        functools.partial(_ffn_act_kernel, rows=seq_len),
        grid=(n_seq, 2),
        in_specs=uspec + cspec + bspec + hspec,
        out_specs=pl.BlockSpec((seq_len, bn), lambda b, j: (b, j)),
        out_shape=jax.ShapeDtypeStruct((tokens, D_FF), F32),
        scratch_shapes=[pltpu.VMEM((HALO + seq_len, bn), F32), pltpu.VMEM((HALO + seq_len, bn), F32)],
        compiler_params=_params("parallel", "parallel"),
        name="ffn_act",
    )(up, up, w_conv, w_conv, bias, bias, hist, hist)


WZ_K = 512


def _in_proj_cast_kernel(x_ref, a_ref, b_ref, z_ref, wz_ref):
    j = pl.program_id(0)
    kb = pl.program_id(1)

    def emit(w_t):
        w = w_t.T.astype(BF16)
        wz_ref[...] = w
        part = jnp.dot(x_ref[...], w, preferred_element_type=F32)

        @pl.when(kb == 0)
        def _():
            z_ref[...] = part

        @pl.when(kb > 0)
        def _():
            z_ref[...] += part

    @pl.when(j < FORGET_COL // WIDTH)
    def _():
        emit(a_ref[...])

    @pl.when(j >= FORGET_COL // WIDTH)
    def _():
        emit(jnp.concatenate([a_ref[N_HEADS:, :], b_ref[...]], axis=0))


def in_proj_cast(x, w_in_t, layer):
    m, d = x.shape
    sub = WIDTH // N_HEADS
    return pl.pallas_call(
        _in_proj_cast_kernel,
        grid=(Z_COLS // WIDTH, d // WZ_K),
        in_specs=[pl.BlockSpec((m, WZ_K), lambda j, kb: (0, kb)),
                  pl.BlockSpec((None, WIDTH, WZ_K), lambda j, kb: (layer, j, kb)),
                  pl.BlockSpec((None, N_HEADS, WZ_K), lambda j, kb: (layer, sub * (j + 1), kb))],
        out_specs=[pl.BlockSpec((m, WIDTH), lambda j, kb: (0, j)),
                   pl.BlockSpec((None, WZ_K, WIDTH), lambda j, kb: (0, kb, j))],
        out_shape=[jax.ShapeDtypeStruct((m, Z_COLS), F32), jax.ShapeDtypeStruct((1, d, Z_COLS), BF16)],
        compiler_params=_params("parallel", "arbitrary"),
        name="in_proj_cast",
    )(x, w_in_t, w_in_t)


def _pad_cols(a, n):
    return jnp.pad(a, ((0, 0), (0, n - a.shape[1])))


def _split_hist(hist):
    pad = ((0, 0), (0, 0), (0, D_FF_PAD - D_FF))
    return jnp.pad(hist[..., :D_FF], pad), jnp.pad(hist[..., D_FF:], pad)


def _join_hist(tail_a, tail_b):
    return jnp.concatenate([tail_a[..., :D_FF], tail_b[..., :D_FF]], axis=-1)


def mixer_and_ffn(grp, layer, x, h, mod_op, lw, small, tabs, attend, ret_s0, conv_hist, ffn_hist, next_pre,
                  next_mod_op):
    n_seq, seq_len, tokens = grp.n_seq, grp.seq_len, grp.tokens
    bmm = grp.mm_block
    act_dtype = BF16 if seq_len >= 16 else F32
    cast = "f32" in lw
    lw16 = dict(b_merge=lw["b_merge"]) if cast else lw

    if cast:
        z, lw16["wz"] = in_proj_cast(h, lw["w_in_t"], layer)
    else:
        z = matmul(h, lw["wz"], 0, F32, bmm, 1024)
    logf_t, cum_t = forget_logits(grp, h, small["wf_t"], small["b_forget"])

    chunk_rows = min(seq_len, CHUNK)
    n_chunks = seq_len // chunk_rows
    o_a, v_tail = chunk_gating(z, small["a_ln_g"], small["a_ws"][:, :chunk_rows, :chunk_rows],
                               jnp.broadcast_to(small["a_bs"][:, :chunk_rows, None],
                                                (N_HEADS, chunk_rows, HEAD_DIM)),
                               n_seq, n_chunks, chunk_rows, act_dtype)
    o_b = attend(z, cum_t)
    o_c, ret_state = retention(z, ret_s0, tabs, small["ret_gn_g"], n_seq, seq_len, act_dtype)
    conv_rows = min(seq_len, 512)
    o_d, conv_state = short_conv(z, small["w_sc_conv"], conv_hist, n_seq, seq_len, conv_rows, act_dtype)

    branches = jnp.stack([o_a, o_b.astype(act_dtype), o_c, o_d], axis=0)
    if cast:
        merged, lw16["w_merge"], lw16["w_branch"] = gated_merge_cast(
            h, branches, lw["w_merge"], lw["b_merge"], lw["w_branch"], layer, 512)
        y, lw16["w_out"] = matmul_cast(merged, lw["w_out"], layer, 512)
    else:
        merged = gated_merge(h, branches, lw["w_merge"], lw["b_merge"], lw["w_branch"], layer, bmm, 512)
        y = matmul(merged, lw["w_out"], 0, F32, bmm, 1024)
    x, h2 = residual(grp, x, y, small["g_post_mix"], mod_op, 2, small["g_pre_ffn"], mod_op, 3, 4)

    cw = small["w_ffn_conv"]
    bias = small["b_ffn_conv"].reshape(1, 2 * D_FF)
    if cast:
        up, lw16["w_up"] = matmul_cast(h2, lw["w_up"], layer, 512)
        act = _pad_cols(ffn_act(up, cw, bias, ffn_hist, n_seq, seq_len), D_FF_PAD)
        ffn_state = up.reshape(n_seq, seq_len, 2 * D_FF)[:, seq_len - 2:]
        y2, lw16["w_down"] = matmul_ksplit_cast(act, lw["w_down"], layer, 512, 2816)
    else:
        hist_a, hist_b = _split_hist(ffn_hist)
        act, tail_a, tail_b = ffn_up(h2, lw["w_up"], 0,
                                     _pad_cols(cw[:, :D_FF], D_FF_PAD), _pad_cols(cw[:, D_FF:], D_FF_PAD),
                                     _pad_cols(bias[:, :D_FF], D_FF_PAD), _pad_cols(bias[:, D_FF:], D_FF_PAD),
                                     hist_a, hist_b, n_seq, seq_len, bmm)
        ffn_state = _join_hist(tail_a[:, HALO - 2:], tail_b[:, HALO - 2:])
        y2 = matmul_ksplit(act, lw["w_down"], 0, bmm, 1024, 2816)
    if next_pre is None:
        x = residual(grp, x, y2, small["g_post_ffn"], mod_op, 5)
        h_next = None
    else:
        x, h_next = residual(grp, x, y2, small["g_post_ffn"], mod_op, 5, next_pre, next_mod_op, 0, 1)

    k_b = z[:, BK_BLK * HEAD_DIM:BV_BLK * HEAD_DIM].reshape(n_seq, seq_len, N_HEADS, HEAD_DIM)
    v_b = z[:, BV_BLK * HEAD_DIM:CQ_BLK * HEAD_DIM].reshape(n_seq, seq_len, N_HEADS, HEAD_DIM)
    logf = logf_t.T.reshape(n_seq, seq_len, N_HEADS)
    return x, h_next, (k_b, v_b, logf, ret_state, conv_state, v_tail, ffn_state), lw16


def kernel(x_prompt, x_sample, cache_k, cache_v, cache_logf, state_ret, state_conv, state_ffn_conv,
           page_table, c_prompt, c_sample, w_ada, b_ada, g_pre_mix, g_post_mix, g_pre_ffn, g_post_ffn,
           w_in, b_forget, a_ln_g, a_ws, a_bs, ret_gn_g, w_sc_conv, w_branch, w_merge, b_merge, w_out,
           w_up, w_ffn_conv, b_ffn_conv, w_down):
    depth = w_in.shape[0]
    bp, lp, _ = x_prompt.shape
    bs, ls, _ = x_sample.shape
    grp_p = Group(bp, lp, row_block=256, mm_block=1024)
    grp_s = Group(bs, ls, row_block=bs * ls, mm_block=bs * ls)

    c_all = jnp.pad(jnp.concatenate([c_prompt, c_sample], axis=0), ((0, 16 - bp - bs), (0, 0)))
    mod = ada_modulation(c_all, w_ada, b_ada)

    tabs_p = retention_tables(jnp.arange(lp), math.gcd(lp, CHUNK))
    tabs_s = retention_tables(PAST_LEN + jnp.arange(ls), math.gcd(ls, CHUNK))

    n_pool = cache_k.shape[1]
    cache_k2 = cache_k.reshape(depth, n_pool, PAGE_SIZE * N_HEADS, HEAD_DIM)
    cache_v2 = cache_v.reshape(depth, n_pool, PAGE_SIZE * N_HEADS, HEAD_DIM)
    cache_lp_t = cache_logf.transpose(0, 1, 3, 2)

    xp = x_prompt.reshape(grp_p.tokens, D_MODEL)
    xs = x_sample.reshape(grp_s.tokens, D_MODEL)
    mods_p = [grp_p.mod_operand(mod[l, :bp]) for l in range(depth)]
    mods_s = [grp_s.mod_operand(mod[l, bp:bp + bs]) for l in range(depth)]
    hp = prenorm(grp_p, xp, g_pre_mix[0], mods_p[0], 0, 1)
    hs = prenorm(grp_s, xs, g_pre_mix[0], mods_s[0], 0, 1)

    zeros_ret = jnp.zeros((bp, N_HEADS, HEAD_DIM, HEAD_DIM), F32)
    zeros_conv = jnp.zeros((bp, CONV_WIDTH - 1, WIDTH), F32)
    zeros_ffn = jnp.zeros((bp, CONV_WIDTH - 1, 2 * D_FF), F32)
    eye = jnp.eye(N_HEADS, dtype=F32)

    w_in_t = jnp.swapaxes(w_in, 1, 2)
    lw32 = dict(f32=True, w_in_t=w_in_t, w_merge=w_merge, b_merge=b_merge, w_branch=w_branch,
                w_out=w_out, w_up=w_up, w_down=w_down)

    st_p, st_s = [], []
    for l in range(depth):
        wf_t = w_in_t[l, FORGET_COL:FORGET_COL + N_HEADS]
        small = dict(wf_t=jnp.pad(wf_t, ((0, 16 - N_HEADS), (0, 0))),
                     b_forget=b_forget[l], a_ln_g=a_ln_g[l], a_ws=a_ws[l], a_bs=a_bs[l],
                     ret_gn_g=ret_gn_g[l], w_sc_conv=w_sc_conv[l],
                     w_ffn_conv=w_ffn_conv[l], b_ffn_conv=b_ffn_conv[l],
                     g_post_mix=g_post_mix[l], g_pre_ffn=g_pre_ffn[l], g_post_ffn=g_post_ffn[l])
        next_pre = g_pre_mix[l + 1] if l + 1 < depth else None

        def attend_p(z, cum_t):
            return fox_prompt(z, cum_t, bp, lp)

        def attend_s(z, cum_t, l=l):
            zs = z.reshape(bs, ls, Z_COLS)
            q = zs[..., BQ_BLK * HEAD_DIM:BK_BLK * HEAD_DIM].reshape(bs, ls, N_HEADS, HEAD_DIM)
            q = q.transpose(0, 2, 1, 3) * HEAD_DIM ** -0.5
            qbd = (q[:, :, :, None, :] * eye[None, :, None, :, None]).reshape(bs, N_HEADS * ls, WIDTH)
            k_new = zs[..., BK_BLK * HEAD_DIM:BV_BLK * HEAD_DIM]
            v_new = zs[..., BV_BLK * HEAD_DIM:CQ_BLK * HEAD_DIM]
            cum = cum_t.reshape(N_HEADS, bs, ls).transpose(1, 0, 2)
            fcol = cum.reshape(bs, N_HEADS * ls, 1)
            frow = jnp.broadcast_to(cum[:, :, None, :], (bs, N_HEADS, ls, ls)).reshape(bs, N_HEADS * ls, ls)
            o = fox_sample(page_table, qbd.astype(BF16), k_new, v_new, fcol, frow,
                           cache_k2, cache_v2, cache_lp_t, l)
            return o.reshape(bs * ls, WIDTH)

        xs, hs, ss, lw16 = mixer_and_ffn(grp_s, l, xs, hs, mods_s[l], lw32, small, tabs_s, attend_s,
                                         state_ret[l], state_conv[l], state_ffn_conv[l], next_pre,
                                         mods_s[l + 1] if l + 1 < depth else None)
        xp, hp, sp, _ = mixer_and_ffn(grp_p, l, xp, hp, mods_p[l], lw16, small, tabs_p, attend_p,
                                      zeros_ret, zeros_conv, zeros_ffn, next_pre,
                                      mods_p[l + 1] if l + 1 < depth else None)
        st_p.append(sp)
        st_s.append(ss)

    def stack(states, idx):
        return jnp.stack([s[idx] for s in states], axis=0)

    return (xp.reshape(bp, lp, D_MODEL), xs.reshape(bs, ls, D_MODEL),
            stack(st_p, 0), stack(st_p, 1), stack(st_p, 2),
            stack(st_s, 0), stack(st_s, 1), stack(st_s, 2),
            stack(st_p, 3), stack(st_s, 3), stack(st_p, 4), stack(st_s, 4),
            stack(st_p, 5), stack(st_s, 5), stack(st_p, 6), stack(st_s, 6))
```

```python
import functools
import math

import jax
import jax.numpy as jnp
from jax import lax
from jax.experimental import pallas as pl
from jax.experimental.pallas import tpu as pltpu

D_MODEL = 4096
HEAD_DIM = 128
N_HEADS = 8
WIDTH = N_HEADS * HEAD_DIM
CHUNK = 128
PAST_LEN = 16384
PAGE_SIZE = 128
ROPE_BASE = 10000.0
CONV_WIDTH = 3
D_FF = 11008
D_FF_PAD = 11264
N_MOD = 6
EPS = 1e-6
NEG_INF = -1e30
Z_COLS = 12 * WIDTH
FORGET_COL = 5 * WIDTH

BQ_BLK, BK_BLK, BV_BLK = 16, 24, 32
CQ_BLK, CK_BLK, CV_BLK, CG_BLK = 40, 48, 56, 64

VMEM_LIMIT_BYTES = 56 * 1024 * 1024

BF16 = jnp.bfloat16
F32 = jnp.float32


def _sigmoid(x):
    return 0.5 * jnp.tanh(0.5 * x) + 0.5


def _params(*semantics):
    return pltpu.CompilerParams(dimension_semantics=semantics, vmem_limit_bytes=VMEM_LIMIT_BYTES)


def _ada_kernel(c_ref, w_ref, b_ref, o_ref):
    c = c_ref[...]
    s = (c * jax.nn.sigmoid(c)).astype(BF16)
    o_ref[...] = jnp.dot(s, w_ref[...].astype(BF16), preferred_element_type=F32) + b_ref[...]


def ada_modulation(c_all, w_ada, b_ada):
    depth, d, n = w_ada.shape
    rows = c_all.shape[0]
    bn = 512
    return pl.pallas_call(
        _ada_kernel,
        grid=(depth, n // bn),
        in_specs=[
            pl.BlockSpec((rows, d), lambda l, j: (0, 0)),
            pl.BlockSpec((None, d, bn), lambda l, j: (l, 0, j)),
            pl.BlockSpec((None, 1, bn), lambda l, j: (l, 0, j)),
        ],
        out_specs=pl.BlockSpec((None, rows, bn), lambda l, j: (l, 0, j)),
        out_shape=jax.ShapeDtypeStruct((depth, rows, n), F32),
        compiler_params=_params("arbitrary", "arbitrary"),
        name="ada_modulation",
    )(c_all, w_ada, b_ada.reshape(depth, 1, n))


class Group:
    def __init__(self, n_seq, seq_len, row_block, mm_block):
        self.n_seq = n_seq
        self.seq_len = seq_len
        self.tokens = n_seq * seq_len
        self.row_block = row_block
        self.mm_block = mm_block
        self.per_seq_rows = row_block <= seq_len

    def mod_operand(self, mod):
        if self.per_seq_rows:
            return mod.reshape(self.n_seq, N_MOD, 1, D_MODEL)
        tok = jnp.repeat(mod.reshape(self.n_seq, N_MOD, D_MODEL), self.seq_len, axis=0)
        return tok.transpose(1, 0, 2)

    def mod_spec(self, idx):
        if self.per_seq_rows:
            per = self.seq_len // self.row_block
            return pl.BlockSpec((None, None, 1, D_MODEL), lambda i: (i // per, idx, 0, 0))
        return pl.BlockSpec((None, self.row_block, D_MODEL), lambda i: (idx, i, 0))


def _rms(x, g):
    x = x.astype(F32)
    return x * lax.rsqrt(jnp.mean(x * x, axis=-1, keepdims=True) + EPS) * g


def _prenorm_kernel(x_ref, g_ref, sc_ref, sh_ref, h_ref):
    h_ref[...] = (_rms(x_ref[...], g_ref[...]) * (1.0 + sc_ref[...]) + sh_ref[...]).astype(h_ref.dtype)


def prenorm(grp, x, g, mod_op, sh_idx, sc_idx):
    br = grp.row_block
    row = pl.BlockSpec((br, D_MODEL), lambda i: (i, 0))
    vec = pl.BlockSpec((1, D_MODEL), lambda i: (0, 0))
    return pl.pallas_call(
        _prenorm_kernel,
        grid=(grp.tokens // br,),
        in_specs=[row, vec, grp.mod_spec(sc_idx), grp.mod_spec(sh_idx)],
        out_specs=row,
        out_shape=jax.ShapeDtypeStruct((grp.tokens, D_MODEL), BF16),
        compiler_params=_params("parallel"),
        name="prenorm",
    )(x, g.reshape(1, D_MODEL), mod_op, mod_op)


def _residual_kernel(x_ref, y_ref, gpost_ref, gate_ref, xo_ref):
    xo_ref[...] = x_ref[...] + gate_ref[...] * _rms(y_ref[...], gpost_ref[...])


def _residual_prenorm_kernel(x_ref, y_ref, gpost_ref, gate_ref, gpre_ref, sc_ref, sh_ref, xo_ref, h_ref):
    xn = x_ref[...] + gate_ref[...] * _rms(y_ref[...], gpost_ref[...])
    xo_ref[...] = xn
    h_ref[...] = (_rms(xn, gpre_ref[...]) * (1.0 + sc_ref[...]) + sh_ref[...]).astype(h_ref.dtype)


def residual(grp, x, y, g_post, mod_op, gate_idx, g_pre=None, pre_mod_op=None, sh_idx=None, sc_idx=None):
    br = grp.row_block
    row = pl.BlockSpec((br, D_MODEL), lambda i: (i, 0))
    vec = pl.BlockSpec((1, D_MODEL), lambda i: (0, 0))
    x_shape = jax.ShapeDtypeStruct((grp.tokens, D_MODEL), F32)
    if g_pre is None:
        return pl.pallas_call(
            _residual_kernel,
            grid=(grp.tokens // br,),
            in_specs=[row, row, vec, grp.mod_spec(gate_idx)],
            out_specs=row,
            out_shape=x_shape,
            compiler_params=_params("parallel"),
            name="residual",
        )(x, y, g_post.reshape(1, D_MODEL), mod_op)
    return pl.pallas_call(
        _residual_prenorm_kernel,
        grid=(grp.tokens // br,),
        in_specs=[row, row, vec, grp.mod_spec(gate_idx), vec, grp.mod_spec(sc_idx), grp.mod_spec(sh_idx)],
        out_specs=[row, row],
        out_shape=[x_shape, jax.ShapeDtypeStruct((grp.tokens, D_MODEL), BF16)],
        compiler_params=_params("parallel"),
        name="residual_prenorm",
    )(x, y, g_post.reshape(1, D_MODEL), mod_op, g_pre.reshape(1, D_MODEL), pre_mod_op, pre_mod_op)


def _mm_kernel(x_ref, w_ref, o_ref):
    o_ref[...] = jnp.dot(x_ref[...].astype(BF16), w_ref[...],
                         preferred_element_type=F32).astype(o_ref.dtype)


def matmul(x, w, layer, out_dtype, bm, bn):
    m, k = x.shape
    n = w.shape[2]
    return pl.pallas_call(
        _mm_kernel,
        grid=(m // bm, n // bn),
        in_specs=[pl.BlockSpec((bm, k), lambda i, j: (i, 0)),
                  pl.BlockSpec((None, k, bn), lambda i, j: (layer, 0, j))],
        out_specs=pl.BlockSpec((bm, bn), lambda i, j: (i, j)),
        out_shape=jax.ShapeDtypeStruct((m, n), out_dtype),
        compiler_params=_params("parallel", "arbitrary"),
        name="matmul",
    )(x, w)


KV_COL_TILE = BK_BLK * HEAD_DIM // WIDTH


def _in_proj_kernel(x_ref, w_ref, z_ref, kv_ref):
    j = pl.program_id(1)
    acc = jnp.dot(x_ref[...], w_ref[...], preferred_element_type=F32)
    z_ref[...] = acc.astype(z_ref.dtype)

    @pl.when((j >= KV_COL_TILE) & (j < KV_COL_TILE + 2))
    def _():
        kv_ref[...] = acc


def in_proj(x, w, bm):
    m, k = x.shape
    return pl.pallas_call(
        _in_proj_kernel,
        grid=(m // bm, Z_COLS // WIDTH),
        in_specs=[pl.BlockSpec((bm, k), lambda i, j: (i, 0)),
                  pl.BlockSpec((None, k, WIDTH), lambda i, j: (0, 0, j))],
        out_specs=[pl.BlockSpec((bm, WIDTH), lambda i, j: (i, j)),
                   pl.BlockSpec((bm, WIDTH), lambda i, j: (i, jnp.clip(j - KV_COL_TILE, 0, 1)))],
        out_shape=[jax.ShapeDtypeStruct((m, Z_COLS), BF16), jax.ShapeDtypeStruct((m, 2 * WIDTH), F32)],
        compiler_params=_params("parallel", "arbitrary"),
        name="in_proj",
    )(x, w)


def _mm_ksplit_kernel(x_ref, w_ref, o_ref):
    @pl.when(pl.program_id(2) == 0)
    def _():
        o_ref[...] = jnp.zeros_like(o_ref)

    o_ref[...] += jnp.dot(x_ref[...].astype(BF16), w_ref[...], preferred_element_type=F32)


def matmul_ksplit(x, w, layer, bm, bn, bk):
    m, k = x.shape
    n = w.shape[2]
    return pl.pallas_call(
        _mm_ksplit_kernel,
        grid=(m // bm, n // bn, k // bk),
        in_specs=[pl.BlockSpec((bm, bk), lambda i, j, kk: (i, kk)),
                  pl.BlockSpec((None, bk, bn), lambda i, j, kk: (layer, kk, j))],
        out_specs=pl.BlockSpec((bm, bn), lambda i, j, kk: (i, j)),
        out_shape=jax.ShapeDtypeStruct((m, n), F32),
        compiler_params=_params("parallel", "parallel", "arbitrary"),
        name="matmul_ksplit",
    )(x, w)


def _mm_cast_kernel(x_ref, w_ref, o_ref, w16_ref):
    w = w_ref[...].astype(BF16)
    w16_ref[...] = w
    o_ref[...] = jnp.dot(x_ref[...].astype(BF16), w, preferred_element_type=F32)


def matmul_cast(x, w, layer, bn):
    m, k = x.shape
    n = w.shape[2]
    return pl.pallas_call(
        _mm_cast_kernel,
        grid=(n // bn,),
        in_specs=[pl.BlockSpec((m, k), lambda j: (0, 0)),
                  pl.BlockSpec((None, k, bn), lambda j: (layer, 0, j))],
        out_specs=[pl.BlockSpec((m, bn), lambda j: (0, j)),
                   pl.BlockSpec((None, k, bn), lambda j: (0, 0, j))],
        out_shape=[jax.ShapeDtypeStruct((m, n), F32), jax.ShapeDtypeStruct((1, k, n), BF16)],
        compiler_params=_params("parallel"),
        name="matmul_cast",
    )(x, w)


def _mm_ksplit_cast_kernel(x_ref, w_ref, o_ref, w16_ref, *, bk, k_valid):
    kk = pl.program_id(1)
    row = kk * bk + lax.broadcasted_iota(jnp.int32, w_ref.shape, 0)
    w = jnp.where(row < k_valid, w_ref[...], 0.0).astype(BF16)
    w16_ref[...] = w

    @pl.when(kk == 0)
    def _():
        o_ref[...] = jnp.zeros_like(o_ref)

    o_ref[...] += jnp.dot(x_ref[...].astype(BF16), w, preferred_element_type=F32)


def matmul_ksplit_cast(x, w, layer, bn, bk):
    m, k_pad = x.shape
    k_valid, n = w.shape[1:]
    return pl.pallas_call(
        functools.partial(_mm_ksplit_cast_kernel, bk=bk, k_valid=k_valid),
        grid=(n // bn, k_pad // bk),
        in_specs=[pl.BlockSpec((m, bk), lambda j, kk: (0, kk)),
                  pl.BlockSpec((None, bk, bn), lambda j, kk: (layer, kk, j))],
        out_specs=[pl.BlockSpec((m, bn), lambda j, kk: (0, j)),
                   pl.BlockSpec((None, bk, bn), lambda j, kk: (0, kk, j))],
        out_shape=[jax.ShapeDtypeStruct((m, n), F32), jax.ShapeDtypeStruct((1, k_pad, n), BF16)],
        compiler_params=_params("parallel", "arbitrary"),
        name="matmul_ksplit_cast",
    )(x, w)


def _logf_kernel(h_ref, wf_ref, bf_ref, logf_ref, cum_ref, carry_ref, *, tile, seq_len, cw):
    i = pl.program_id(0)
    logits = lax.dot_general(wf_ref[...].astype(BF16), h_ref[...], (((1,), (1,)), ((), ())),
                             preferred_element_type=F32)[:N_HEADS]
    x = logits + bf_ref[...]
    logf = jnp.minimum(x, 0.0) - jnp.log1p(jnp.exp(-jnp.abs(x)))
    logf_ref[...] = logf

    seg = min(seq_len, cw)
    src = lax.broadcasted_iota(jnp.int32, (cw, cw), 0)
    dst = lax.broadcasted_iota(jnp.int32, (cw, cw), 1)
    tri = ((src <= dst) & (src // seg == dst // seg)).astype(F32)

    if seq_len > tile:
        @pl.when((i * tile) % seq_len == 0)
        def _():
            carry_ref[...] = jnp.zeros_like(carry_ref)
        carry = carry_ref[...]
    else:
        carry = jnp.zeros((N_HEADS, 1), F32)

    for c in range(tile // cw):
        part = jnp.dot(logf[:, c * cw:(c + 1) * cw], tri, precision=lax.Precision.HIGHEST,
                       preferred_element_type=F32)
        cum = part + carry
        cum_ref[:, c * cw:(c + 1) * cw] = cum
        if seq_len > cw:
            carry = cum[:, cw - 1:cw]
    if seq_len > tile:
        carry_ref[...] = carry


def forget_logits(grp, h, wf_t, b_forget):
    tile = min(512, grp.tokens)
    cw = min(128, tile)
    kern = functools.partial(_logf_kernel, tile=tile, seq_len=grp.seq_len, cw=cw)
    out = jax.ShapeDtypeStruct((N_HEADS, grp.tokens), F32)
    return pl.pallas_call(
        kern,
        grid=(grp.tokens // tile,),
        in_specs=[pl.BlockSpec((tile, D_MODEL), lambda i: (i, 0)),
                  pl.BlockSpec((16, D_MODEL), lambda i: (0, 0)),
                  pl.BlockSpec((N_HEADS, 1), lambda i: (0, 0))],
        out_specs=[pl.BlockSpec((N_HEADS, tile), lambda i: (0, i))] * 2,
        out_shape=[out, out],
        scratch_shapes=[pltpu.VMEM((N_HEADS, 1), F32)],
        compiler_params=_params("arbitrary"),
        name="forget_logits",
    )(h, wf_t, b_forget.reshape(N_HEADS, 1))


def _gating_kernel(z_ref, lng_ref, ws_ref, bs_ref, o_ref, vt_ref, *, rows, mm_dtype):
    z = jax.nn.gelu(z_ref[...].astype(F32))
    u = z[:, :WIDTH]
    v = z[:, WIDTH:]
    vc = v - jnp.mean(v, axis=-1, keepdims=True)
    vn = vc * lax.rsqrt(jnp.mean(vc * vc, axis=-1, keepdims=True) + EPS) * lng_ref[...]
    vt_ref[...] = vn
    t_idx = lax.broadcasted_iota(jnp.int32, (rows, rows), 0)
    s_idx = lax.broadcasted_iota(jnp.int32, (rows, rows), 1)
    causal = s_idx <= t_idx
    for g in range(N_HEADS):
        cols = slice(g * HEAD_DIM, (g + 1) * HEAD_DIM)
        ws = jnp.where(causal, ws_ref[g], 0.0).astype(mm_dtype)
        mixed = jnp.dot(ws, vn[:, cols].astype(mm_dtype), preferred_element_type=F32) + bs_ref[g]
        o_ref[:, cols] = (u[:, cols] * mixed).astype(o_ref.dtype)


def chunk_gating(z, ln_g, ws, bs_b, n_seq, n_chunks, rows, out_dtype):
    tokens = z.shape[0]
    kern = functools.partial(_gating_kernel, rows=rows, mm_dtype=BF16 if rows >= 16 else F32)
    return pl.pallas_call(
        kern,
        grid=(n_seq, n_chunks),
        in_specs=[pl.BlockSpec((rows, 2 * WIDTH), lambda b, n: (b * n_chunks + n, 0)),
                  pl.BlockSpec((1, WIDTH), lambda b, n: (0, 0)),
                  pl.BlockSpec((N_HEADS, rows, rows), lambda b, n: (0, 0, 0)),
                  pl.BlockSpec((N_HEADS, rows, HEAD_DIM), lambda b, n: (0, 0, 0))],
        out_specs=[pl.BlockSpec((rows, WIDTH), lambda b, n: (b * n_chunks + n, 0)),
                   pl.BlockSpec((None, rows, WIDTH), lambda b, n: (b, 0, 0))],
        out_shape=[jax.ShapeDtypeStruct((tokens, WIDTH), out_dtype),
                   jax.ShapeDtypeStruct((n_seq, rows, WIDTH), F32)],
        compiler_params=_params("arbitrary", "arbitrary"),
        name="chunk_gating",
    )(z, ln_g.reshape(1, WIDTH), ws, bs_b)


def _fox_prompt_kernel(q_ref, k_ref, v_ref, fq_ref, fk_ref, o_ref, *, blk):
    qi = pl.program_id(2)
    q = (q_ref[...].astype(F32) * HEAD_DIM ** -0.5).astype(BF16)
    fq = fq_ref[...]
    q_pos = qi * blk + lax.broadcasted_iota(jnp.int32, (blk, blk), 0)
    k_off = lax.broadcasted_iota(jnp.int32, (blk, blk), 1)

    def step(kb, carry, masked):
        m, l, acc = carry
        ks = pl.multiple_of(kb * blk, blk)
        k = k_ref[pl.ds(ks, blk), :].astype(BF16)
        v = v_ref[pl.ds(ks, blk), :].astype(BF16)
        s = lax.dot_general(q, k, (((1,), (1,)), ((), ())), preferred_element_type=F32)
        s = s + fq - fk_ref[kb]
        if masked:
            s = jnp.where(ks + k_off <= q_pos, s, NEG_INF)
        m_new = jnp.maximum(m, jnp.max(s, axis=-1, keepdims=True))
        alpha = jnp.exp(m - m_new)
        p = jnp.exp(s - m_new)
        l = alpha * l + jnp.sum(p, axis=-1, keepdims=True)
        acc = alpha * acc + jnp.dot(p.astype(BF16), v, preferred_element_type=F32)
        return m_new, l, acc

    init = (jnp.full((blk, 1), NEG_INF, F32), jnp.zeros((blk, 1), F32), jnp.zeros((blk, HEAD_DIM), F32))
    carry = lax.fori_loop(0, qi, functools.partial(step, masked=False), init)
    _, l, acc = step(qi, carry, masked=True)
    o_ref[...] = (acc / l).astype(o_ref.dtype)


def fox_prompt(z, cum_t, n_seq, seq_len):
    blk = 512
    nq = seq_len // blk
    tokens = z.shape[0]
    fq = cum_t.reshape(N_HEADS, tokens, 1)
    fk = cum_t.reshape(N_HEADS, tokens // blk, 1, blk)
    return pl.pallas_call(
        functools.partial(_fox_prompt_kernel, blk=blk),
        grid=(n_seq, N_HEADS, nq),
        in_specs=[pl.BlockSpec((blk, HEAD_DIM), lambda b, h, i: (b * nq + i, BQ_BLK + h)),
                  pl.BlockSpec((seq_len, HEAD_DIM), lambda b, h, i: (b, BK_BLK + h)),
                  pl.BlockSpec((seq_len, HEAD_DIM), lambda b, h, i: (b, BV_BLK + h)),
                  pl.BlockSpec((None, blk, 1), lambda b, h, i: (h, b * nq + i, 0)),
                  pl.BlockSpec((None, nq, 1, blk), lambda b, h, i: (h, b, 0, 0))],
        out_specs=pl.BlockSpec((blk, HEAD_DIM), lambda b, h, i: (b * nq + i, h)),
        out_shape=jax.ShapeDtypeStruct((tokens, WIDTH), BF16),
        compiler_params=_params("parallel", "parallel", "arbitrary"),
        name="fox_prompt",
    )(z, z, z, fq, fk)


def _fox_sample_kernel(pt_ref, qbd_ref, kn_ref, vn_ref, fcol_ref, frow_ref, *rest, pages, n_new):
    page_refs = rest[:3 * pages]
    o_ref, kbuf_ref, vbuf_ref, m_ref, l_ref, acc_ref, carry_ref = rest[3 * pages:]
    del pt_ref
    j = pl.program_id(1)
    rows = N_HEADS * n_new
    qbd = qbd_ref[...]
    fcol = fcol_ref[...]

    def update(s, v):
        m_old = m_ref[...]
        m_new = jnp.maximum(m_old, jnp.max(s, axis=-1, keepdims=True))
        alpha = jnp.exp(m_old - m_new)
        p = jnp.exp(s - m_new)
        l_ref[...] = alpha * l_ref[...] + jnp.sum(p, axis=-1, keepdims=True)
        acc_ref[...] = alpha * acc_ref[...] + jnp.dot(p.astype(BF16), v, preferred_element_type=F32)
        m_ref[...] = m_new

    @pl.when(j == 0)
    def _():
        m_ref[...] = jnp.full_like(m_ref, NEG_INF)
        l_ref[...] = jnp.zeros_like(l_ref)
        acc_ref[...] = jnp.zeros_like(acc_ref)
        carry_ref[...] = jnp.zeros_like(carry_ref)
        kn = kn_ref[...].astype(BF16)
        s = lax.dot_general(qbd, kn, (((1,), (1,)), ((), ())), preferred_element_type=F32)
        s = s + fcol - frow_ref[...]
        t_idx = lax.broadcasted_iota(jnp.int32, (rows, n_new), 0) % n_new
        s_idx = lax.broadcasted_iota(jnp.int32, (rows, n_new), 1)
        s = jnp.where(s_idx <= t_idx, s, NEG_INF)
        update(s, vn_ref[...].astype(BF16))

    lane = lax.broadcasted_iota(jnp.int32, (N_HEADS, PAGE_SIZE), 1)
    later = carry_ref[...]
    bias_parts = []
    for p in range(pages):
        k_ref, v_ref, lp_ref = page_refs[3 * p:3 * p + 3]
        key_rows = slice(p * PAGE_SIZE, (p + 1) * PAGE_SIZE)
        for h in range(N_HEADS):
            head_rows = pl.ds(h, PAGE_SIZE, stride=N_HEADS)
            cols = slice(h * HEAD_DIM, (h + 1) * HEAD_DIM)
            kbuf_ref[key_rows, cols] = k_ref[head_rows, :].astype(BF16)
            vbuf_ref[key_rows, cols] = v_ref[head_rows, :].astype(BF16)
        lp = lp_ref[...]
        suf = lp
        step = 1
        while step < PAGE_SIZE:
            shifted = pltpu.roll(suf, PAGE_SIZE - step, axis=1)
            suf = suf + jnp.where(lane + step < PAGE_SIZE, shifted, 0.0)
            step *= 2
        bias_parts.append(suf - lp + later)
        later = later + suf[:, 0:1]
    carry_ref[...] = later
    n_keys = pages * PAGE_SIZE
    bias = jnp.concatenate(bias_parts, axis=1)
    bias = jnp.broadcast_to(bias[:, None, :], (N_HEADS, n_new, n_keys)).reshape(rows, n_keys)
    s = lax.dot_general(qbd, kbuf_ref[...], (((1,), (1,)), ((), ())), preferred_element_type=F32)
    update(s + fcol + bias, vbuf_ref[...])

    @pl.when(j == pl.num_programs(1) - 1)
    def _():
        inv = 1.0 / l_ref[...]
        for h in range(N_HEADS):
            r = slice(h * n_new, (h + 1) * n_new)
            c = slice(h * HEAD_DIM, (h + 1) * HEAD_DIM)
            o_ref[:, c] = acc_ref[r, c] * inv[r]


def fox_sample(page_table, qbd, k_new, v_new, fcol, frow, cache_k, cache_v, cache_lp_t, layer):
    n_seq, n_pages = page_table.shape
    n_new = k_new.shape[1]
    rows = N_HEADS * n_new
    pages = 8
    steps = n_pages // pages

    def page_map(p):
        def index(b, j, pt):
            return (layer, pt[b, n_pages - 1 - (j * pages + p)], 0, 0)
        return index

    page_specs = []
    page_args = []
    for p in range(pages):
        page_specs += [pl.BlockSpec((None, None, PAGE_SIZE * N_HEADS, HEAD_DIM), page_map(p)),
                       pl.BlockSpec((None, None, PAGE_SIZE * N_HEADS, HEAD_DIM), page_map(p)),
                       pl.BlockSpec((None, None, N_HEADS, PAGE_SIZE), page_map(p))]
        page_args += [cache_k, cache_v, cache_lp_t]

    def per_seq(shape):
        return pl.BlockSpec((None,) + shape, lambda b, j, pt: (b, 0, 0))

    grid_spec = pltpu.PrefetchScalarGridSpec(
        num_scalar_prefetch=1,
        grid=(n_seq, steps),
        in_specs=[per_seq((rows, WIDTH)), per_seq((n_new, WIDTH)), per_seq((n_new, WIDTH)),
                  per_seq((rows, 1)), per_seq((rows, n_new))] + page_specs,
        out_specs=per_seq((n_new, WIDTH)),
        scratch_shapes=[pltpu.VMEM((pages * PAGE_SIZE, WIDTH), BF16), pltpu.VMEM((pages * PAGE_SIZE, WIDTH), BF16),
                        pltpu.VMEM((rows, 1), F32), pltpu.VMEM((rows, 1), F32),
                        pltpu.VMEM((rows, WIDTH), F32), pltpu.VMEM((N_HEADS, 1), F32)],
    )
    return pl.pallas_call(
        functools.partial(_fox_sample_kernel, pages=pages, n_new=n_new),
        grid_spec=grid_spec,
        out_shape=jax.ShapeDtypeStruct((n_seq, n_new, WIDTH), F32),
        compiler_params=_params("arbitrary", "arbitrary"),
        name="fox_sample",
    )(page_table, qbd, k_new, v_new, fcol, frow, *page_args)


def _retention_kernel(q_ref, k_ref, v_ref, g_ref, cos_ref, sin_ref, dmat_ref, qdec_ref, kdec_ref,
                      cdec_ref, gn_ref, s0_ref, o_ref, s_ref, *, chunk, n_chunks, mm_dtype):
    dmat = dmat_ref[...]
    qdec = qdec_ref[...]
    kdec = kdec_ref[...]
    cdec = cdec_ref[...]
    gn = gn_ref[...]

    def rope(x, cos, sin):
        return x * cos + pltpu.roll(x, HEAD_DIM // 2, axis=1) * sin

    def body(i, state):
        r = pl.multiple_of(i * chunk, chunk)
        rows = pl.ds(r, chunk)
        cos = cos_ref[rows, :]
        sin = sin_ref[rows, :]
        q = rope(q_ref[rows, :].astype(F32), cos, sin)
        k = rope(k_ref[rows, :].astype(F32), cos, sin) * HEAD_DIM ** -0.5
        v = v_ref[rows, :].astype(mm_dtype)
        inner = lax.dot_general(q.astype(mm_dtype), k.astype(mm_dtype), (((1,), (1,)), ((), ())),
                                preferred_element_type=F32) * dmat
        o = (jnp.dot(inner.astype(mm_dtype), v, preferred_element_type=F32)
             + jnp.dot((q * qdec).astype(mm_dtype), state.astype(mm_dtype), preferred_element_type=F32))
        kd_t = (k * kdec).T.astype(mm_dtype)
        state = state * cdec + jnp.dot(kd_t, v, preferred_element_type=F32)
        oc = o - jnp.mean(o, axis=-1, keepdims=True)
        y = oc * lax.rsqrt(jnp.mean(oc * oc, axis=-1, keepdims=True) + EPS) * gn
        gate = g_ref[rows, :].astype(F32)
        o_ref[rows, :] = (gate * jax.nn.sigmoid(gate) * y).astype(o_ref.dtype)
        return state

    s_ref[...] = lax.fori_loop(0, n_chunks, body, s0_ref[...], unroll=min(8, n_chunks))


def retention(z, s0, tabs, gn_g, n_seq, seq_len, out_dtype):
    cos, sin, dmat, qdec, kdec, cdec = tabs
    chunk = dmat.shape[-1]
    tokens = z.shape[0]

    def col(blk):
        return pl.BlockSpec((seq_len, HEAD_DIM), lambda b, h: (b, blk + h))

    def per_head(shape):
        return pl.BlockSpec((None,) + shape, lambda b, h: (h, 0, 0))

    table = pl.BlockSpec((seq_len, HEAD_DIM), lambda b, h: (0, 0))
    state = pl.BlockSpec((None, None, HEAD_DIM, HEAD_DIM), lambda b, h: (b, h, 0, 0))
    kern = functools.partial(_retention_kernel, chunk=chunk, n_chunks=seq_len // chunk,
                             mm_dtype=BF16 if chunk >= 16 else F32)
    return pl.pallas_call(
        kern,
        grid=(n_seq, N_HEADS),
        in_specs=[col(CQ_BLK), col(CK_BLK), col(CV_BLK), col(CG_BLK), table, table,
                  per_head((chunk, chunk)), per_head((chunk, HEAD_DIM)), per_head((chunk, HEAD_DIM)),
                  per_head((1, HEAD_DIM)), pl.BlockSpec((1, HEAD_DIM), lambda b, h: (0, h)), state],
        out_specs=[pl.BlockSpec((seq_len, HEAD_DIM), lambda b, h: (b, h)), state],
        out_shape=[jax.ShapeDtypeStruct((tokens, WIDTH), out_dtype),
                   jax.ShapeDtypeStruct((n_seq, N_HEADS, HEAD_DIM, HEAD_DIM), F32)],
        compiler_params=_params("parallel", "parallel"),
        name="retention",
    )(z, z, z, z, cos, sin, dmat, qdec, kdec, cdec, gn_g.reshape(1, WIDTH), s0)


def retention_tables(pos, chunk):
    half = HEAD_DIM // 2
    inv = ROPE_BASE ** (-jnp.arange(half, dtype=F32) / half)
    ang = pos.astype(F32)[:, None] * inv[None, :]
    cos = jnp.concatenate([jnp.cos(ang), jnp.cos(ang)], axis=-1)
    sin = jnp.concatenate([-jnp.sin(ang), jnp.sin(ang)], axis=-1)
    lg = jnp.log1p(-jnp.exp2(-5.0 - jnp.arange(N_HEADS, dtype=F32)))
    idx = jnp.arange(chunk, dtype=F32)
    diff = idx[:, None] - idx[None, :]
    dmat = jnp.where(diff >= 0, jnp.exp(jnp.maximum(diff, 0.0)[None] * lg[:, None, None]), 0.0)
    qdec = jnp.exp((idx[None, :] + 1.0) * lg[:, None])
    kdec = jnp.exp((chunk - 1.0 - idx)[None, :] * lg[:, None])
    cdec = jnp.exp(chunk * lg)
    lanes = (N_HEADS, chunk, HEAD_DIM)
    return (cos, sin, dmat, jnp.broadcast_to(qdec[:, :, None], lanes),
            jnp.broadcast_to(kdec[:, :, None], lanes),
            jnp.broadcast_to(cdec[:, None, None], (N_HEADS, 1, HEAD_DIM)))


HALO = 8


def _conv3(stage_ref, w_ref, rows, start=0):
    base = HALO + start
    return (w_ref[2:3, :] * stage_ref[base:base + rows, :]
            + w_ref[1:2, :] * stage_ref[base - 1:base - 1 + rows, :]
            + w_ref[0:1, :] * stage_ref[base - 2:base - 2 + rows, :])


def _short_conv_kernel(bg_ref, cg_ref, hd_ref, w_ref, hist_ref, o_ref, hist_out_ref, stage_ref, *, rows):
    i = pl.program_id(1)

    @pl.when(i == 0)
    def _():
        stage_ref[HALO - 2:HALO, :] = hist_ref[...]

    stage_ref[HALO:HALO + rows, :] = cg_ref[...].astype(F32) * hd_ref[...].astype(F32)
    y = _conv3(stage_ref, w_ref, rows)
    o_ref[...] = (bg_ref[...].astype(F32) * y).astype(o_ref.dtype)
    tail = stage_ref[rows:rows + HALO, :]
    stage_ref[0:HALO, :] = tail
    hist_out_ref[...] = tail[HALO - 2:, :]


def short_conv(z, w_conv, hist, n_seq, seq_len, rows, out_dtype):
    tokens = z.shape[0]
    per = seq_len // rows

    def col(blk):
        return pl.BlockSpec((rows, WIDTH), lambda b, i: (b * per + i, blk))

    hist_spec = pl.BlockSpec((None, CONV_WIDTH - 1, WIDTH), lambda b, i: (b, 0, 0))
    return pl.pallas_call(
        functools.partial(_short_conv_kernel, rows=rows),
        grid=(n_seq, per),
        in_specs=[col(9), col(10), col(11), pl.BlockSpec((CONV_WIDTH, WIDTH), lambda b, i: (0, 0)), hist_spec],
        out_specs=[pl.BlockSpec((rows, WIDTH), lambda b, i: (b * per + i, 0)), hist_spec],
        out_shape=[jax.ShapeDtypeStruct((tokens, WIDTH), out_dtype),
                   jax.ShapeDtypeStruct((n_seq, CONV_WIDTH - 1, WIDTH), F32)],
        scratch_shapes=[pltpu.VMEM((HALO + rows, WIDTH), F32)],
        compiler_params=_params("arbitrary", "arbitrary"),
        name="short_conv",
    )(z, z, z, w_conv, hist)


def _merge_kernel(h_ref, wm_ref, bm_ref, o_ref, wb_ref, out_ref, acc_ref):
    n = pl.program_id(2)
    gate = jnp.dot(h_ref[...], wm_ref[...], preferred_element_type=F32) + bm_ref[...]
    proj = jnp.dot(o_ref[...].astype(BF16), wb_ref[...], preferred_element_type=F32)
    term = _sigmoid(gate) * proj

    @pl.when(n == 0)
    def _():
        acc_ref[...] = term

    @pl.when(n > 0)
    def _():
        acc_ref[...] += term

    @pl.when(n == pl.num_programs(2) - 1)
    def _():
        out_ref[...] = acc_ref[...].astype(out_ref.dtype)


def _merge_cast_kernel(h_ref, wm_ref, bm_ref, o_ref, wb_ref, out_ref, wm16_ref, wb16_ref, acc_ref):
    n = pl.program_id(1)
    wm = wm_ref[...].astype(BF16)
    wb = wb_ref[...].astype(BF16)
    wm16_ref[...] = wm
    wb16_ref[...] = wb
    gate = jnp.dot(h_ref[...], wm, preferred_element_type=F32) + bm_ref[...]
    proj = jnp.dot(o_ref[...].astype(BF16), wb, preferred_element_type=F32)
    term = _sigmoid(gate) * proj

    @pl.when(n == 0)
    def _():
        acc_ref[...] = term

    @pl.when(n > 0)
    def _():
        acc_ref[...] += term

    @pl.when(n == pl.num_programs(1) - 1)
    def _():
        out_ref[...] = acc_ref[...].astype(out_ref.dtype)


def gated_merge_cast(h, branches, w_merge, b_merge, w_branch, layer, bn):
    nb, tokens, _ = branches.shape
    depth = w_merge.shape[0]
    return pl.pallas_call(
        _merge_cast_kernel,
        grid=(D_MODEL // bn, nb),
        in_specs=[pl.BlockSpec((tokens, D_MODEL), lambda j, n: (0, 0)),
                  pl.BlockSpec((None, None, D_MODEL, bn), lambda j, n: (layer, n, 0, j)),
                  pl.BlockSpec((None, None, 1, bn), lambda j, n: (layer, n, 0, j)),
                  pl.BlockSpec((None, tokens, WIDTH), lambda j, n: (n, 0, 0)),
                  pl.BlockSpec((None, None, WIDTH, bn), lambda j, n: (layer, n, 0, j))],
        out_specs=[pl.BlockSpec((tokens, bn), lambda j, n: (0, j)),
                   pl.BlockSpec((None, None, D_MODEL, bn), lambda j, n: (0, n, 0, j)),
                   pl.BlockSpec((None, None, WIDTH, bn), lambda j, n: (0, n, 0, j))],
        out_shape=[jax.ShapeDtypeStruct((tokens, D_MODEL), BF16),
                   jax.ShapeDtypeStruct((1, nb, D_MODEL, D_MODEL), BF16),
                   jax.ShapeDtypeStruct((1, nb, WIDTH, D_MODEL), BF16)],
        scratch_shapes=[pltpu.VMEM((tokens, bn), F32)],
        compiler_params=_params("arbitrary", "arbitrary"),
        name="gated_merge_cast",
    )(h, w_merge, b_merge.reshape(depth, nb, 1, D_MODEL), branches, w_branch)


def gated_merge(h, branches, w_merge, b_merge, w_branch, layer, bm, bn):
    nb, tokens, _ = branches.shape
    depth = b_merge.shape[0]
    return pl.pallas_call(
        _merge_kernel,
        grid=(tokens // bm, D_MODEL // bn, nb),
        in_specs=[pl.BlockSpec((bm, D_MODEL), lambda i, j, n: (i, 0)),
                  pl.BlockSpec((None, None, D_MODEL, bn), lambda i, j, n: (0, n, 0, j)),
                  pl.BlockSpec((None, None, 1, bn), lambda i, j, n: (layer, n, 0, j)),
                  pl.BlockSpec((None, bm, WIDTH), lambda i, j, n: (n, i, 0)),
                  pl.BlockSpec((None, None, WIDTH, bn), lambda i, j, n: (0, n, 0, j))],
        out_specs=pl.BlockSpec((bm, bn), lambda i, j, n: (i, j)),
        out_shape=jax.ShapeDtypeStruct((tokens, D_MODEL), BF16),
        scratch_shapes=[pltpu.VMEM((bm, bn), F32)],
        compiler_params=_params("parallel", "arbitrary", "arbitrary"),
        name="gated_merge",
    )(h, w_merge, b_merge.reshape(depth, nb, 1, D_MODEL), branches, w_branch)


FF_BLK = 256
FF_HALF_BLKS = D_FF // FF_BLK


def _ffn_up_kernel(h_ref, wa0_ref, wa1_ref, wb0_ref, wb1_ref, cwa_ref, cwb_ref, ba_ref, bb_ref,
                   ha_ref, hb_ref, act_ref, ta_ref, tb_ref, sa_ref, sb_ref, ca_ref, cb_ref,
                   *, rows, per_seq):
    i = pl.program_id(0)
    j = pl.program_id(1)
    first = (i % per_seq) == 0

    @pl.when(first)
    def _():
        sa_ref[HALO - 2:HALO, :] = ha_ref[...]
        sb_ref[HALO - 2:HALO, :] = hb_ref[...]

    @pl.when(jnp.logical_not(first))
    def _():
        sa_ref[0:HALO, :] = ca_ref[j]
        sb_ref[0:HALO, :] = cb_ref[j]

    h = h_ref[...]
    cur = slice(HALO, HALO + rows)
    sa_ref[cur, :FF_BLK] = jnp.dot(h, wa0_ref[...], preferred_element_type=F32)
    sb_ref[cur, :FF_BLK] = jnp.dot(h, wb0_ref[...], preferred_element_type=F32)
    sa_ref[cur, FF_BLK:] = jnp.dot(h, wa1_ref[...], preferred_element_type=F32)
    sb_ref[cur, FF_BLK:] = jnp.dot(h, wb1_ref[...], preferred_element_type=F32)
    a = _conv3(sa_ref, cwa_ref, rows) + ba_ref[...]
    b = _conv3(sb_ref, cwb_ref, rows) + bb_ref[...]
    act_ref[...] = (a * _sigmoid(a) * b).astype(act_ref.dtype)
    tail_a = sa_ref[rows:rows + HALO, :]
    tail_b = sb_ref[rows:rows + HALO, :]
    ca_ref[j] = tail_a
    cb_ref[j] = tail_b
    ta_ref[...] = tail_a
    tb_ref[...] = tail_b


def ffn_up(h, w_up, layer, cwa, cwb, ba, bb, hist_a, hist_b, n_seq, seq_len, bm):
    tokens = h.shape[0]
    per_seq = seq_len // bm
    bn = 2 * FF_BLK
    nj = D_FF_PAD // bn
    last_blk = 2 * FF_HALF_BLKS - 1

    def wspec(first_blk, sub):
        return pl.BlockSpec((None, D_MODEL, FF_BLK),
                            lambda i, j: (layer, 0, jnp.minimum(first_blk + 2 * j + sub, last_blk)))

    cspec = pl.BlockSpec((CONV_WIDTH, bn), lambda i, j: (0, j))
    bspec = pl.BlockSpec((1, bn), lambda i, j: (0, j))
    hspec = pl.BlockSpec((None, CONV_WIDTH - 1, bn), lambda i, j: (i // per_seq, 0, j))
    tspec = pl.BlockSpec((None, HALO, bn), lambda i, j: (i, 0, j))
    tail_shape = jax.ShapeDtypeStruct((tokens // bm, HALO, D_FF_PAD), F32)
    act, tail_a, tail_b = pl.pallas_call(
        functools.partial(_ffn_up_kernel, rows=bm, per_seq=per_seq),
        grid=(tokens // bm, nj),
        in_specs=[pl.BlockSpec((bm, D_MODEL), lambda i, j: (i, 0)),
                  wspec(0, 0), wspec(0, 1), wspec(FF_HALF_BLKS, 0), wspec(FF_HALF_BLKS, 1),
                  cspec, cspec, bspec, bspec, hspec, hspec],
        out_specs=[pl.BlockSpec((bm, bn), lambda i, j: (i, j)), tspec, tspec],
        out_shape=[jax.ShapeDtypeStruct((tokens, D_FF_PAD), BF16), tail_shape, tail_shape],
        scratch_shapes=[pltpu.VMEM((HALO + bm, bn), F32), pltpu.VMEM((HALO + bm, bn), F32),
                        pltpu.VMEM((nj, HALO, bn), F32), pltpu.VMEM((nj, HALO, bn), F32)],
        compiler_params=_params("arbitrary", "arbitrary"),
        name="ffn_up",
    )(h, w_up, w_up, w_up, w_up, cwa, cwb, ba, bb, hist_a, hist_b)
    return act, tail_a[per_seq - 1::per_seq], tail_b[per_seq - 1::per_seq]


def _ffn_act_kernel(ua_ref, ub_ref, cwa_ref, cwb_ref, ba_ref, bb_ref, ha_ref, hb_ref, act_ref,
                    sa_ref, sb_ref, *, rows):
    sa_ref[HALO - 2:HALO, :] = ha_ref[...]
    sb_ref[HALO - 2:HALO, :] = hb_ref[...]
    sa_ref[HALO:HALO + rows, :] = ua_ref[...]
    sb_ref[HALO:HALO + rows, :] = ub_ref[...]
    a = _conv3(sa_ref, cwa_ref, rows) + ba_ref[...]
    b = _conv3(sb_ref, cwb_ref, rows) + bb_ref[...]
    act_ref[...] = (a * _sigmoid(a) * b).astype(act_ref.dtype)


def ffn_act(up, w_conv, bias, hist, n_seq, seq_len):
    tokens = up.shape[0]
    bn = D_FF // 2
    cspec = [pl.BlockSpec((CONV_WIDTH, bn), lambda b, j, o=o: (0, o + j)) for o in (0, 2)]
    bspec = [pl.BlockSpec((1, bn), lambda b, j, o=o: (0, o + j)) for o in (0, 2)]
    hspec = [pl.BlockSpec((None, CONV_WIDTH - 1, bn), lambda b, j, o=o: (b, 0, o + j)) for o in (0, 2)]
    uspec = [pl.BlockSpec((seq_len, bn), lambda b, j, o=o: (b, o + j)) for o in (0, 2)]
    return pl.pallas_call(
        functools.partial(_ffn_act_kernel, rows=seq_len),
        grid=(n_seq, 2),
        in_specs=uspec + cspec + bspec + hspec,
        out_specs=pl.BlockSpec((seq_len, bn), lambda b, j: (b, j)),
        out_shape=jax.ShapeDtypeStruct((tokens, D_FF), F32),
        scratch_shapes=[pltpu.VMEM((HALO + seq_len, bn), F32), pltpu.VMEM((HALO + seq_len, bn), F32)],
        compiler_params=_params("parallel", "parallel"),
        name="ffn_act",
    )(up, up, w_conv, w_conv, bias, bias, hist, hist)


WZ_K = 512


def _in_proj_cast_kernel(x_ref, a_ref, b_ref, z_ref, wz_ref):
    j = pl.program_id(0)
    kb = pl.program_id(1)

    def emit(w_t):
        w = w_t.T.astype(BF16)
        wz_ref[...] = w
        part = jnp.dot(x_ref[...], w, preferred_element_type=F32)

        @pl.when(kb == 0)
        def _():
            z_ref[...] = part

        @pl.when(kb > 0)
        def _():
            z_ref[...] += part

    @pl.when(j < FORGET_COL // WIDTH)
    def _():
        emit(a_ref[...])

    @pl.when(j >= FORGET_COL // WIDTH)
    def _():
        emit(jnp.concatenate([a_ref[N_HEADS:, :], b_ref[...]], axis=0))


def in_proj_cast(x, w_in_t, layer):
    m, d = x.shape
    sub = WIDTH // N_HEADS
    return pl.pallas_call(
        _in_proj_cast_kernel,
        grid=(Z_COLS // WIDTH, d // WZ_K),
        in_specs=[pl.BlockSpec((m, WZ_K), lambda j, kb: (0, kb)),
                  pl.BlockSpec((None, WIDTH, WZ_K), lambda j, kb: (layer, j, kb)),
                  pl.BlockSpec((None, N_HEADS, WZ_K), lambda j, kb: (layer, sub * (j + 1), kb))],
        out_specs=[pl.BlockSpec((m, WIDTH), lambda j, kb: (0, j)),
                   pl.BlockSpec((None, WZ_K, WIDTH), lambda j, kb: (0, kb, j))],
        out_shape=[jax.ShapeDtypeStruct((m, Z_COLS), F32), jax.ShapeDtypeStruct((1, d, Z_COLS), BF16)],
        compiler_params=_params("parallel", "arbitrary"),
        name="in_proj_cast",
    )(x, w_in_t, w_in_t)


def _pad_cols(a, n):
    return jnp.pad(a, ((0, 0), (0, n - a.shape[1])))


def _split_hist(hist):
    pad = ((0, 0), (0, 0), (0, D_FF_PAD - D_FF))
    return jnp.pad(hist[..., :D_FF], pad), jnp.pad(hist[..., D_FF:], pad)


def _join_hist(tail_a, tail_b):
    return jnp.concatenate([tail_a[..., :D_FF], tail_b[..., :D_FF]], axis=-1)


def mixer_and_ffn(grp, layer, x, h, mod_op, lw, small, tabs, attend, ret_s0, conv_hist, ffn_hist, next_pre,
                  next_mod_op):
    n_seq, seq_len, tokens = grp.n_seq, grp.seq_len, grp.tokens
    bmm = grp.mm_block
    act_dtype = BF16 if seq_len >= 16 else F32
    cast = "f32" in lw
    lw16 = dict(b_merge=lw["b_merge"]) if cast else lw

    if cast:
        z, lw16["wz"] = in_proj_cast(h, lw["w_in_t"], layer)
        kv = z[:, BK_BLK * HEAD_DIM:CQ_BLK * HEAD_DIM]
    else:
        z, kv = in_proj(h, lw["wz"], bmm)
    logf_t, cum_t = forget_logits(grp, h, small["wf_t"], small["b_forget"])

    chunk_rows = min(seq_len, CHUNK)
    n_chunks = seq_len // chunk_rows
    o_a, v_tail = chunk_gating(z, small["a_ln_g"], small["a_ws"][:, :chunk_rows, :chunk_rows],
                               jnp.broadcast_to(small["a_bs"][:, :chunk_rows, None],
                                                (N_HEADS, chunk_rows, HEAD_DIM)),
                               n_seq, n_chunks, chunk_rows, act_dtype)
    o_b = attend(z, cum_t)
    o_c, ret_state = retention(z, ret_s0, tabs, small["ret_gn_g"], n_seq, seq_len, act_dtype)
    conv_rows = min(seq_len, 512)
    o_d, conv_state = short_conv(z, small["w_sc_conv"], conv_hist, n_seq, seq_len, conv_rows, act_dtype)

    branches = jnp.stack([o_a, o_b.astype(act_dtype), o_c, o_d], axis=0)
    if cast:
        merged, lw16["w_merge"], lw16["w_branch"] = gated_merge_cast(
            h, branches, lw["w_merge"], lw["b_merge"], lw["w_branch"], layer, 512)
        y, lw16["w_out"] = matmul_cast(merged, lw["w_out"], layer, 512)
    else:
        merged = gated_merge(h, branches, lw["w_merge"], lw["b_merge"], lw["w_branch"], layer, bmm, 512)
        y = matmul(merged, lw["w_out"], 0, BF16, bmm, 1024)
    x, h2 = residual(grp, x, y, small["g_post_mix"], mod_op, 2, small["g_pre_ffn"], mod_op, 3, 4)

    cw = small["w_ffn_conv"]
    bias = small["b_ffn_conv"].reshape(1, 2 * D_FF)
    if cast:
        up, lw16["w_up"] = matmul_cast(h2, lw["w_up"], layer, 512)
        act = _pad_cols(ffn_act(up, cw, bias, ffn_hist, n_seq, seq_len), D_FF_PAD)
        ffn_state = up.reshape(n_seq, seq_len, 2 * D_FF)[:, seq_len - 2:]
        y2, lw16["w_down"] = matmul_ksplit_cast(act, lw["w_down"], layer, 512, 2816)
    else:
        hist_a, hist_b = _split_hist(ffn_hist)
        act, tail_a, tail_b = ffn_up(h2, lw["w_up"], 0,
                                     _pad_cols(cw[:, :D_FF], D_FF_PAD), _pad_cols(cw[:, D_FF:], D_FF_PAD),
                                     _pad_cols(bias[:, :D_FF], D_FF_PAD), _pad_cols(bias[:, D_FF:], D_FF_PAD),
                                     hist_a, hist_b, n_seq, seq_len, bmm)
        ffn_state = _join_hist(tail_a[:, HALO - 2:], tail_b[:, HALO - 2:])
        y2 = matmul_ksplit(act, lw["w_down"], 0, bmm, 1024, 2816)
    if next_pre is None:
        x = residual(grp, x, y2, small["g_post_ffn"], mod_op, 5)
        h_next = None
    else:
        x, h_next = residual(grp, x, y2, small["g_post_ffn"], mod_op, 5, next_pre, next_mod_op, 0, 1)

    k_b = kv[:, :WIDTH].reshape(n_seq, seq_len, N_HEADS, HEAD_DIM)
    v_b = kv[:, WIDTH:].reshape(n_seq, seq_len, N_HEADS, HEAD_DIM)
    logf = logf_t.T.reshape(n_seq, seq_len, N_HEADS)
    return x, h_next, (k_b, v_b, logf, ret_state, conv_state, v_tail, ffn_state), lw16


def kernel(x_prompt, x_sample, cache_k, cache_v, cache_logf, state_ret, state_conv, state_ffn_conv,
           page_table, c_prompt, c_sample, w_ada, b_ada, g_pre_mix, g_post_mix, g_pre_ffn, g_post_ffn,
           w_in, b_forget, a_ln_g, a_ws, a_bs, ret_gn_g, w_sc_conv, w_branch, w_merge, b_merge, w_out,
           w_up, w_ffn_conv, b_ffn_conv, w_down):
    depth = w_in.shape[0]
    bp, lp, _ = x_prompt.shape
    bs, ls, _ = x_sample.shape
    grp_p = Group(bp, lp, row_block=256, mm_block=1024)
    grp_s = Group(bs, ls, row_block=bs * ls, mm_block=bs * ls)

    c_all = jnp.pad(jnp.concatenate([c_prompt, c_sample], axis=0), ((0, 16 - bp - bs), (0, 0)))
    mod = ada_modulation(c_all, w_ada, b_ada)

    tabs_p = retention_tables(jnp.arange(lp), math.gcd(lp, CHUNK))
    tabs_s = retention_tables(PAST_LEN + jnp.arange(ls), math.gcd(ls, CHUNK))

    n_pool = cache_k.shape[1]
    cache_k2 = cache_k.reshape(depth, n_pool, PAGE_SIZE * N_HEADS, HEAD_DIM)
    cache_v2 = cache_v.reshape(depth, n_pool, PAGE_SIZE * N_HEADS, HEAD_DIM)
    cache_lp_t = cache_logf.transpose(0, 1, 3, 2)

    xp = x_prompt.reshape(grp_p.tokens, D_MODEL)
    xs = x_sample.reshape(grp_s.tokens, D_MODEL)
    mods_p = [grp_p.mod_operand(mod[l, :bp]) for l in range(depth)]
    mods_s = [grp_s.mod_operand(mod[l, bp:bp + bs]) for l in range(depth)]
    hp = prenorm(grp_p, xp, g_pre_mix[0], mods_p[0], 0, 1)
    hs = prenorm(grp_s, xs, g_pre_mix[0], mods_s[0], 0, 1)

    zeros_ret = jnp.zeros((bp, N_HEADS, HEAD_DIM, HEAD_DIM), F32)
    zeros_conv = jnp.zeros((bp, CONV_WIDTH - 1, WIDTH), F32)
    zeros_ffn = jnp.zeros((bp, CONV_WIDTH - 1, 2 * D_FF), F32)
    eye = jnp.eye(N_HEADS, dtype=F32)

    w_in_t = jnp.swapaxes(w_in, 1, 2)
    lw32 = dict(f32=True, w_in_t=w_in_t, w_merge=w_merge, b_merge=b_merge, w_branch=w_branch,
                w_out=w_out, w_up=w_up, w_down=w_down)

    st_p, st_s = [], []
    for l in range(depth):
        wf_t = w_in_t[l, FORGET_COL:FORGET_COL + N_HEADS]
        small = dict(wf_t=jnp.pad(wf_t, ((0, 16 - N_HEADS), (0, 0))),
                     b_forget=b_forget[l], a_ln_g=a_ln_g[l], a_ws=a_ws[l], a_bs=a_bs[l],
                     ret_gn_g=ret_gn_g[l], w_sc_conv=w_sc_conv[l],
                     w_ffn_conv=w_ffn_conv[l], b_ffn_conv=b_ffn_conv[l],
                     g_post_mix=g_post_mix[l], g_pre_ffn=g_pre_ffn[l], g_post_ffn=g_post_ffn[l])
        next_pre = g_pre_mix[l + 1] if l + 1 < depth else None

        def attend_p(z, cum_t):
            return fox_prompt(z, cum_t, bp, lp)

        def attend_s(z, cum_t, l=l):
            zs = z.reshape(bs, ls, Z_COLS)
            q = zs[..., BQ_BLK * HEAD_DIM:BK_BLK * HEAD_DIM].reshape(bs, ls, N_HEADS, HEAD_DIM)
            q = q.transpose(0, 2, 1, 3) * HEAD_DIM ** -0.5
            qbd = (q[:, :, :, None, :] * eye[None, :, None, :, None]).reshape(bs, N_HEADS * ls, WIDTH)
            k_new = zs[..., BK_BLK * HEAD_DIM:BV_BLK * HEAD_DIM]
            v_new = zs[..., BV_BLK * HEAD_DIM:CQ_BLK * HEAD_DIM]
            cum = cum_t.reshape(N_HEADS, bs, ls).transpose(1, 0, 2)
            fcol = cum.reshape(bs, N_HEADS * ls, 1)
            frow = jnp.broadcast_to(cum[:, :, None, :], (bs, N_HEADS, ls, ls)).reshape(bs, N_HEADS * ls, ls)
            o = fox_sample(page_table, qbd.astype(BF16), k_new, v_new, fcol, frow,
                           cache_k2, cache_v2, cache_lp_t, l)
            return o.reshape(bs * ls, WIDTH)

        xs, hs, ss, lw16 = mixer_and_ffn(grp_s, l, xs, hs, mods_s[l], lw32, small, tabs_s, attend_s,
                                         state_ret[l], state_conv[l], state_ffn_conv[l], next_pre,
                                         mods_s[l + 1] if l + 1 < depth else None)
        xp, hp, sp, _ = mixer_and_ffn(grp_p, l, xp, hp, mods_p[l], lw16, small, tabs_p, attend_p,
                                      zeros_ret, zeros_conv, zeros_ffn, next_pre,
                                      mods_p[l + 1] if l + 1 < depth else None)
        st_p.append(sp)
        st_s.append(ss)

    def stack(states, idx):
        return jnp.stack([s[idx] for s in states], axis=0)

    return (xp.reshape(bp, lp, D_MODEL), xs.reshape(bs, ls, D_MODEL),
            stack(st_p, 0), stack(st_p, 1), stack(st_p, 2),
            stack(st_s, 0), stack(st_s, 1), stack(st_s, 2),
            stack(st_p, 3), stack(st_s, 3), stack(st_p, 4), stack(st_s, 4),
            stack(st_p, 5), stack(st_s, 5), stack(st_p, 6), stack(st_s, 6))
```

```python
import functools
import math

import jax
import jax.numpy as jnp
from jax import lax
from jax.experimental import pallas as pl
from jax.experimental.pallas import tpu as pltpu

D_MODEL = 4096
HEAD_DIM = 128
N_HEADS = 8
WIDTH = N_HEADS * HEAD_DIM
CHUNK = 128
PAST_LEN = 16384
PAGE_SIZE = 128
ROPE_BASE = 10000.0
CONV_WIDTH = 3
D_FF = 11008
D_FF_PAD = 11264
N_MOD = 6
EPS = 1e-6
NEG_INF = -1e30
Z_COLS = 12 * WIDTH
FORGET_COL = 5 * WIDTH

BQ_BLK, BK_BLK, BV_BLK = 16, 24, 32
CQ_BLK, CK_BLK, CV_BLK, CG_BLK = 40, 48, 56, 64

VMEM_LIMIT_BYTES = 56 * 1024 * 1024

BF16 = jnp.bfloat16
F32 = jnp.float32


def _params(*semantics):
    return pltpu.CompilerParams(dimension_semantics=semantics, vmem_limit_bytes=VMEM_LIMIT_BYTES)


def _ada_kernel(c_ref, w_ref, b_ref, o_ref):
    c = c_ref[...]
    s = (c * jax.nn.sigmoid(c)).astype(BF16)
    o_ref[...] = jnp.dot(s, w_ref[...].astype(BF16), preferred_element_type=F32) + b_ref[...]


def ada_modulation(c_all, w_ada, b_ada):
    depth, d, n = w_ada.shape
    rows = c_all.shape[0]
    bn = 512
    return pl.pallas_call(
        _ada_kernel,
        grid=(depth, n // bn),
        in_specs=[
            pl.BlockSpec((rows, d), lambda l, j: (0, 0)),
            pl.BlockSpec((None, d, bn), lambda l, j: (l, 0, j)),
            pl.BlockSpec((None, 1, bn), lambda l, j: (l, 0, j)),
        ],
        out_specs=pl.BlockSpec((None, rows, bn), lambda l, j: (l, 0, j)),
        out_shape=jax.ShapeDtypeStruct((depth, rows, n), F32),
        compiler_params=_params("arbitrary", "arbitrary"),
        name="ada_modulation",
    )(c_all, w_ada, b_ada.reshape(depth, 1, n))


class Group:
    def __init__(self, n_seq, seq_len, row_block, mm_block):
        self.n_seq = n_seq
        self.seq_len = seq_len
        self.tokens = n_seq * seq_len
        self.row_block = row_block
        self.mm_block = mm_block
        self.per_seq_rows = row_block <= seq_len

    def mod_operand(self, mod):
        if self.per_seq_rows:
            return mod.reshape(self.n_seq, N_MOD, 1, D_MODEL)
        tok = jnp.repeat(mod.reshape(self.n_seq, N_MOD, D_MODEL), self.seq_len, axis=0)
        return tok.transpose(1, 0, 2)

    def mod_spec(self, idx):
        if self.per_seq_rows:
            per = self.seq_len // self.row_block
            return pl.BlockSpec((None, None, 1, D_MODEL), lambda i: (i // per, idx, 0, 0))
        return pl.BlockSpec((None, self.row_block, D_MODEL), lambda i: (idx, i, 0))


def _rms(x, g):
    return x * lax.rsqrt(jnp.mean(x * x, axis=-1, keepdims=True) + EPS) * g


def _prenorm_kernel(x_ref, g_ref, sc_ref, sh_ref, h_ref):
    h_ref[...] = (_rms(x_ref[...], g_ref[...]) * (1.0 + sc_ref[...]) + sh_ref[...]).astype(h_ref.dtype)


def prenorm(grp, x, g, mod_op, sh_idx, sc_idx):
    br = grp.row_block
    row = pl.BlockSpec((br, D_MODEL), lambda i: (i, 0))
    vec = pl.BlockSpec((1, D_MODEL), lambda i: (0, 0))
    return pl.pallas_call(
        _prenorm_kernel,
        grid=(grp.tokens // br,),
        in_specs=[row, vec, grp.mod_spec(sc_idx), grp.mod_spec(sh_idx)],
        out_specs=row,
        out_shape=jax.ShapeDtypeStruct((grp.tokens, D_MODEL), BF16),
        compiler_params=_params("parallel"),
        name="prenorm",
    )(x, g.reshape(1, D_MODEL), mod_op, mod_op)


def _residual_kernel(x_ref, y_ref, gpost_ref, gate_ref, xo_ref):
    xo_ref[...] = x_ref[...] + gate_ref[...] * _rms(y_ref[...], gpost_ref[...])


def _residual_prenorm_kernel(x_ref, y_ref, gpost_ref, gate_ref, gpre_ref, sc_ref, sh_ref, xo_ref, h_ref):
    xn = x_ref[...] + gate_ref[...] * _rms(y_ref[...], gpost_ref[...])
    xo_ref[...] = xn
    h_ref[...] = (_rms(xn, gpre_ref[...]) * (1.0 + sc_ref[...]) + sh_ref[...]).astype(h_ref.dtype)


def residual(grp, x, y, g_post, mod_op, gate_idx, g_pre=None, pre_mod_op=None, sh_idx=None, sc_idx=None):
    br = grp.row_block
    row = pl.BlockSpec((br, D_MODEL), lambda i: (i, 0))
    vec = pl.BlockSpec((1, D_MODEL), lambda i: (0, 0))
    x_shape = jax.ShapeDtypeStruct((grp.tokens, D_MODEL), F32)
    if g_pre is None:
        return pl.pallas_call(
            _residual_kernel,
            grid=(grp.tokens // br,),
            in_specs=[row, row, vec, grp.mod_spec(gate_idx)],
            out_specs=row,
            out_shape=x_shape,
            compiler_params=_params("parallel"),
            name="residual",
        )(x, y, g_post.reshape(1, D_MODEL), mod_op)
    return pl.pallas_call(
        _residual_prenorm_kernel,
        grid=(grp.tokens // br,),
        in_specs=[row, row, vec, grp.mod_spec(gate_idx), vec, grp.mod_spec(sc_idx), grp.mod_spec(sh_idx)],
        out_specs=[row, row],
        out_shape=[x_shape, jax.ShapeDtypeStruct((grp.tokens, D_MODEL), BF16)],
        compiler_params=_params("parallel"),
        name="residual_prenorm",
    )(x, y, g_post.reshape(1, D_MODEL), mod_op, g_pre.reshape(1, D_MODEL), pre_mod_op, pre_mod_op)


def _mm_kernel(x_ref, w_ref, o_ref):
    o_ref[...] = jnp.dot(x_ref[...].astype(BF16), w_ref[...],
                         preferred_element_type=F32).astype(o_ref.dtype)


def matmul(x, w, layer, out_dtype, bm, bn):
    m, k = x.shape
    n = w.shape[2]
    return pl.pallas_call(
        _mm_kernel,
        grid=(m // bm, n // bn),
        in_specs=[pl.BlockSpec((bm, k), lambda i, j: (i, 0)),
                  pl.BlockSpec((None, k, bn), lambda i, j: (layer, 0, j))],
        out_specs=pl.BlockSpec((bm, bn), lambda i, j: (i, j)),
        out_shape=jax.ShapeDtypeStruct((m, n), out_dtype),
        compiler_params=_params("parallel", "arbitrary"),
        name="matmul",
    )(x, w)


def _mm_ksplit_kernel(x_ref, w_ref, o_ref):
    @pl.when(pl.program_id(2) == 0)
    def _():
        o_ref[...] = jnp.zeros_like(o_ref)

    o_ref[...] += jnp.dot(x_ref[...].astype(BF16), w_ref[...], preferred_element_type=F32)


def matmul_ksplit(x, w, layer, bm, bn, bk):
    m, k = x.shape
    n = w.shape[2]
    return pl.pallas_call(
        _mm_ksplit_kernel,
        grid=(m // bm, n // bn, k // bk),
        in_specs=[pl.BlockSpec((bm, bk), lambda i, j, kk: (i, kk)),
                  pl.BlockSpec((None, bk, bn), lambda i, j, kk: (layer, kk, j))],
        out_specs=pl.BlockSpec((bm, bn), lambda i, j, kk: (i, j)),
        out_shape=jax.ShapeDtypeStruct((m, n), F32),
        compiler_params=_params("parallel", "parallel", "arbitrary"),
        name="matmul_ksplit",
    )(x, w)


def _mm_cast_kernel(x_ref, w_ref, o_ref, w16_ref):
    w = w_ref[...].astype(BF16)
    w16_ref[...] = w
    o_ref[...] = jnp.dot(x_ref[...].astype(BF16), w, preferred_element_type=F32)


def matmul_cast(x, w, layer, bn):
    m, k = x.shape
    n = w.shape[2]
    return pl.pallas_call(
        _mm_cast_kernel,
        grid=(n // bn,),
        in_specs=[pl.BlockSpec((m, k), lambda j: (0, 0)),
                  pl.BlockSpec((None, k, bn), lambda j: (layer, 0, j))],
        out_specs=[pl.BlockSpec((m, bn), lambda j: (0, j)),
                   pl.BlockSpec((None, k, bn), lambda j: (0, 0, j))],
        out_shape=[jax.ShapeDtypeStruct((m, n), F32), jax.ShapeDtypeStruct((1, k, n), BF16)],
        compiler_params=_params("parallel"),
        name="matmul_cast",
    )(x, w)


def _mm_ksplit_cast_kernel(x_ref, w_ref, o_ref, w16_ref, *, bk, k_valid):
    kk = pl.program_id(1)
    row = kk * bk + lax.broadcasted_iota(jnp.int32, w_ref.shape, 0)
    w = jnp.where(row < k_valid, w_ref[...], 0.0).astype(BF16)
    w16_ref[...] = w

    @pl.when(kk == 0)
    def _():
        o_ref[...] = jnp.zeros_like(o_ref)

    o_ref[...] += jnp.dot(x_ref[...].astype(BF16), w, preferred_element_type=F32)


def matmul_ksplit_cast(x, w, layer, bn, bk):
    m, k_pad = x.shape
    k_valid, n = w.shape[1:]
    return pl.pallas_call(
        functools.partial(_mm_ksplit_cast_kernel, bk=bk, k_valid=k_valid),
        grid=(n // bn, k_pad // bk),
        in_specs=[pl.BlockSpec((m, bk), lambda j, kk: (0, kk)),
                  pl.BlockSpec((None, bk, bn), lambda j, kk: (layer, kk, j))],
        out_specs=[pl.BlockSpec((m, bn), lambda j, kk: (0, j)),
                   pl.BlockSpec((None, bk, bn), lambda j, kk: (0, kk, j))],
        out_shape=[jax.ShapeDtypeStruct((m, n), F32), jax.ShapeDtypeStruct((1, k_pad, n), BF16)],
        compiler_params=_params("parallel", "arbitrary"),
        name="matmul_ksplit_cast",
    )(x, w)


def _logf_kernel(h_ref, wf_ref, bf_ref, logf_ref, cum_ref, carry_ref, *, tile, seq_len, cw):
    i = pl.program_id(0)
    logits = lax.dot_general(wf_ref[...].astype(BF16), h_ref[...], (((1,), (1,)), ((), ())),
                             preferred_element_type=F32)[:N_HEADS]
    x = logits + bf_ref[...]
    logf = jnp.minimum(x, 0.0) - jnp.log1p(jnp.exp(-jnp.abs(x)))
    logf_ref[...] = logf

    seg = min(seq_len, cw)
    src = lax.broadcasted_iota(jnp.int32, (cw, cw), 0)
    dst = lax.broadcasted_iota(jnp.int32, (cw, cw), 1)
    tri = ((src <= dst) & (src // seg == dst // seg)).astype(F32)

    if seq_len > tile:
        @pl.when((i * tile) % seq_len == 0)
        def _():
            carry_ref[...] = jnp.zeros_like(carry_ref)
        carry = carry_ref[...]
    else:
        carry = jnp.zeros((N_HEADS, 1), F32)

    for c in range(tile // cw):
        part = jnp.dot(logf[:, c * cw:(c + 1) * cw], tri, precision=lax.Precision.HIGHEST,
                       preferred_element_type=F32)
        cum = part + carry
        cum_ref[:, c * cw:(c + 1) * cw] = cum
        if seq_len > cw:
            carry = cum[:, cw - 1:cw]
    if seq_len > tile:
        carry_ref[...] = carry


def forget_logits(grp, h, wf_t, b_forget):
    tile = min(512, grp.tokens)
    cw = min(128, tile)
    kern = functools.partial(_logf_kernel, tile=tile, seq_len=grp.seq_len, cw=cw)
    out = jax.ShapeDtypeStruct((N_HEADS, grp.tokens), F32)
    return pl.pallas_call(
        kern,
        grid=(grp.tokens // tile,),
        in_specs=[pl.BlockSpec((tile, D_MODEL), lambda i: (i, 0)),
                  pl.BlockSpec((16, D_MODEL), lambda i: (0, 0)),
                  pl.BlockSpec((N_HEADS, 1), lambda i: (0, 0))],
        out_specs=[pl.BlockSpec((N_HEADS, tile), lambda i: (0, i))] * 2,
        out_shape=[out, out],
        scratch_shapes=[pltpu.VMEM((N_HEADS, 1), F32)],
        compiler_params=_params("arbitrary"),
        name="forget_logits",
    )(h, wf_t, b_forget.reshape(N_HEADS, 1))


def _gating_kernel(z_ref, lng_ref, ws_ref, bs_ref, o_ref, vt_ref, *, rows, mm_dtype):
    z = jax.nn.gelu(z_ref[...])
    u = z[:, :WIDTH]
    v = z[:, WIDTH:]
    vc = v - jnp.mean(v, axis=-1, keepdims=True)
    vn = vc * lax.rsqrt(jnp.mean(vc * vc, axis=-1, keepdims=True) + EPS) * lng_ref[...]
    vt_ref[...] = vn
    t_idx = lax.broadcasted_iota(jnp.int32, (rows, rows), 0)
    s_idx = lax.broadcasted_iota(jnp.int32, (rows, rows), 1)
    causal = s_idx <= t_idx
    for g in range(N_HEADS):
        cols = slice(g * HEAD_DIM, (g + 1) * HEAD_DIM)
        ws = jnp.where(causal, ws_ref[g], 0.0).astype(mm_dtype)
        mixed = jnp.dot(ws, vn[:, cols].astype(mm_dtype), preferred_element_type=F32) + bs_ref[g]
        o_ref[:, cols] = (u[:, cols] * mixed).astype(o_ref.dtype)


def chunk_gating(z, ln_g, ws, bs_b, n_seq, n_chunks, rows, out_dtype):
    tokens = z.shape[0]
    kern = functools.partial(_gating_kernel, rows=rows, mm_dtype=BF16 if rows >= 16 else F32)
    return pl.pallas_call(
        kern,
        grid=(n_seq, n_chunks),
        in_specs=[pl.BlockSpec((rows, 2 * WIDTH), lambda b, n: (b * n_chunks + n, 0)),
                  pl.BlockSpec((1, WIDTH), lambda b, n: (0, 0)),
                  pl.BlockSpec((N_HEADS, rows, rows), lambda b, n: (0, 0, 0)),
                  pl.BlockSpec((N_HEADS, rows, HEAD_DIM), lambda b, n: (0, 0, 0))],
        out_specs=[pl.BlockSpec((rows, WIDTH), lambda b, n: (b * n_chunks + n, 0)),
                   pl.BlockSpec((None, rows, WIDTH), lambda b, n: (b, 0, 0))],
        out_shape=[jax.ShapeDtypeStruct((tokens, WIDTH), out_dtype),
                   jax.ShapeDtypeStruct((n_seq, rows, WIDTH), F32)],
        compiler_params=_params("arbitrary", "arbitrary"),
        name="chunk_gating",
    )(z, ln_g.reshape(1, WIDTH), ws, bs_b)


def _fox_prompt_kernel(q_ref, k_ref, v_ref, fq_ref, fk_ref, o_ref, *, blk):
    qi = pl.program_id(2)
    q = (q_ref[...] * HEAD_DIM ** -0.5).astype(BF16)
    fq = fq_ref[...]
    q_pos = qi * blk + lax.broadcasted_iota(jnp.int32, (blk, blk), 0)
    k_off = lax.broadcasted_iota(jnp.int32, (blk, blk), 1)

    def step(kb, carry, masked):
        m, l, acc = carry
        ks = pl.multiple_of(kb * blk, blk)
        k = k_ref[pl.ds(ks, blk), :].astype(BF16)
        v = v_ref[pl.ds(ks, blk), :].astype(BF16)
        s = lax.dot_general(q, k, (((1,), (1,)), ((), ())), preferred_element_type=F32)
        s = s + fq - fk_ref[kb]
        if masked:
            s = jnp.where(ks + k_off <= q_pos, s, NEG_INF)
        m_new = jnp.maximum(m, jnp.max(s, axis=-1, keepdims=True))
        alpha = jnp.exp(m - m_new)
        p = jnp.exp(s - m_new)
        l = alpha * l + jnp.sum(p, axis=-1, keepdims=True)
        acc = alpha * acc + jnp.dot(p.astype(BF16), v, preferred_element_type=F32)
        return m_new, l, acc

    init = (jnp.full((blk, 1), NEG_INF, F32), jnp.zeros((blk, 1), F32), jnp.zeros((blk, HEAD_DIM), F32))
    carry = lax.fori_loop(0, qi, functools.partial(step, masked=False), init)
    _, l, acc = step(qi, carry, masked=True)
    o_ref[...] = (acc / l).astype(o_ref.dtype)


def fox_prompt(z, cum_t, n_seq, seq_len):
    blk = 512
    nq = seq_len // blk
    tokens = z.shape[0]
    fq = cum_t.reshape(N_HEADS, tokens, 1)
    fk = cum_t.reshape(N_HEADS, tokens // blk, 1, blk)
    return pl.pallas_call(
        functools.partial(_fox_prompt_kernel, blk=blk),
        grid=(n_seq, N_HEADS, nq),
        in_specs=[pl.BlockSpec((blk, HEAD_DIM), lambda b, h, i: (b * nq + i, BQ_BLK + h)),
                  pl.BlockSpec((seq_len, HEAD_DIM), lambda b, h, i: (b, BK_BLK + h)),
                  pl.BlockSpec((seq_len, HEAD_DIM), lambda b, h, i: (b, BV_BLK + h)),
                  pl.BlockSpec((None, blk, 1), lambda b, h, i: (h, b * nq + i, 0)),
                  pl.BlockSpec((None, nq, 1, blk), lambda b, h, i: (h, b, 0, 0))],
        out_specs=pl.BlockSpec((blk, HEAD_DIM), lambda b, h, i: (b * nq + i, h)),
        out_shape=jax.ShapeDtypeStruct((tokens, WIDTH), BF16),
        compiler_params=_params("parallel", "parallel", "arbitrary"),
        name="fox_prompt",
    )(z, z, z, fq, fk)


def _fox_sample_kernel(pt_ref, qbd_ref, kn_ref, vn_ref, fcol_ref, frow_ref, *rest, pages, n_new):
    page_refs = rest[:3 * pages]
    o_ref, kbuf_ref, vbuf_ref, m_ref, l_ref, acc_ref, carry_ref = rest[3 * pages:]
    del pt_ref
    j = pl.program_id(1)
    rows = N_HEADS * n_new
    qbd = qbd_ref[...]
    fcol = fcol_ref[...]

    def update(s, v):
        m_old = m_ref[...]
        m_new = jnp.maximum(m_old, jnp.max(s, axis=-1, keepdims=True))
        alpha = jnp.exp(m_old - m_new)
        p = jnp.exp(s - m_new)
        l_ref[...] = alpha * l_ref[...] + jnp.sum(p, axis=-1, keepdims=True)
        acc_ref[...] = alpha * acc_ref[...] + jnp.dot(p.astype(BF16), v, preferred_element_type=F32)
        m_ref[...] = m_new

    @pl.when(j == 0)
    def _():
        m_ref[...] = jnp.full_like(m_ref, NEG_INF)
        l_ref[...] = jnp.zeros_like(l_ref)
        acc_ref[...] = jnp.zeros_like(acc_ref)
        carry_ref[...] = jnp.zeros_like(carry_ref)
        kn = kn_ref[...].astype(BF16)
        s = lax.dot_general(qbd, kn, (((1,), (1,)), ((), ())), preferred_element_type=F32)
        s = s + fcol - frow_ref[...]
        t_idx = lax.broadcasted_iota(jnp.int32, (rows, n_new), 0) % n_new
        s_idx = lax.broadcasted_iota(jnp.int32, (rows, n_new), 1)
        s = jnp.where(s_idx <= t_idx, s, NEG_INF)
        update(s, vn_ref[...].astype(BF16))

    lane = lax.broadcasted_iota(jnp.int32, (N_HEADS, PAGE_SIZE), 1)
    later = carry_ref[...]
    bias_parts = []
    for p in range(pages):
        k_ref, v_ref, lp_ref = page_refs[3 * p:3 * p + 3]
        key_rows = slice(p * PAGE_SIZE, (p + 1) * PAGE_SIZE)
        for h in range(N_HEADS):
            head_rows = pl.ds(h, PAGE_SIZE, stride=N_HEADS)
            cols = slice(h * HEAD_DIM, (h + 1) * HEAD_DIM)
            kbuf_ref[key_rows, cols] = k_ref[head_rows, :].astype(BF16)
            vbuf_ref[key_rows, cols] = v_ref[head_rows, :].astype(BF16)
        lp = lp_ref[...]
        suf = lp
        step = 1
        while step < PAGE_SIZE:
            shifted = pltpu.roll(suf, PAGE_SIZE - step, axis=1)
            suf = suf + jnp.where(lane + step < PAGE_SIZE, shifted, 0.0)
            step *= 2
        bias_parts.append(suf - lp + later)
        later = later + suf[:, 0:1]
    carry_ref[...] = later
    n_keys = pages * PAGE_SIZE
    bias = jnp.concatenate(bias_parts, axis=1)
    bias = jnp.broadcast_to(bias[:, None, :], (N_HEADS, n_new, n_keys)).reshape(rows, n_keys)
    s = lax.dot_general(qbd, kbuf_ref[...], (((1,), (1,)), ((), ())), preferred_element_type=F32)
    update(s + fcol + bias, vbuf_ref[...])

    @pl.when(j == pl.num_programs(1) - 1)
    def _():
        inv = 1.0 / l_ref[...]
        for h in range(N_HEADS):
            r = slice(h * n_new, (h + 1) * n_new)
            c = slice(h * HEAD_DIM, (h + 1) * HEAD_DIM)
            o_ref[:, c] = acc_ref[r, c] * inv[r]


def fox_sample(page_table, qbd, k_new, v_new, fcol, frow, cache_k, cache_v, cache_lp_t, layer):
    n_seq, n_pages = page_table.shape
    n_new = k_new.shape[1]
    rows = N_HEADS * n_new
    pages = 16
    steps = n_pages // pages

    def page_map(p):
        def index(b, j, pt):
            return (layer, pt[b, n_pages - 1 - (j * pages + p)], 0, 0)
        return index

    page_specs = []
    page_args = []
    for p in range(pages):
        page_specs += [pl.BlockSpec((None, None, PAGE_SIZE * N_HEADS, HEAD_DIM), page_map(p)),
                       pl.BlockSpec((None, None, PAGE_SIZE * N_HEADS, HEAD_DIM), page_map(p)),
                       pl.BlockSpec((None, None, N_HEADS, PAGE_SIZE), page_map(p))]
        page_args += [cache_k, cache_v, cache_lp_t]

    def per_seq(shape):
        return pl.BlockSpec((None,) + shape, lambda b, j, pt: (b, 0, 0))

    grid_spec = pltpu.PrefetchScalarGridSpec(
        num_scalar_prefetch=1,
        grid=(n_seq, steps),
        in_specs=[per_seq((rows, WIDTH)), per_seq((n_new, WIDTH)), per_seq((n_new, WIDTH)),
                  per_seq((rows, 1)), per_seq((rows, n_new))] + page_specs,
        out_specs=per_seq((n_new, WIDTH)),
        scratch_shapes=[pltpu.VMEM((pages * PAGE_SIZE, WIDTH), BF16), pltpu.VMEM((pages * PAGE_SIZE, WIDTH), BF16),
                        pltpu.VMEM((rows, 1), F32), pltpu.VMEM((rows, 1), F32),
                        pltpu.VMEM((rows, WIDTH), F32), pltpu.VMEM((N_HEADS, 1), F32)],
    )
    return pl.pallas_call(
        functools.partial(_fox_sample_kernel, pages=pages, n_new=n_new),
        grid_spec=grid_spec,
        out_shape=jax.ShapeDtypeStruct((n_seq, n_new, WIDTH), F32),
        compiler_params=_params("arbitrary", "arbitrary"),
        name="fox_sample",
    )(page_table, qbd, k_new, v_new, fcol, frow, *page_args)


def _retention_kernel(q_ref, k_ref, v_ref, g_ref, cos_ref, sin_ref, dmat_ref, qdec_ref, kdec_ref,
                      cdec_ref, gn_ref, s0_ref, o_ref, s_ref, *, chunk, n_chunks, mm_dtype):
    dmat = dmat_ref[...]
    qdec = qdec_ref[...]
    kdec = kdec_ref[...]
    cdec = cdec_ref[...]
    gn = gn_ref[...]

    def rope(x, cos, sin):
        return x * cos + pltpu.roll(x, HEAD_DIM // 2, axis=1) * sin

    def body(i, state):
        r = pl.multiple_of(i * chunk, chunk)
        rows = pl.ds(r, chunk)
        cos = cos_ref[rows, :]
        sin = sin_ref[rows, :]
        q = rope(q_ref[rows, :], cos, sin)
        k = rope(k_ref[rows, :], cos, sin) * HEAD_DIM ** -0.5
        v = v_ref[rows, :].astype(mm_dtype)
        inner = lax.dot_general(q.astype(mm_dtype), k.astype(mm_dtype), (((1,), (1,)), ((), ())),
                                preferred_element_type=F32) * dmat
        o = (jnp.dot(inner.astype(mm_dtype), v, preferred_element_type=F32)
             + jnp.dot((q * qdec).astype(mm_dtype), state.astype(mm_dtype), preferred_element_type=F32))
        kd_t = (k * kdec).T.astype(mm_dtype)
        state = state * cdec + jnp.dot(kd_t, v, preferred_element_type=F32)
        oc = o - jnp.mean(o, axis=-1, keepdims=True)
        y = oc * lax.rsqrt(jnp.mean(oc * oc, axis=-1, keepdims=True) + EPS) * gn
        gate = g_ref[rows, :]
        o_ref[rows, :] = (gate * jax.nn.sigmoid(gate) * y).astype(o_ref.dtype)
        return state

    s_ref[...] = lax.fori_loop(0, n_chunks, body, s0_ref[...], unroll=min(8, n_chunks))


def retention(z, s0, tabs, gn_g, n_seq, seq_len, out_dtype):
    cos, sin, dmat, qdec, kdec, cdec = tabs
    chunk = dmat.shape[-1]
    tokens = z.shape[0]

    def col(blk):
        return pl.BlockSpec((seq_len, HEAD_DIM), lambda b, h: (b, blk + h))

    def per_head(shape):
        return pl.BlockSpec((None,) + shape, lambda b, h: (h, 0, 0))

    table = pl.BlockSpec((seq_len, HEAD_DIM), lambda b, h: (0, 0))
    state = pl.BlockSpec((None, None, HEAD_DIM, HEAD_DIM), lambda b, h: (b, h, 0, 0))
    kern = functools.partial(_retention_kernel, chunk=chunk, n_chunks=seq_len // chunk,
                             mm_dtype=BF16 if chunk >= 16 else F32)
    return pl.pallas_call(
        kern,
        grid=(n_seq, N_HEADS),
        in_specs=[col(CQ_BLK), col(CK_BLK), col(CV_BLK), col(CG_BLK), table, table,
                  per_head((chunk, chunk)), per_head((chunk, HEAD_DIM)), per_head((chunk, HEAD_DIM)),
                  per_head((1, HEAD_DIM)), pl.BlockSpec((1, HEAD_DIM), lambda b, h: (0, h)), state],
        out_specs=[pl.BlockSpec((seq_len, HEAD_DIM), lambda b, h: (b, h)), state],
        out_shape=[jax.ShapeDtypeStruct((tokens, WIDTH), out_dtype),
                   jax.ShapeDtypeStruct((n_seq, N_HEADS, HEAD_DIM, HEAD_DIM), F32)],
        compiler_params=_params("parallel", "parallel"),
        name="retention",
    )(z, z, z, z, cos, sin, dmat, qdec, kdec, cdec, gn_g.reshape(1, WIDTH), s0)


def retention_tables(pos, chunk):
    half = HEAD_DIM // 2
    inv = ROPE_BASE ** (-jnp.arange(half, dtype=F32) / half)
    ang = pos.astype(F32)[:, None] * inv[None, :]
    cos = jnp.concatenate([jnp.cos(ang), jnp.cos(ang)], axis=-1)
    sin = jnp.concatenate([-jnp.sin(ang), jnp.sin(ang)], axis=-1)
    lg = jnp.log1p(-jnp.exp2(-5.0 - jnp.arange(N_HEADS, dtype=F32)))
    idx = jnp.arange(chunk, dtype=F32)
    diff = idx[:, None] - idx[None, :]
    dmat = jnp.where(diff >= 0, jnp.exp(jnp.maximum(diff, 0.0)[None] * lg[:, None, None]), 0.0)
    qdec = jnp.exp((idx[None, :] + 1.0) * lg[:, None])
    kdec = jnp.exp((chunk - 1.0 - idx)[None, :] * lg[:, None])
    cdec = jnp.exp(chunk * lg)
    lanes = (N_HEADS, chunk, HEAD_DIM)
    return (cos, sin, dmat, jnp.broadcast_to(qdec[:, :, None], lanes),
            jnp.broadcast_to(kdec[:, :, None], lanes),
            jnp.broadcast_to(cdec[:, None, None], (N_HEADS, 1, HEAD_DIM)))


HALO = 8


def _conv3(stage_ref, w_ref, rows):
    return (w_ref[2:3, :] * stage_ref[HALO:HALO + rows, :]
            + w_ref[1:2, :] * stage_ref[HALO - 1:HALO - 1 + rows, :]
            + w_ref[0:1, :] * stage_ref[HALO - 2:HALO - 2 + rows, :])


def _short_conv_kernel(bg_ref, cg_ref, hd_ref, w_ref, hist_ref, o_ref, hist_out_ref, stage_ref, *, rows):
    i = pl.program_id(1)

    @pl.when(i == 0)
    def _():
        stage_ref[HALO - 2:HALO, :] = hist_ref[...]

    stage_ref[HALO:HALO + rows, :] = cg_ref[...] * hd_ref[...]
    y = _conv3(stage_ref, w_ref, rows)
    o_ref[...] = (bg_ref[...] * y).astype(o_ref.dtype)
    tail = stage_ref[rows:rows + HALO, :]
    stage_ref[0:HALO, :] = tail
    hist_out_ref[...] = tail[HALO - 2:, :]


def short_conv(z, w_conv, hist, n_seq, seq_len, rows, out_dtype):
    tokens = z.shape[0]
    per = seq_len // rows

    def col(blk):
        return pl.BlockSpec((rows, WIDTH), lambda b, i: (b * per + i, blk))

    hist_spec = pl.BlockSpec((None, CONV_WIDTH - 1, WIDTH), lambda b, i: (b, 0, 0))
    return pl.pallas_call(
        functools.partial(_short_conv_kernel, rows=rows),
        grid=(n_seq, per),
        in_specs=[col(9), col(10), col(11), pl.BlockSpec((CONV_WIDTH, WIDTH), lambda b, i: (0, 0)), hist_spec],
        out_specs=[pl.BlockSpec((rows, WIDTH), lambda b, i: (b * per + i, 0)), hist_spec],
        out_shape=[jax.ShapeDtypeStruct((tokens, WIDTH), out_dtype),
                   jax.ShapeDtypeStruct((n_seq, CONV_WIDTH - 1, WIDTH), F32)],
        scratch_shapes=[pltpu.VMEM((HALO + rows, WIDTH), F32)],
        compiler_params=_params("arbitrary", "arbitrary"),
        name="short_conv",
    )(z, z, z, w_conv, hist)


def _merge_kernel(h_ref, wm_ref, bm_ref, o_ref, wb_ref, out_ref, acc_ref):
    n = pl.program_id(2)
    gate = jnp.dot(h_ref[...], wm_ref[...], preferred_element_type=F32) + bm_ref[...]
    proj = jnp.dot(o_ref[...].astype(BF16), wb_ref[...], preferred_element_type=F32)
    term = jax.nn.sigmoid(gate) * proj

    @pl.when(n == 0)
    def _():
        acc_ref[...] = term

    @pl.when(n > 0)
    def _():
        acc_ref[...] += term

    @pl.when(n == pl.num_programs(2) - 1)
    def _():
        out_ref[...] = acc_ref[...].astype(out_ref.dtype)


def _merge_cast_kernel(h_ref, wm_ref, bm_ref, o_ref, wb_ref, out_ref, wm16_ref, wb16_ref, acc_ref):
    n = pl.program_id(1)
    wm = wm_ref[...].astype(BF16)
    wb = wb_ref[...].astype(BF16)
    wm16_ref[...] = wm
    wb16_ref[...] = wb
    gate = jnp.dot(h_ref[...], wm, preferred_element_type=F32) + bm_ref[...]
    proj = jnp.dot(o_ref[...].astype(BF16), wb, preferred_element_type=F32)
    term = jax.nn.sigmoid(gate) * proj

    @pl.when(n == 0)
    def _():
        acc_ref[...] = term

    @pl.when(n > 0)
    def _():
        acc_ref[...] += term

    @pl.when(n == pl.num_programs(1) - 1)
    def _():
        out_ref[...] = acc_ref[...].astype(out_ref.dtype)


def gated_merge_cast(h, branches, w_merge, b_merge, w_branch, layer, bn):
    nb, tokens, _ = branches.shape
    depth = w_merge.shape[0]
    return pl.pallas_call(
        _merge_cast_kernel,
        grid=(D_MODEL // bn, nb),
        in_specs=[pl.BlockSpec((tokens, D_MODEL), lambda j, n: (0, 0)),
                  pl.BlockSpec((None, None, D_MODEL, bn), lambda j, n: (layer, n, 0, j)),
                  pl.BlockSpec((None, None, 1, bn), lambda j, n: (layer, n, 0, j)),
                  pl.BlockSpec((None, tokens, WIDTH), lambda j, n: (n, 0, 0)),
                  pl.BlockSpec((None, None, WIDTH, bn), lambda j, n: (layer, n, 0, j))],
        out_specs=[pl.BlockSpec((tokens, bn), lambda j, n: (0, j)),
                   pl.BlockSpec((None, None, D_MODEL, bn), lambda j, n: (0, n, 0, j)),
                   pl.BlockSpec((None, None, WIDTH, bn), lambda j, n: (0, n, 0, j))],
        out_shape=[jax.ShapeDtypeStruct((tokens, D_MODEL), BF16),
                   jax.ShapeDtypeStruct((1, nb, D_MODEL, D_MODEL), BF16),
                   jax.ShapeDtypeStruct((1, nb, WIDTH, D_MODEL), BF16)],
        scratch_shapes=[pltpu.VMEM((tokens, bn), F32)],
        compiler_params=_params("arbitrary", "arbitrary"),
        name="gated_merge_cast",
    )(h, w_merge, b_merge.reshape(depth, nb, 1, D_MODEL), branches, w_branch)


def gated_merge(h, branches, w_merge, b_merge, w_branch, layer, bm, bn):
    nb, tokens, _ = branches.shape
    depth = b_merge.shape[0]
    return pl.pallas_call(
        _merge_kernel,
        grid=(tokens // bm, D_MODEL // bn, nb),
        in_specs=[pl.BlockSpec((bm, D_MODEL), lambda i, j, n: (i, 0)),
                  pl.BlockSpec((None, None, D_MODEL, bn), lambda i, j, n: (0, n, 0, j)),
                  pl.BlockSpec((None, None, 1, bn), lambda i, j, n: (layer, n, 0, j)),
                  pl.BlockSpec((None, bm, WIDTH), lambda i, j, n: (n, i, 0)),
                  pl.BlockSpec((None, None, WIDTH, bn), lambda i, j, n: (0, n, 0, j))],
        out_specs=pl.BlockSpec((bm, bn), lambda i, j, n: (i, j)),
        out_shape=jax.ShapeDtypeStruct((tokens, D_MODEL), BF16),
        scratch_shapes=[pltpu.VMEM((bm, bn), F32)],
        compiler_params=_params("parallel", "arbitrary", "arbitrary"),
        name="gated_merge",
    )(h, w_merge, b_merge.reshape(depth, nb, 1, D_MODEL), branches, w_branch)


FF_BLK = 256
FF_HALF_BLKS = D_FF // FF_BLK


def _ffn_up_kernel(h_ref, wa0_ref, wa1_ref, wb0_ref, wb1_ref, cwa_ref, cwb_ref, ba_ref, bb_ref,
                   ha_ref, hb_ref, act_ref, ta_ref, tb_ref, sa_ref, sb_ref, ca_ref, cb_ref,
                   *, rows, per_seq):
    i = pl.program_id(0)
    j = pl.program_id(1)
    first = (i % per_seq) == 0

    @pl.when(first)
    def _():
        sa_ref[HALO - 2:HALO, :] = ha_ref[...]
        sb_ref[HALO - 2:HALO, :] = hb_ref[...]

    @pl.when(jnp.logical_not(first))
    def _():
        sa_ref[0:HALO, :] = ca_ref[j]
        sb_ref[0:HALO, :] = cb_ref[j]

    h = h_ref[...]
    cur = slice(HALO, HALO + rows)
    sa_ref[cur, :FF_BLK] = jnp.dot(h, wa0_ref[...], preferred_element_type=F32)
    sb_ref[cur, :FF_BLK] = jnp.dot(h, wb0_ref[...], preferred_element_type=F32)
    sa_ref[cur, FF_BLK:] = jnp.dot(h, wa1_ref[...], preferred_element_type=F32)
    sb_ref[cur, FF_BLK:] = jnp.dot(h, wb1_ref[...], preferred_element_type=F32)
    a = _conv3(sa_ref, cwa_ref, rows) + ba_ref[...]
    b = _conv3(sb_ref, cwb_ref, rows) + bb_ref[...]
    act_ref[...] = (a * jax.nn.sigmoid(a) * b).astype(act_ref.dtype)
    tail_a = sa_ref[rows:rows + HALO, :]
    tail_b = sb_ref[rows:rows + HALO, :]
    ca_ref[j] = tail_a
    cb_ref[j] = tail_b
    ta_ref[...] = tail_a
    tb_ref[...] = tail_b


def ffn_up(h, w_up, layer, cwa, cwb, ba, bb, hist_a, hist_b, n_seq, seq_len, bm):
    tokens = h.shape[0]
    per_seq = seq_len // bm
    bn = 2 * FF_BLK
    nj = D_FF_PAD // bn
    last_blk = 2 * FF_HALF_BLKS - 1

    def wspec(first_blk, sub):
        return pl.BlockSpec((None, D_MODEL, FF_BLK),
                            lambda i, j: (layer, 0, jnp.minimum(first_blk + 2 * j + sub, last_blk)))

    cspec = pl.BlockSpec((CONV_WIDTH, bn), lambda i, j: (0, j))
    bspec = pl.BlockSpec((1, bn), lambda i, j: (0, j))
    hspec = pl.BlockSpec((None, CONV_WIDTH - 1, bn), lambda i, j: (i // per_seq, 0, j))
    tspec = pl.BlockSpec((None, HALO, bn), lambda i, j: (i, 0, j))
    tail_shape = jax.ShapeDtypeStruct((tokens // bm, HALO, D_FF_PAD), F32)
    act, tail_a, tail_b = pl.pallas_call(
        functools.partial(_ffn_up_kernel, rows=bm, per_seq=per_seq),
        grid=(tokens // bm, nj),
        in_specs=[pl.BlockSpec((bm, D_MODEL), lambda i, j: (i, 0)),
                  wspec(0, 0), wspec(0, 1), wspec(FF_HALF_BLKS, 0), wspec(FF_HALF_BLKS, 1),
                  cspec, cspec, bspec, bspec, hspec, hspec],
        out_specs=[pl.BlockSpec((bm, bn), lambda i, j: (i, j)), tspec, tspec],
        out_shape=[jax.ShapeDtypeStruct((tokens, D_FF_PAD), BF16), tail_shape, tail_shape],
        scratch_shapes=[pltpu.VMEM((HALO + bm, bn), F32), pltpu.VMEM((HALO + bm, bn), F32),
                        pltpu.VMEM((nj, HALO, bn), F32), pltpu.VMEM((nj, HALO, bn), F32)],
        compiler_params=_params("arbitrary", "arbitrary"),
        name="ffn_up",
    )(h, w_up, w_up, w_up, w_up, cwa, cwb, ba, bb, hist_a, hist_b)
    return act, tail_a[per_seq - 1::per_seq], tail_b[per_seq - 1::per_seq]


def _ffn_act_kernel(ua_ref, ub_ref, cwa_ref, cwb_ref, ba_ref, bb_ref, ha_ref, hb_ref, act_ref,
                    sa_ref, sb_ref, *, rows):
    sa_ref[HALO - 2:HALO, :] = ha_ref[...]
    sb_ref[HALO - 2:HALO, :] = hb_ref[...]
    sa_ref[HALO:HALO + rows, :] = ua_ref[...]
    sb_ref[HALO:HALO + rows, :] = ub_ref[...]
    a = _conv3(sa_ref, cwa_ref, rows) + ba_ref[...]
    b = _conv3(sb_ref, cwb_ref, rows) + bb_ref[...]
    act_ref[...] = (a * jax.nn.sigmoid(a) * b).astype(act_ref.dtype)


def ffn_act(up, w_conv, bias, hist, n_seq, seq_len):
    tokens = up.shape[0]
    bn = D_FF // 2
    cspec = [pl.BlockSpec((CONV_WIDTH, bn), lambda b, j, o=o: (0, o + j)) for o in (0, 2)]
    bspec = [pl.BlockSpec((1, bn), lambda b, j, o=o: (0, o + j)) for o in (0, 2)]
    hspec = [pl.BlockSpec((None, CONV_WIDTH - 1, bn), lambda b, j, o=o: (b, 0, o + j)) for o in (0, 2)]
    uspec = [pl.BlockSpec((seq_len, bn), lambda b, j, o=o: (b, o + j)) for o in (0, 2)]
    return pl.pallas_call(
        functools.partial(_ffn_act_kernel, rows=seq_len),
        grid=(n_seq, 2),
        in_specs=uspec + cspec + bspec + hspec,
        out_specs=pl.BlockSpec((seq_len, bn), lambda b, j: (b, j)),
        out_shape=jax.ShapeDtypeStruct((tokens, D_FF), F32),
        scratch_shapes=[pltpu.VMEM((HALO + seq_len, bn), F32), pltpu.VMEM((HALO + seq_len, bn), F32)],
        compiler_params=_params("parallel", "parallel"),
        name="ffn_act",
    )(up, up, w_conv, w_conv, bias, bias, hist, hist)


WZ_K = 512


def _in_proj_cast_kernel(x_ref, a_ref, b_ref, z_ref, wz_ref):
    j = pl.program_id(0)
    kb = pl.program_id(1)

    def emit(w_t):
        w = w_t.T.astype(BF16)
        wz_ref[...] = w
        part = jnp.dot(x_ref[...], w, preferred_element_type=F32)

        @pl.when(kb == 0)
        def _():
            z_ref[...] = part

        @pl.when(kb > 0)
        def _():
            z_ref[...] += part

    @pl.when(j < FORGET_COL // WIDTH)
    def _():
        emit(a_ref[...])

    @pl.when(j >= FORGET_COL // WIDTH)
    def _():
        emit(jnp.concatenate([a_ref[N_HEADS:, :], b_ref[...]], axis=0))


def in_proj_cast(x, w_in_t, layer):
    m, d = x.shape
    sub = WIDTH // N_HEADS
    return pl.pallas_call(
        _in_proj_cast_kernel,
        grid=(Z_COLS // WIDTH, d // WZ_K),
        in_specs=[pl.BlockSpec((m, WZ_K), lambda j, kb: (0, kb)),
                  pl.BlockSpec((None, WIDTH, WZ_K), lambda j, kb: (layer, j, kb)),
                  pl.BlockSpec((None, N_HEADS, WZ_K), lambda j, kb: (layer, sub * (j + 1), kb))],
        out_specs=[pl.BlockSpec((m, WIDTH), lambda j, kb: (0, j)),
                   pl.BlockSpec((None, WZ_K, WIDTH), lambda j, kb: (0, kb, j))],
        out_shape=[jax.ShapeDtypeStruct((m, Z_COLS), F32), jax.ShapeDtypeStruct((1, d, Z_COLS), BF16)],
        compiler_params=_params("parallel", "arbitrary"),
        name="in_proj_cast",
    )(x, w_in_t, w_in_t)


def _pad_cols(a, n):
    return jnp.pad(a, ((0, 0), (0, n - a.shape[1])))


def _split_hist(hist):
    pad = ((0, 0), (0, 0), (0, D_FF_PAD - D_FF))
    return jnp.pad(hist[..., :D_FF], pad), jnp.pad(hist[..., D_FF:], pad)


def _join_hist(tail_a, tail_b):
    return jnp.concatenate([tail_a[..., :D_FF], tail_b[..., :D_FF]], axis=-1)


def mixer_and_ffn(grp, layer, x, h, mod_op, lw, small, tabs, attend, ret_s0, conv_hist, ffn_hist, next_pre,
                  next_mod_op):
    n_seq, seq_len, tokens = grp.n_seq, grp.seq_len, grp.tokens
    bmm = grp.mm_block
    act_dtype = BF16 if seq_len >= 16 else F32
    cast = "f32" in lw
    lw16 = dict(b_merge=lw["b_merge"]) if cast else lw

    if cast:
        z, lw16["wz"] = in_proj_cast(h, lw["w_in_t"], layer)
    else:
        z = matmul(h, lw["wz"], 0, F32, bmm, 1024)
    logf_t, cum_t = forget_logits(grp, h, small["wf_t"], small["b_forget"])

    chunk_rows = min(seq_len, CHUNK)
    n_chunks = seq_len // chunk_rows
    o_a, v_tail = chunk_gating(z, small["a_ln_g"], small["a_ws"][:, :chunk_rows, :chunk_rows],
                               jnp.broadcast_to(small["a_bs"][:, :chunk_rows, None],
                                                (N_HEADS, chunk_rows, HEAD_DIM)),
                               n_seq, n_chunks, chunk_rows, act_dtype)
    o_b = attend(z, cum_t)
    o_c, ret_state = retention(z, ret_s0, tabs, small["ret_gn_g"], n_seq, seq_len, act_dtype)
    conv_rows = min(seq_len, 512)
    o_d, conv_state = short_conv(z, small["w_sc_conv"], conv_hist, n_seq, seq_len, conv_rows, act_dtype)

    branches = jnp.stack([o_a, o_b.astype(act_dtype), o_c, o_d], axis=0)
    if cast:
        merged, lw16["w_merge"], lw16["w_branch"] = gated_merge_cast(
            h, branches, lw["w_merge"], lw["b_merge"], lw["w_branch"], layer, 512)
        y, lw16["w_out"] = matmul_cast(merged, lw["w_out"], layer, 512)
    else:
        merged = gated_merge(h, branches, lw["w_merge"], lw["b_merge"], lw["w_branch"], layer, bmm, 512)
        y = matmul(merged, lw["w_out"], 0, F32, bmm, 1024)
    x, h2 = residual(grp, x, y, small["g_post_mix"], mod_op, 2, small["g_pre_ffn"], mod_op, 3, 4)

    cw = small["w_ffn_conv"]
    bias = small["b_ffn_conv"].reshape(1, 2 * D_FF)
    if cast:
        up, lw16["w_up"] = matmul_cast(h2, lw["w_up"], layer, 512)
        act = _pad_cols(ffn_act(up, cw, bias, ffn_hist, n_seq, seq_len), D_FF_PAD)
        ffn_state = up.reshape(n_seq, seq_len, 2 * D_FF)[:, seq_len - 2:]
        y2, lw16["w_down"] = matmul_ksplit_cast(act, lw["w_down"], layer, 512, 2816)
    else:
        hist_a, hist_b = _split_hist(ffn_hist)
        act, tail_a, tail_b = ffn_up(h2, lw["w_up"], 0,
                                     _pad_cols(cw[:, :D_FF], D_FF_PAD), _pad_cols(cw[:, D_FF:], D_FF_PAD),
                                     _pad_cols(bias[:, :D_FF], D_FF_PAD), _pad_cols(bias[:, D_FF:], D_FF_PAD),
                                     hist_a, hist_b, n_seq, seq_len, bmm)
        ffn_state = _join_hist(tail_a[:, HALO - 2:], tail_b[:, HALO - 2:])
        y2 = matmul_ksplit(act, lw["w_down"], 0, bmm, 1024, 2816)
    if next_pre is None:
        x = residual(grp, x, y2, small["g_post_ffn"], mod_op, 5)
        h_next = None
    else:
        x, h_next = residual(grp, x, y2, small["g_post_ffn"], mod_op, 5, next_pre, next_mod_op, 0, 1)

    k_b = z[:, BK_BLK * HEAD_DIM:BV_BLK * HEAD_DIM].reshape(n_seq, seq_len, N_HEADS, HEAD_DIM)
    v_b = z[:, BV_BLK * HEAD_DIM:CQ_BLK * HEAD_DIM].reshape(n_seq, seq_len, N_HEADS, HEAD_DIM)
    logf = logf_t.T.reshape(n_seq, seq_len, N_HEADS)
    return x, h_next, (k_b, v_b, logf, ret_state, conv_state, v_tail, ffn_state), lw16


def kernel(x_prompt, x_sample, cache_k, cache_v, cache_logf, state_ret, state_conv, state_ffn_conv,
           page_table, c_prompt, c_sample, w_ada, b_ada, g_pre_mix, g_post_mix, g_pre_ffn, g_post_ffn,
           w_in, b_forget, a_ln_g, a_ws, a_bs, ret_gn_g, w_sc_conv, w_branch, w_merge, b_merge, w_out,
           w_up, w_ffn_conv, b_ffn_conv, w_down):
    depth = w_in.shape[0]
    bp, lp, _ = x_prompt.shape
    bs, ls, _ = x_sample.shape
    grp_p = Group(bp, lp, row_block=256, mm_block=1024)
    grp_s = Group(bs, ls, row_block=bs * ls, mm_block=bs * ls)

    c_all = jnp.pad(jnp.concatenate([c_prompt, c_sample], axis=0), ((0, 16 - bp - bs), (0, 0)))
    mod = ada_modulation(c_all, w_ada, b_ada)

    tabs_p = retention_tables(jnp.arange(lp), math.gcd(lp, CHUNK))
    tabs_s = retention_tables(PAST_LEN + jnp.arange(ls), math.gcd(ls, CHUNK))

    n_pool = cache_k.shape[1]
    cache_k2 = cache_k.reshape(depth, n_pool, PAGE_SIZE * N_HEADS, HEAD_DIM)
    cache_v2 = cache_v.reshape(depth, n_pool, PAGE_SIZE * N_HEADS, HEAD_DIM)
    cache_lp_t = cache_logf.transpose(0, 1, 3, 2)

    xp = x_prompt.reshape(grp_p.tokens, D_MODEL)
    xs = x_sample.reshape(grp_s.tokens, D_MODEL)
    mods_p = [grp_p.mod_operand(mod[l, :bp]) for l in range(depth)]
    mods_s = [grp_s.mod_operand(mod[l, bp:bp + bs]) for l in range(depth)]
    hp = prenorm(grp_p, xp, g_pre_mix[0], mods_p[0], 0, 1)
    hs = prenorm(grp_s, xs, g_pre_mix[0], mods_s[0], 0, 1)

    zeros_ret = jnp.zeros((bp, N_HEADS, HEAD_DIM, HEAD_DIM), F32)
    zeros_conv = jnp.zeros((bp, CONV_WIDTH - 1, WIDTH), F32)
    zeros_ffn = jnp.zeros((bp, CONV_WIDTH - 1, 2 * D_FF), F32)
    eye = jnp.eye(N_HEADS, dtype=F32)

    w_in_t = jnp.swapaxes(w_in, 1, 2)
    lw32 = dict(f32=True, w_in_t=w_in_t, w_merge=w_merge, b_merge=b_merge, w_branch=w_branch,
                w_out=w_out, w_up=w_up, w_down=w_down)

    st_p, st_s = [], []
    for l in range(depth):
        wf_t = w_in_t[l, FORGET_COL:FORGET_COL + N_HEADS]
        small = dict(wf_t=jnp.pad(wf_t, ((0, 16 - N_HEADS), (0, 0))),
                     b_forget=b_forget[l], a_ln_g=a_ln_g[l], a_ws=a_ws[l], a_bs=a_bs[l],
                     ret_gn_g=ret_gn_g[l], w_sc_conv=w_sc_conv[l],
                     w_ffn_conv=w_ffn_conv[l], b_ffn_conv=b_ffn_conv[l],
                     g_post_mix=g_post_mix[l], g_pre_ffn=g_pre_ffn[l], g_post_ffn=g_post_ffn[l])
        next_pre = g_pre_mix[l + 1] if l + 1 < depth else None

        def attend_p(z, cum_t):
            return fox_prompt(z, cum_t, bp, lp)

        def attend_s(z, cum_t, l=l):
            zs = z.reshape(bs, ls, Z_COLS)
            q = zs[..., BQ_BLK * HEAD_DIM:BK_BLK * HEAD_DIM].reshape(bs, ls, N_HEADS, HEAD_DIM)
            q = q.transpose(0, 2, 1, 3) * HEAD_DIM ** -0.5
            qbd = (q[:, :, :, None, :] * eye[None, :, None, :, None]).reshape(bs, N_HEADS * ls, WIDTH)
            k_new = zs[..., BK_BLK * HEAD_DIM:BV_BLK * HEAD_DIM]
            v_new = zs[..., BV_BLK * HEAD_DIM:CQ_BLK * HEAD_DIM]
            cum = cum_t.reshape(N_HEADS, bs, ls).transpose(1, 0, 2)
            fcol = cum.reshape(bs, N_HEADS * ls, 1)
            frow = jnp.broadcast_to(cum[:, :, None, :], (bs, N_HEADS, ls, ls)).reshape(bs, N_HEADS * ls, ls)
            o = fox_sample(page_table, qbd.astype(BF16), k_new, v_new, fcol, frow,
                           cache_k2, cache_v2, cache_lp_t, l)
            return o.reshape(bs * ls, WIDTH)

        xs, hs, ss, lw16 = mixer_and_ffn(grp_s, l, xs, hs, mods_s[l], lw32, small, tabs_s, attend_s,
                                         state_ret[l], state_conv[l], state_ffn_conv[l], next_pre,
                                         mods_s[l + 1] if l + 1 < depth else None)
        xp, hp, sp, _ = mixer_and_ffn(grp_p, l, xp, hp, mods_p[l], lw16, small, tabs_p, attend_p,
                                      zeros_ret, zeros_conv, zeros_ffn, next_pre,
                                      mods_p[l + 1] if l + 1 < depth else None)
        st_p.append(sp)
        st_s.append(ss)

    def stack(states, idx):
        return jnp.stack([s[idx] for s in states], axis=0)

    return (xp.reshape(bp, lp, D_MODEL), xs.reshape(bs, ls, D_MODEL),
            stack(st_p, 0), stack(st_p, 1), stack(st_p, 2),
            stack(st_s, 0), stack(st_s, 1), stack(st_s, 2),
            stack(st_p, 3), stack(st_s, 3), stack(st_p, 4), stack(st_s, 4),
            stack(st_p, 5), stack(st_s, 5), stack(st_p, 6), stack(st_s, 6))
```

```python
import functools
import math

import jax
import jax.numpy as jnp
from jax import lax
from jax.experimental import pallas as pl
from jax.experimental.pallas import tpu as pltpu

D_MODEL = 4096
HEAD_DIM = 128
N_HEADS = 8
WIDTH = N_HEADS * HEAD_DIM
CHUNK = 128
PAST_LEN = 16384
PAGE_SIZE = 128
ROPE_BASE = 10000.0
CONV_WIDTH = 3
D_FF = 11008
D_FF_PAD = 11264
N_MOD = 6
EPS = 1e-6
NEG_INF = -1e30
Z_COLS = 12 * WIDTH
FORGET_COL = 5 * WIDTH

BQ_BLK, BK_BLK, BV_BLK = 16, 24, 32
CQ_BLK, CK_BLK, CV_BLK, CG_BLK = 40, 48, 56, 64

VMEM_LIMIT_BYTES = 56 * 1024 * 1024

MM_ROWS = 1024
ROW_BLOCK = 256
MM_COLS = 1024
MERGE_COLS = 512
CAST_COLS = 512
DOWN_K = D_FF_PAD // 4
ADA_COLS = 512
LOGF_TILE = 512
FOX_BLOCK = 512
FOX_PAGES = 16
CONV_ROWS = 512
RET_UNROLL = 16

BF16 = jnp.bfloat16
F32 = jnp.float32


def _params(*semantics):
    return pltpu.CompilerParams(dimension_semantics=semantics, vmem_limit_bytes=VMEM_LIMIT_BYTES)


def _ada_kernel(c_ref, w_ref, b_ref, o_ref):
    c = c_ref[...]
    s = (c * jax.nn.sigmoid(c)).astype(BF16)
    o_ref[...] = jnp.dot(s, w_ref[...].astype(BF16), preferred_element_type=F32) + b_ref[...]


def ada_modulation(c_all, w_ada, b_ada):
    depth, d, n = w_ada.shape
    rows = c_all.shape[0]
    bn = ADA_COLS
    return pl.pallas_call(
        _ada_kernel,
        grid=(depth, n // bn),
        in_specs=[
            pl.BlockSpec((rows, d), lambda l, j: (0, 0)),
            pl.BlockSpec((None, d, bn), lambda l, j: (l, 0, j)),
            pl.BlockSpec((None, 1, bn), lambda l, j: (l, 0, j)),
        ],
        out_specs=pl.BlockSpec((None, rows, bn), lambda l, j: (l, 0, j)),
        out_shape=jax.ShapeDtypeStruct((depth, rows, n), F32),
        compiler_params=_params("arbitrary", "arbitrary"),
        name="ada_modulation",
    )(c_all, w_ada, b_ada.reshape(depth, 1, n))


class Group:
    def __init__(self, n_seq, seq_len, row_block, mm_block):
        self.n_seq = n_seq
        self.seq_len = seq_len
        self.tokens = n_seq * seq_len
        self.row_block = row_block
        self.mm_block = mm_block
        self.per_seq_rows = row_block <= seq_len

    def mod_operand(self, mod):
        if self.per_seq_rows:
            return mod.reshape(self.n_seq, N_MOD, 1, D_MODEL)
        tok = jnp.repeat(mod.reshape(self.n_seq, N_MOD, D_MODEL), self.seq_len, axis=0)
        return tok.transpose(1, 0, 2)

    def mod_spec(self, idx):
        if self.per_seq_rows:
            per = self.seq_len // self.row_block
            return pl.BlockSpec((None, None, 1, D_MODEL), lambda i: (i // per, idx, 0, 0))
        return pl.BlockSpec((None, self.row_block, D_MODEL), lambda i: (idx, i, 0))


def _rms(x, g):
    return x * lax.rsqrt(jnp.mean(x * x, axis=-1, keepdims=True) + EPS) * g


def _prenorm_kernel(x_ref, g_ref, sc_ref, sh_ref, h_ref):
    h_ref[...] = (_rms(x_ref[...], g_ref[...]) * (1.0 + sc_ref[...]) + sh_ref[...]).astype(h_ref.dtype)


def prenorm(grp, x, g, mod_op, sh_idx, sc_idx):
    br = grp.row_block
    row = pl.BlockSpec((br, D_MODEL), lambda i: (i, 0))
    vec = pl.BlockSpec((1, D_MODEL), lambda i: (0, 0))
    return pl.pallas_call(
        _prenorm_kernel,
        grid=(grp.tokens // br,),
        in_specs=[row, vec, grp.mod_spec(sc_idx), grp.mod_spec(sh_idx)],
        out_specs=row,
        out_shape=jax.ShapeDtypeStruct((grp.tokens, D_MODEL), BF16),
        compiler_params=_params("parallel"),
        name="prenorm",
    )(x, g.reshape(1, D_MODEL), mod_op, mod_op)


def _residual_kernel(x_ref, y_ref, gpost_ref, gate_ref, xo_ref):
    xo_ref[...] = x_ref[...] + gate_ref[...] * _rms(y_ref[...], gpost_ref[...])


def _residual_prenorm_kernel(x_ref, y_ref, gpost_ref, gate_ref, gpre_ref, sc_ref, sh_ref, xo_ref, h_ref):
    xn = x_ref[...] + gate_ref[...] * _rms(y_ref[...], gpost_ref[...])
    xo_ref[...] = xn
    h_ref[...] = (_rms(xn, gpre_ref[...]) * (1.0 + sc_ref[...]) + sh_ref[...]).astype(h_ref.dtype)


def residual(grp, x, y, g_post, mod_op, gate_idx, g_pre=None, pre_mod_op=None, sh_idx=None, sc_idx=None):
    br = grp.row_block
    row = pl.BlockSpec((br, D_MODEL), lambda i: (i, 0))
    vec = pl.BlockSpec((1, D_MODEL), lambda i: (0, 0))
    x_shape = jax.ShapeDtypeStruct((grp.tokens, D_MODEL), F32)
    if g_pre is None:
        return pl.pallas_call(
            _residual_kernel,
            grid=(grp.tokens // br,),
            in_specs=[row, row, vec, grp.mod_spec(gate_idx)],
            out_specs=row,
            out_shape=x_shape,
            compiler_params=_params("parallel"),
            name="residual",
        )(x, y, g_post.reshape(1, D_MODEL), mod_op)
    return pl.pallas_call(
        _residual_prenorm_kernel,
        grid=(grp.tokens // br,),
        in_specs=[row, row, vec, grp.mod_spec(gate_idx), vec, grp.mod_spec(sc_idx), grp.mod_spec(sh_idx)],
        out_specs=[row, row],
        out_shape=[x_shape, jax.ShapeDtypeStruct((grp.tokens, D_MODEL), BF16)],
        compiler_params=_params("parallel"),
        name="residual_prenorm",
    )(x, y, g_post.reshape(1, D_MODEL), mod_op, g_pre.reshape(1, D_MODEL), pre_mod_op, pre_mod_op)


def _mm_kernel(x_ref, w_ref, o_ref):
    o_ref[...] = jnp.dot(x_ref[...].astype(BF16), w_ref[...],
                         preferred_element_type=F32).astype(o_ref.dtype)


def matmul(x, w, layer, out_dtype, bm, bn):
    m, k = x.shape
    n = w.shape[2]
    return pl.pallas_call(
        _mm_kernel,
        grid=(m // bm, n // bn),
        in_specs=[pl.BlockSpec((bm, k), lambda i, j: (i, 0)),
                  pl.BlockSpec((None, k, bn), lambda i, j: (layer, 0, j))],
        out_specs=pl.BlockSpec((bm, bn), lambda i, j: (i, j)),
        out_shape=jax.ShapeDtypeStruct((m, n), out_dtype),
        compiler_params=_params("parallel", "arbitrary"),
        name="matmul",
    )(x, w)


def _mm_ksplit_kernel(x_ref, w_ref, o_ref):
    @pl.when(pl.program_id(2) == 0)
    def _():
        o_ref[...] = jnp.zeros_like(o_ref)

    o_ref[...] += jnp.dot(x_ref[...].astype(BF16), w_ref[...], preferred_element_type=F32)


def matmul_ksplit(x, w, layer, bm, bn, bk):
    m, k = x.shape
    n = w.shape[2]
    return pl.pallas_call(
        _mm_ksplit_kernel,
        grid=(m // bm, n // bn, k // bk),
        in_specs=[pl.BlockSpec((bm, bk), lambda i, j, kk: (i, kk)),
                  pl.BlockSpec((None, bk, bn), lambda i, j, kk: (layer, kk, j))],
        out_specs=pl.BlockSpec((bm, bn), lambda i, j, kk: (i, j)),
        out_shape=jax.ShapeDtypeStruct((m, n), F32),
        compiler_params=_params("parallel", "parallel", "arbitrary"),
        name="matmul_ksplit",
    )(x, w)


def _mm_cast_kernel(x_ref, w_ref, o_ref, w16_ref):
    w = w_ref[...].astype(BF16)
    w16_ref[...] = w
    o_ref[...] = jnp.dot(x_ref[...].astype(BF16), w, preferred_element_type=F32)


def matmul_cast(x, w, layer, bn):
    m, k = x.shape
    n = w.shape[2]
    return pl.pallas_call(
        _mm_cast_kernel,
        grid=(n // bn,),
        in_specs=[pl.BlockSpec((m, k), lambda j: (0, 0)),
                  pl.BlockSpec((None, k, bn), lambda j: (layer, 0, j))],
        out_specs=[pl.BlockSpec((m, bn), lambda j: (0, j)),
                   pl.BlockSpec((None, k, bn), lambda j: (0, 0, j))],
        out_shape=[jax.ShapeDtypeStruct((m, n), F32), jax.ShapeDtypeStruct((1, k, n), BF16)],
        compiler_params=_params("parallel"),
        name="matmul_cast",
    )(x, w)


def _mm_ksplit_cast_kernel(x_ref, w_ref, o_ref, w16_ref, *, bk, k_valid):
    kk = pl.program_id(1)
    row = kk * bk + lax.broadcasted_iota(jnp.int32, w_ref.shape, 0)
    w = jnp.where(row < k_valid, w_ref[...], 0.0).astype(BF16)
    w16_ref[...] = w

    @pl.when(kk == 0)
    def _():
        o_ref[...] = jnp.zeros_like(o_ref)

    o_ref[...] += jnp.dot(x_ref[...].astype(BF16), w, preferred_element_type=F32)


def matmul_ksplit_cast(x, w, layer, bn, bk):
    m, k_pad = x.shape
    k_valid, n = w.shape[1:]
    return pl.pallas_call(
        functools.partial(_mm_ksplit_cast_kernel, bk=bk, k_valid=k_valid),
        grid=(n // bn, k_pad // bk),
        in_specs=[pl.BlockSpec((m, bk), lambda j, kk: (0, kk)),
                  pl.BlockSpec((None, bk, bn), lambda j, kk: (layer, kk, j))],
        out_specs=[pl.BlockSpec((m, bn), lambda j, kk: (0, j)),
                   pl.BlockSpec((None, bk, bn), lambda j, kk: (0, kk, j))],
        out_shape=[jax.ShapeDtypeStruct((m, n), F32), jax.ShapeDtypeStruct((1, k_pad, n), BF16)],
        compiler_params=_params("parallel", "arbitrary"),
        name="matmul_ksplit_cast",
    )(x, w)


def _logf_kernel(h_ref, wf_ref, bf_ref, logf_ref, cum_ref, carry_ref, *, tile, seq_len, cw):
    i = pl.program_id(0)
    logits = lax.dot_general(wf_ref[...].astype(BF16), h_ref[...], (((1,), (1,)), ((), ())),
                             preferred_element_type=F32)[:N_HEADS]
    x = logits + bf_ref[...]
    logf = jnp.minimum(x, 0.0) - jnp.log1p(jnp.exp(-jnp.abs(x)))
    logf_ref[...] = logf

    seg = min(seq_len, cw)
    src = lax.broadcasted_iota(jnp.int32, (cw, cw), 0)
    dst = lax.broadcasted_iota(jnp.int32, (cw, cw), 1)
    tri = ((src <= dst) & (src // seg == dst // seg)).astype(F32)

    if seq_len > tile:
        @pl.when((i * tile) % seq_len == 0)
        def _():
            carry_ref[...] = jnp.zeros_like(carry_ref)
        carry = carry_ref[...]
    else:
        carry = jnp.zeros((N_HEADS, 1), F32)

    for c in range(tile // cw):
        part = jnp.dot(logf[:, c * cw:(c + 1) * cw], tri, precision=lax.Precision.HIGHEST,
                       preferred_element_type=F32)
        cum = part + carry
        cum_ref[:, c * cw:(c + 1) * cw] = cum
        if seq_len > cw:
            carry = cum[:, cw - 1:cw]
    if seq_len > tile:
        carry_ref[...] = carry


def forget_logits(grp, h, wf_t, b_forget):
    tile = min(LOGF_TILE, grp.tokens)
    cw = min(128, tile)
    kern = functools.partial(_logf_kernel, tile=tile, seq_len=grp.seq_len, cw=cw)
    out = jax.ShapeDtypeStruct((N_HEADS, grp.tokens), F32)
    return pl.pallas_call(
        kern,
        grid=(grp.tokens // tile,),
        in_specs=[pl.BlockSpec((tile, D_MODEL), lambda i: (i, 0)),
                  pl.BlockSpec((16, D_MODEL), lambda i: (0, 0)),
                  pl.BlockSpec((N_HEADS, 1), lambda i: (0, 0))],
        out_specs=[pl.BlockSpec((N_HEADS, tile), lambda i: (0, i))] * 2,
        out_shape=[out, out],
        scratch_shapes=[pltpu.VMEM((N_HEADS, 1), F32)],
        compiler_params=_params("arbitrary"),
        name="forget_logits",
    )(h, wf_t, b_forget.reshape(N_HEADS, 1))


def _gating_kernel(z_ref, lng_ref, ws_ref, bs_ref, o_ref, vt_ref, *, rows, mm_dtype):
    z = jax.nn.gelu(z_ref[...])
    u = z[:, :WIDTH]
    v = z[:, WIDTH:]
    vc = v - jnp.mean(v, axis=-1, keepdims=True)
    vn = vc * lax.rsqrt(jnp.mean(vc * vc, axis=-1, keepdims=True) + EPS) * lng_ref[...]
    vt_ref[...] = vn
    t_idx = lax.broadcasted_iota(jnp.int32, (rows, rows), 0)
    s_idx = lax.broadcasted_iota(jnp.int32, (rows, rows), 1)
    causal = s_idx <= t_idx
    for g in range(N_HEADS):
        cols = slice(g * HEAD_DIM, (g + 1) * HEAD_DIM)
        ws = jnp.where(causal, ws_ref[g], 0.0).astype(mm_dtype)
        mixed = jnp.dot(ws, vn[:, cols].astype(mm_dtype), preferred_element_type=F32) + bs_ref[g]
        o_ref[:, cols] = (u[:, cols] * mixed).astype(o_ref.dtype)


def chunk_gating(z, ln_g, ws, bs_b, n_seq, n_chunks, rows, out_dtype):
    tokens = z.shape[0]
    kern = functools.partial(_gating_kernel, rows=rows, mm_dtype=BF16 if rows >= 16 else F32)
    return pl.pallas_call(
        kern,
        grid=(n_seq, n_chunks),
        in_specs=[pl.BlockSpec((rows, 2 * WIDTH), lambda b, n: (b * n_chunks + n, 0)),
                  pl.BlockSpec((1, WIDTH), lambda b, n: (0, 0)),
                  pl.BlockSpec((N_HEADS, rows, rows), lambda b, n: (0, 0, 0)),
                  pl.BlockSpec((N_HEADS, rows, HEAD_DIM), lambda b, n: (0, 0, 0))],
        out_specs=[pl.BlockSpec((rows, WIDTH), lambda b, n: (b * n_chunks + n, 0)),
                   pl.BlockSpec((None, rows, WIDTH), lambda b, n: (b, 0, 0))],
        out_shape=[jax.ShapeDtypeStruct((tokens, WIDTH), out_dtype),
                   jax.ShapeDtypeStruct((n_seq, rows, WIDTH), F32)],
        compiler_params=_params("arbitrary", "arbitrary"),
        name="chunk_gating",
    )(z, ln_g.reshape(1, WIDTH), ws, bs_b)


def _fox_prompt_kernel(q_ref, k_ref, v_ref, fq_ref, fk_ref, o_ref, *, blk):
    qi = pl.program_id(2)
    q = (q_ref[...] * HEAD_DIM ** -0.5).astype(BF16)
    fq = fq_ref[...]
    q_pos = qi * blk + lax.broadcasted_iota(jnp.int32, (blk, blk), 0)
    k_off = lax.broadcasted_iota(jnp.int32, (blk, blk), 1)

    def step(kb, carry, masked):
        m, l, acc = carry
        ks = pl.multiple_of(kb * blk, blk)
        k = k_ref[pl.ds(ks, blk), :].astype(BF16)
        v = v_ref[pl.ds(ks, blk), :].astype(BF16)
        s = lax.dot_general(q, k, (((1,), (1,)), ((), ())), preferred_element_type=F32)
        s = s + fq - fk_ref[kb]
        if masked:
            s = jnp.where(ks + k_off <= q_pos, s, NEG_INF)
        m_new = jnp.maximum(m, jnp.max(s, axis=-1, keepdims=True))
        alpha = jnp.exp(m - m_new)
        p = jnp.exp(s - m_new)
        l = alpha * l + jnp.sum(p, axis=-1, keepdims=True)
        acc = alpha * acc + jnp.dot(p.astype(BF16), v, preferred_element_type=F32)
        return m_new, l, acc

    init = (jnp.full((blk, 1), NEG_INF, F32), jnp.zeros((blk, 1), F32), jnp.zeros((blk, HEAD_DIM), F32))
    carry = lax.fori_loop(0, qi, functools.partial(step, masked=False), init)
    _, l, acc = step(qi, carry, masked=True)
    o_ref[...] = (acc / l).astype(o_ref.dtype)


def fox_prompt(z, cum_t, n_seq, seq_len):
    blk = FOX_BLOCK
    nq = seq_len // blk
    tokens = z.shape[0]
    fq = cum_t.reshape(N_HEADS, tokens, 1)
    fk = cum_t.reshape(N_HEADS, tokens // blk, 1, blk)
    return pl.pallas_call(
        functools.partial(_fox_prompt_kernel, blk=blk),
        grid=(n_seq, N_HEADS, nq),
        in_specs=[pl.BlockSpec((blk, HEAD_DIM), lambda b, h, i: (b * nq + i, BQ_BLK + h)),
                  pl.BlockSpec((seq_len, HEAD_DIM), lambda b, h, i: (b, BK_BLK + h)),
                  pl.BlockSpec((seq_len, HEAD_DIM), lambda b, h, i: (b, BV_BLK + h)),
                  pl.BlockSpec((None, blk, 1), lambda b, h, i: (h, b * nq + i, 0)),
                  pl.BlockSpec((None, nq, 1, blk), lambda b, h, i: (h, b, 0, 0))],
        out_specs=pl.BlockSpec((blk, HEAD_DIM), lambda b, h, i: (b * nq + i, h)),
        out_shape=jax.ShapeDtypeStruct((tokens, WIDTH), BF16),
        compiler_params=_params("parallel", "parallel", "arbitrary"),
        name="fox_prompt",
    )(z, z, z, fq, fk)


def _fox_sample_kernel(pt_ref, qbd_ref, kn_ref, vn_ref, fcol_ref, frow_ref, *rest, pages, n_new):
    page_refs = rest[:3 * pages]
    o_ref, kbuf_ref, vbuf_ref, m_ref, l_ref, acc_ref, carry_ref = rest[3 * pages:]
    del pt_ref
    j = pl.program_id(1)
    rows = N_HEADS * n_new
    qbd = qbd_ref[...]
    fcol = fcol_ref[...]

    def update(s, v):
        m_old = m_ref[...]
        m_new = jnp.maximum(m_old, jnp.max(s, axis=-1, keepdims=True))
        alpha = jnp.exp(m_old - m_new)
        p = jnp.exp(s - m_new)
        l_ref[...] = alpha * l_ref[...] + jnp.sum(p, axis=-1, keepdims=True)
        acc_ref[...] = alpha * acc_ref[...] + jnp.dot(p.astype(BF16), v, preferred_element_type=F32)
        m_ref[...] = m_new

    @pl.when(j == 0)
    def _():
        m_ref[...] = jnp.full_like(m_ref, NEG_INF)
        l_ref[...] = jnp.zeros_like(l_ref)
        acc_ref[...] = jnp.zeros_like(acc_ref)
        carry_ref[...] = jnp.zeros_like(carry_ref)
        kn = kn_ref[...].astype(BF16)
        s = lax.dot_general(qbd, kn, (((1,), (1,)), ((), ())), preferred_element_type=F32)
        s = s + fcol - frow_ref[...]
        t_idx = lax.broadcasted_iota(jnp.int32, (rows, n_new), 0) % n_new
        s_idx = lax.broadcasted_iota(jnp.int32, (rows, n_new), 1)
        s = jnp.where(s_idx <= t_idx, s, NEG_INF)
        update(s, vn_ref[...].astype(BF16))

    lane = lax.broadcasted_iota(jnp.int32, (N_HEADS, PAGE_SIZE), 1)
    later = carry_ref[...]
    bias_parts = []
    for p in range(pages):
        k_ref, v_ref, lp_ref = page_refs[3 * p:3 * p + 3]
        key_rows = slice(p * PAGE_SIZE, (p + 1) * PAGE_SIZE)
        for h in range(N_HEADS):
            head_rows = pl.ds(h, PAGE_SIZE, stride=N_HEADS)
            cols = slice(h * HEAD_DIM, (h + 1) * HEAD_DIM)
            kbuf_ref[key_rows, cols] = k_ref[head_rows, :].astype(BF16)
            vbuf_ref[key_rows, cols] = v_ref[head_rows, :].astype(BF16)
        lp = lp_ref[...]
        suf = lp
        step = 1
        while step < PAGE_SIZE:
            shifted = pltpu.roll(suf, PAGE_SIZE - step, axis=1)
            suf = suf + jnp.where(lane + step < PAGE_SIZE, shifted, 0.0)
            step *= 2
        bias_parts.append(suf - lp + later)
        later = later + suf[:, 0:1]
    carry_ref[...] = later
    n_keys = pages * PAGE_SIZE
    bias = jnp.concatenate(bias_parts, axis=1)
    bias = jnp.broadcast_to(bias[:, None, :], (N_HEADS, n_new, n_keys)).reshape(rows, n_keys)
    s = lax.dot_general(qbd, kbuf_ref[...], (((1,), (1,)), ((), ())), preferred_element_type=F32)
    update(s + fcol + bias, vbuf_ref[...])

    @pl.when(j == pl.num_programs(1) - 1)
    def _():
        inv = 1.0 / l_ref[...]
        for h in range(N_HEADS):
            r = slice(h * n_new, (h + 1) * n_new)
            c = slice(h * HEAD_DIM, (h + 1) * HEAD_DIM)
            o_ref[:, c] = acc_ref[r, c] * inv[r]


def fox_sample(page_table, qbd, k_new, v_new, fcol, frow, cache_k, cache_v, cache_lp_t, layer):
    n_seq, n_pages = page_table.shape
    n_new = k_new.shape[1]
    rows = N_HEADS * n_new
    pages = FOX_PAGES
    steps = n_pages // pages

    def page_map(p):
        def index(b, j, pt):
            return (layer, pt[b, n_pages - 1 - (j * pages + p)], 0, 0)
        return index

    page_specs = []
    page_args = []
    for p in range(pages):
        page_specs += [pl.BlockSpec((None, None, PAGE_SIZE * N_HEADS, HEAD_DIM), page_map(p)),
                       pl.BlockSpec((None, None, PAGE_SIZE * N_HEADS, HEAD_DIM), page_map(p)),
                       pl.BlockSpec((None, None, N_HEADS, PAGE_SIZE), page_map(p))]
        page_args += [cache_k, cache_v, cache_lp_t]

    def per_seq(shape):
        return pl.BlockSpec((None,) + shape, lambda b, j, pt: (b, 0, 0))

    grid_spec = pltpu.PrefetchScalarGridSpec(
        num_scalar_prefetch=1,
        grid=(n_seq, steps),
        in_specs=[per_seq((rows, WIDTH)), per_seq((n_new, WIDTH)), per_seq((n_new, WIDTH)),
                  per_seq((rows, 1)), per_seq((rows, n_new))] + page_specs,
        out_specs=per_seq((n_new, WIDTH)),
        scratch_shapes=[pltpu.VMEM((pages * PAGE_SIZE, WIDTH), BF16), pltpu.VMEM((pages * PAGE_SIZE, WIDTH), BF16),
                        pltpu.VMEM((rows, 1), F32), pltpu.VMEM((rows, 1), F32),
                        pltpu.VMEM((rows, WIDTH), F32), pltpu.VMEM((N_HEADS, 1), F32)],
    )
    return pl.pallas_call(
        functools.partial(_fox_sample_kernel, pages=pages, n_new=n_new),
        grid_spec=grid_spec,
        out_shape=jax.ShapeDtypeStruct((n_seq, n_new, WIDTH), F32),
        compiler_params=_params("arbitrary", "arbitrary"),
        name="fox_sample",
    )(page_table, qbd, k_new, v_new, fcol, frow, *page_args)


def _retention_kernel(q_ref, k_ref, v_ref, g_ref, cos_ref, sin_ref, dmat_ref, qdec_ref, kdec_ref,
                      cdec_ref, gn_ref, s0_ref, o_ref, s_ref, *, chunk, n_chunks, mm_dtype):
    dmat = dmat_ref[...]
    qdec = qdec_ref[...]
    kdec = kdec_ref[...]
    cdec = cdec_ref[...]
    gn = gn_ref[...]

    def rope(x, cos, sin):
        return x * cos + pltpu.roll(x, HEAD_DIM // 2, axis=1) * sin

    def body(i, state):
        r = pl.multiple_of(i * chunk, chunk)
        rows = pl.ds(r, chunk)
        cos = cos_ref[rows, :]
        sin = sin_ref[rows, :]
        q = rope(q_ref[rows, :], cos, sin)
        k = rope(k_ref[rows, :], cos, sin) * HEAD_DIM ** -0.5
        v = v_ref[rows, :].astype(mm_dtype)
        inner = lax.dot_general(q.astype(mm_dtype), k.astype(mm_dtype), (((1,), (1,)), ((), ())),
                                preferred_element_type=F32) * dmat
        o = (jnp.dot(inner.astype(mm_dtype), v, preferred_element_type=F32)
             + jnp.dot((q * qdec).astype(mm_dtype), state.astype(mm_dtype), preferred_element_type=F32))
        kd_t = (k * kdec).T.astype(mm_dtype)
        state = state * cdec + jnp.dot(kd_t, v, preferred_element_type=F32)
        oc = o - jnp.mean(o, axis=-1, keepdims=True)
        y = oc * lax.rsqrt(jnp.mean(oc * oc, axis=-1, keepdims=True) + EPS) * gn
        gate = g_ref[rows, :]
        o_ref[rows, :] = (gate * jax.nn.sigmoid(gate) * y).astype(o_ref.dtype)
        return state

    s_ref[...] = lax.fori_loop(0, n_chunks, body, s0_ref[...], unroll=min(RET_UNROLL, n_chunks))


def retention(z, s0, tabs, gn_g, n_seq, seq_len, out_dtype):
    cos, sin, dmat, qdec, kdec, cdec = tabs
    chunk = dmat.shape[-1]
    tokens = z.shape[0]

    def col(blk):
        return pl.BlockSpec((seq_len, HEAD_DIM), lambda b, h: (b, blk + h))

    def per_head(shape):
        return pl.BlockSpec((None,) + shape, lambda b, h: (h, 0, 0))

    table = pl.BlockSpec((seq_len, HEAD_DIM), lambda b, h: (0, 0))
    state = pl.BlockSpec((None, None, HEAD_DIM, HEAD_DIM), lambda b, h: (b, h, 0, 0))
    kern = functools.partial(_retention_kernel, chunk=chunk, n_chunks=seq_len // chunk,
                             mm_dtype=BF16 if chunk >= 16 else F32)
    return pl.pallas_call(
        kern,
        grid=(n_seq, N_HEADS),
        in_specs=[col(CQ_BLK), col(CK_BLK), col(CV_BLK), col(CG_BLK), table, table,
                  per_head((chunk, chunk)), per_head((chunk, HEAD_DIM)), per_head((chunk, HEAD_DIM)),
                  per_head((1, HEAD_DIM)), pl.BlockSpec((1, HEAD_DIM), lambda b, h: (0, h)), state],
        out_specs=[pl.BlockSpec((seq_len, HEAD_DIM), lambda b, h: (b, h)), state],
        out_shape=[jax.ShapeDtypeStruct((tokens, WIDTH), out_dtype),
                   jax.ShapeDtypeStruct((n_seq, N_HEADS, HEAD_DIM, HEAD_DIM), F32)],
        compiler_params=_params("parallel", "parallel"),
        name="retention",
    )(z, z, z, z, cos, sin, dmat, qdec, kdec, cdec, gn_g.reshape(1, WIDTH), s0)


def retention_tables(pos, chunk):
    half = HEAD_DIM // 2
    inv = ROPE_BASE ** (-jnp.arange(half, dtype=F32) / half)
    ang = pos.astype(F32)[:, None] * inv[None, :]
    cos = jnp.concatenate([jnp.cos(ang), jnp.cos(ang)], axis=-1)
    sin = jnp.concatenate([-jnp.sin(ang), jnp.sin(ang)], axis=-1)
    lg = jnp.log1p(-jnp.exp2(-5.0 - jnp.arange(N_HEADS, dtype=F32)))
    idx = jnp.arange(chunk, dtype=F32)
    diff = idx[:, None] - idx[None, :]
    dmat = jnp.where(diff >= 0, jnp.exp(jnp.maximum(diff, 0.0)[None] * lg[:, None, None]), 0.0)
    qdec = jnp.exp((idx[None, :] + 1.0) * lg[:, None])
    kdec = jnp.exp((chunk - 1.0 - idx)[None, :] * lg[:, None])
    cdec = jnp.exp(chunk * lg)
    lanes = (N_HEADS, chunk, HEAD_DIM)
    return (cos, sin, dmat, jnp.broadcast_to(qdec[:, :, None], lanes),
            jnp.broadcast_to(kdec[:, :, None], lanes),
            jnp.broadcast_to(cdec[:, None, None], (N_HEADS, 1, HEAD_DIM)))


HALO = 8


def _conv3(stage_ref, w_ref, rows):
    return (w_ref[2:3, :] * stage_ref[HALO:HALO + rows, :]
            + w_ref[1:2, :] * stage_ref[HALO - 1:HALO - 1 + rows, :]
            + w_ref[0:1, :] * stage_ref[HALO - 2:HALO - 2 + rows, :])


def _short_conv_kernel(bg_ref, cg_ref, hd_ref, w_ref, hist_ref, o_ref, hist_out_ref, stage_ref, *, rows):
    i = pl.program_id(1)

    @pl.when(i == 0)
    def _():
        stage_ref[HALO - 2:HALO, :] = hist_ref[...]

    stage_ref[HALO:HALO + rows, :] = cg_ref[...] * hd_ref[...]
    y = _conv3(stage_ref, w_ref, rows)
    o_ref[...] = (bg_ref[...] * y).astype(o_ref.dtype)
    tail = stage_ref[rows:rows + HALO, :]
    stage_ref[0:HALO, :] = tail
    hist_out_ref[...] = tail[HALO - 2:, :]


def short_conv(z, w_conv, hist, n_seq, seq_len, rows, out_dtype):
    tokens = z.shape[0]
    per = seq_len // rows

    def col(blk):
        return pl.BlockSpec((rows, WIDTH), lambda b, i: (b * per + i, blk))

    hist_spec = pl.BlockSpec((None, CONV_WIDTH - 1, WIDTH), lambda b, i: (b, 0, 0))
    return pl.pallas_call(
        functools.partial(_short_conv_kernel, rows=rows),
        grid=(n_seq, per),
        in_specs=[col(9), col(10), col(11), pl.BlockSpec((CONV_WIDTH, WIDTH), lambda b, i: (0, 0)), hist_spec],
        out_specs=[pl.BlockSpec((rows, WIDTH), lambda b, i: (b * per + i, 0)), hist_spec],
        out_shape=[jax.ShapeDtypeStruct((tokens, WIDTH), out_dtype),
                   jax.ShapeDtypeStruct((n_seq, CONV_WIDTH - 1, WIDTH), F32)],
        scratch_shapes=[pltpu.VMEM((HALO + rows, WIDTH), F32)],
        compiler_params=_params("arbitrary", "arbitrary"),
        name="short_conv",
    )(z, z, z, w_conv, hist)


def _merge_kernel(h_ref, wm_ref, bm_ref, o_ref, wb_ref, out_ref, acc_ref):
    n = pl.program_id(2)
    gate = jnp.dot(h_ref[...], wm_ref[...], preferred_element_type=F32) + bm_ref[...]
    proj = jnp.dot(o_ref[...].astype(BF16), wb_ref[...], preferred_element_type=F32)
    term = jax.nn.sigmoid(gate) * proj

    @pl.when(n == 0)
    def _():
        acc_ref[...] = term

    @pl.when(n > 0)
    def _():
        acc_ref[...] += term

    @pl.when(n == pl.num_programs(2) - 1)
    def _():
        out_ref[...] = acc_ref[...].astype(out_ref.dtype)


def _merge_cast_kernel(h_ref, wm_ref, bm_ref, o_ref, wb_ref, out_ref, wm16_ref, wb16_ref, acc_ref):
    n = pl.program_id(1)
    wm = wm_ref[...].astype(BF16)
    wb = wb_ref[...].astype(BF16)
    wm16_ref[...] = wm
    wb16_ref[...] = wb
    gate = jnp.dot(h_ref[...], wm, preferred_element_type=F32) + bm_ref[...]
    proj = jnp.dot(o_ref[...].astype(BF16), wb, preferred_element_type=F32)
    term = jax.nn.sigmoid(gate) * proj

    @pl.when(n == 0)
    def _():
        acc_ref[...] = term

    @pl.when(n > 0)
    def _():
        acc_ref[...] += term

    @pl.when(n == pl.num_programs(1) - 1)
    def _():
        out_ref[...] = acc_ref[...].astype(out_ref.dtype)


def gated_merge_cast(h, branches, w_merge, b_merge, w_branch, layer, bn):
    nb, tokens, _ = branches.shape
    depth = w_merge.shape[0]
    return pl.pallas_call(
        _merge_cast_kernel,
        grid=(D_MODEL // bn, nb),
        in_specs=[pl.BlockSpec((tokens, D_MODEL), lambda j, n: (0, 0)),
                  pl.BlockSpec((None, None, D_MODEL, bn), lambda j, n: (layer, n, 0, j)),
                  pl.BlockSpec((None, None, 1, bn), lambda j, n: (layer, n, 0, j)),
                  pl.BlockSpec((None, tokens, WIDTH), lambda j, n: (n, 0, 0)),
                  pl.BlockSpec((None, None, WIDTH, bn), lambda j, n: (layer, n, 0, j))],
        out_specs=[pl.BlockSpec((tokens, bn), lambda j, n: (0, j)),
                   pl.BlockSpec((None, None, D_MODEL, bn), lambda j, n: (0, n, 0, j)),
                   pl.BlockSpec((None, None, WIDTH, bn), lambda j, n: (0, n, 0, j))],
        out_shape=[jax.ShapeDtypeStruct((tokens, D_MODEL), BF16),
                   jax.ShapeDtypeStruct((1, nb, D_MODEL, D_MODEL), BF16),
                   jax.ShapeDtypeStruct((1, nb, WIDTH, D_MODEL), BF16)],
        scratch_shapes=[pltpu.VMEM((tokens, bn), F32)],
        compiler_params=_params("arbitrary", "arbitrary"),
        name="gated_merge_cast",
    )(h, w_merge, b_merge.reshape(depth, nb, 1, D_MODEL), branches, w_branch)


def gated_merge(h, branches, w_merge, b_merge, w_branch, layer, bm, bn):
    nb, tokens, _ = branches.shape
    depth = b_merge.shape[0]
    return pl.pallas_call(
        _merge_kernel,
        grid=(tokens // bm, D_MODEL // bn, nb),
        in_specs=[pl.BlockSpec((bm, D_MODEL), lambda i, j, n: (i, 0)),
                  pl.BlockSpec((None, None, D_MODEL, bn), lambda i, j, n: (0, n, 0, j)),
                  pl.BlockSpec((None, None, 1, bn), lambda i, j, n: (layer, n, 0, j)),
                  pl.BlockSpec((None, bm, WIDTH), lambda i, j, n: (n, i, 0)),
                  pl.BlockSpec((None, None, WIDTH, bn), lambda i, j, n: (0, n, 0, j))],
        out_specs=pl.BlockSpec((bm, bn), lambda i, j, n: (i, j)),
        out_shape=jax.ShapeDtypeStruct((tokens, D_MODEL), BF16),
        scratch_shapes=[pltpu.VMEM((bm, bn), F32)],
        compiler_params=_params("parallel", "arbitrary", "arbitrary"),
        name="gated_merge",
    )(h, w_merge, b_merge.reshape(depth, nb, 1, D_MODEL), branches, w_branch)


FF_BLK = 256
FF_HALF_BLKS = D_FF // FF_BLK


def _ffn_up_kernel(h_ref, wa0_ref, wa1_ref, wb0_ref, wb1_ref, cwa_ref, cwb_ref, ba_ref, bb_ref,
                   ha_ref, hb_ref, act_ref, ta_ref, tb_ref, sa_ref, sb_ref, ca_ref, cb_ref,
                   *, rows, per_seq):
    i = pl.program_id(0)
    j = pl.program_id(1)
    first = (i % per_seq) == 0

    @pl.when(first)
    def _():
        sa_ref[HALO - 2:HALO, :] = ha_ref[...]
        sb_ref[HALO - 2:HALO, :] = hb_ref[...]

    @pl.when(jnp.logical_not(first))
    def _():
        sa_ref[0:HALO, :] = ca_ref[j]
        sb_ref[0:HALO, :] = cb_ref[j]

    h = h_ref[...]
    cur = slice(HALO, HALO + rows)
    sa_ref[cur, :FF_BLK] = jnp.dot(h, wa0_ref[...], preferred_element_type=F32)
    sb_ref[cur, :FF_BLK] = jnp.dot(h, wb0_ref[...], preferred_element_type=F32)
    sa_ref[cur, FF_BLK:] = jnp.dot(h, wa1_ref[...], preferred_element_type=F32)
    sb_ref[cur, FF_BLK:] = jnp.dot(h, wb1_ref[...], preferred_element_type=F32)
    a = _conv3(sa_ref, cwa_ref, rows) + ba_ref[...]
    b = _conv3(sb_ref, cwb_ref, rows) + bb_ref[...]
    act_ref[...] = (a * jax.nn.sigmoid(a) * b).astype(act_ref.dtype)
    tail_a = sa_ref[rows:rows + HALO, :]
    tail_b = sb_ref[rows:rows + HALO, :]
    ca_ref[j] = tail_a
    cb_ref[j] = tail_b
    ta_ref[...] = tail_a
    tb_ref[...] = tail_b


def ffn_up(h, w_up, layer, cwa, cwb, ba, bb, hist_a, hist_b, n_seq, seq_len, bm):
    tokens = h.shape[0]
    per_seq = seq_len // bm
    bn = 2 * FF_BLK
    nj = D_FF_PAD // bn
    last_blk = 2 * FF_HALF_BLKS - 1

    def wspec(first_blk, sub):
        return pl.BlockSpec((None, D_MODEL, FF_BLK),
                            lambda i, j: (layer, 0, jnp.minimum(first_blk + 2 * j + sub, last_blk)))

    cspec = pl.BlockSpec((CONV_WIDTH, bn), lambda i, j: (0, j))
    bspec = pl.BlockSpec((1, bn), lambda i, j: (0, j))
    hspec = pl.BlockSpec((None, CONV_WIDTH - 1, bn), lambda i, j: (i // per_seq, 0, j))
    tspec = pl.BlockSpec((None, HALO, bn), lambda i, j: (i, 0, j))
    tail_shape = jax.ShapeDtypeStruct((tokens // bm, HALO, D_FF_PAD), F32)
    act, tail_a, tail_b = pl.pallas_call(
        functools.partial(_ffn_up_kernel, rows=bm, per_seq=per_seq),
        grid=(tokens // bm, nj),
        in_specs=[pl.BlockSpec((bm, D_MODEL), lambda i, j: (i, 0)),
                  wspec(0, 0), wspec(0, 1), wspec(FF_HALF_BLKS, 0), wspec(FF_HALF_BLKS, 1),
                  cspec, cspec, bspec, bspec, hspec, hspec],
        out_specs=[pl.BlockSpec((bm, bn), lambda i, j: (i, j)), tspec, tspec],
        out_shape=[jax.ShapeDtypeStruct((tokens, D_FF_PAD), BF16), tail_shape, tail_shape],
        scratch_shapes=[pltpu.VMEM((HALO + bm, bn), F32), pltpu.VMEM((HALO + bm, bn), F32),
                        pltpu.VMEM((nj, HALO, bn), F32), pltpu.VMEM((nj, HALO, bn), F32)],
        compiler_params=_params("arbitrary", "arbitrary"),
        name="ffn_up",
    )(h, w_up, w_up, w_up, w_up, cwa, cwb, ba, bb, hist_a, hist_b)
    return act, tail_a[per_seq - 1::per_seq], tail_b[per_seq - 1::per_seq]


def _ffn_act_kernel(ua_ref, ub_ref, cwa_ref, cwb_ref, ba_ref, bb_ref, ha_ref, hb_ref, act_ref,
                    sa_ref, sb_ref, *, rows):
    sa_ref[HALO - 2:HALO, :] = ha_ref[...]
    sb_ref[HALO - 2:HALO, :] = hb_ref[...]
    sa_ref[HALO:HALO + rows, :] = ua_ref[...]
    sb_ref[HALO:HALO + rows, :] = ub_ref[...]
    a = _conv3(sa_ref, cwa_ref, rows) + ba_ref[...]
    b = _conv3(sb_ref, cwb_ref, rows) + bb_ref[...]
    act_ref[...] = (a * jax.nn.sigmoid(a) * b).astype(act_ref.dtype)


def ffn_act(up, w_conv, bias, hist, n_seq, seq_len):
    tokens = up.shape[0]
    bn = D_FF // 2
    cspec = [pl.BlockSpec((CONV_WIDTH, bn), lambda b, j, o=o: (0, o + j)) for o in (0, 2)]
    bspec = [pl.BlockSpec((1, bn), lambda b, j, o=o: (0, o + j)) for o in (0, 2)]
    hspec = [pl.BlockSpec((None, CONV_WIDTH - 1, bn), lambda b, j, o=o: (b, 0, o + j)) for o in (0, 2)]
    uspec = [pl.BlockSpec((seq_len, bn), lambda b, j, o=o: (b, o + j)) for o in (0, 2)]
    return pl.pallas_call(
        functools.partial(_ffn_act_kernel, rows=seq_len),
        grid=(n_seq, 2),
        in_specs=uspec + cspec + bspec + hspec,
        out_specs=pl.BlockSpec((seq_len, bn), lambda b, j: (b, j)),
        out_shape=jax.ShapeDtypeStruct((tokens, D_FF), F32),
        scratch_shapes=[pltpu.VMEM((HALO + seq_len, bn), F32), pltpu.VMEM((HALO + seq_len, bn), F32)],
        compiler_params=_params("parallel", "parallel"),
        name="ffn_act",
    )(up, up, w_conv, w_conv, bias, bias, hist, hist)


WZ_K = 1024


def _in_proj_cast_kernel(x_ref, a_ref, b_ref, z_ref, wz_ref):
    j = pl.program_id(0)
    kb = pl.program_id(1)

    def emit(w_t):
        w = w_t.T.astype(BF16)
        wz_ref[...] = w
        part = jnp.dot(x_ref[...], w, preferred_element_type=F32)

        @pl.when(kb == 0)
        def _():
            z_ref[...] = part

        @pl.when(kb > 0)
        def _():
            z_ref[...] += part

    @pl.when(j < FORGET_COL // WIDTH)
    def _():
        emit(a_ref[...])

    @pl.when(j >= FORGET_COL // WIDTH)
    def _():
        emit(jnp.concatenate([a_ref[N_HEADS:, :], b_ref[...]], axis=0))


def in_proj_cast(x, w_in_t, layer):
    m, d = x.shape
    sub = WIDTH // N_HEADS
    return pl.pallas_call(
        _in_proj_cast_kernel,
        grid=(Z_COLS // WIDTH, d // WZ_K),
        in_specs=[pl.BlockSpec((m, WZ_K), lambda j, kb: (0, kb)),
                  pl.BlockSpec((None, WIDTH, WZ_K), lambda j, kb: (layer, j, kb)),
                  pl.BlockSpec((None, N_HEADS, WZ_K), lambda j, kb: (layer, sub * (j + 1), kb))],
        out_specs=[pl.BlockSpec((m, WIDTH), lambda j, kb: (0, j)),
                   pl.BlockSpec((None, WZ_K, WIDTH), lambda j, kb: (0, kb, j))],
        out_shape=[jax.ShapeDtypeStruct((m, Z_COLS), F32), jax.ShapeDtypeStruct((1, d, Z_COLS), BF16)],
        compiler_params=_params("parallel", "arbitrary"),
        name="in_proj_cast",
    )(x, w_in_t, w_in_t)


def _pad_cols(a, n):
    return jnp.pad(a, ((0, 0), (0, n - a.shape[1])))


def _split_hist(hist):
    pad = ((0, 0), (0, 0), (0, D_FF_PAD - D_FF))
    return jnp.pad(hist[..., :D_FF], pad), jnp.pad(hist[..., D_FF:], pad)


def _join_hist(tail_a, tail_b):
    return jnp.concatenate([tail_a[..., :D_FF], tail_b[..., :D_FF]], axis=-1)


def mixer_and_ffn(grp, layer, x, h, mod_op, lw, small, tabs, attend, ret_s0, conv_hist, ffn_hist, next_pre,
                  next_mod_op):
    n_seq, seq_len, tokens = grp.n_seq, grp.seq_len, grp.tokens
    bmm = grp.mm_block
    act_dtype = BF16 if seq_len >= 16 else F32
    cast = "f32" in lw
    lw16 = dict(b_merge=lw["b_merge"]) if cast else lw

    if cast:
        z, lw16["wz"] = in_proj_cast(h, lw["w_in_t"], layer)
    else:
        z = matmul(h, lw["wz"], 0, F32, bmm, MM_COLS)
    logf_t, cum_t = forget_logits(grp, h, small["wf_t"], small["b_forget"])

    chunk_rows = min(seq_len, CHUNK)
    n_chunks = seq_len // chunk_rows
    o_a, v_tail = chunk_gating(z, small["a_ln_g"], small["a_ws"][:, :chunk_rows, :chunk_rows],
                               jnp.broadcast_to(small["a_bs"][:, :chunk_rows, None],
                                                (N_HEADS, chunk_rows, HEAD_DIM)),
                               n_seq, n_chunks, chunk_rows, act_dtype)
    o_b = attend(z, cum_t)
    o_c, ret_state = retention(z, ret_s0, tabs, small["ret_gn_g"], n_seq, seq_len, act_dtype)
    conv_rows = min(seq_len, CONV_ROWS)
    o_d, conv_state = short_conv(z, small["w_sc_conv"], conv_hist, n_seq, seq_len, conv_rows, act_dtype)

    branches = jnp.stack([o_a, o_b.astype(act_dtype), o_c, o_d], axis=0)
    if cast:
        merged, lw16["w_merge"], lw16["w_branch"] = gated_merge_cast(
            h, branches, lw["w_merge"], lw["b_merge"], lw["w_branch"], layer, CAST_COLS)
        y, lw16["w_out"] = matmul_cast(merged, lw["w_out"], layer, CAST_COLS)
    else:
        merged = gated_merge(h, branches, lw["w_merge"], lw["b_merge"], lw["w_branch"], layer, bmm, MERGE_COLS)
        y = matmul(merged, lw["w_out"], 0, F32, bmm, MM_COLS)
    x, h2 = residual(grp, x, y, small["g_post_mix"], mod_op, 2, small["g_pre_ffn"], mod_op, 3, 4)

    cw = small["w_ffn_conv"]
    bias = small["b_ffn_conv"].reshape(1, 2 * D_FF)
    if cast:
        up, lw16["w_up"] = matmul_cast(h2, lw["w_up"], layer, CAST_COLS)
        act = _pad_cols(ffn_act(up, cw, bias, ffn_hist, n_seq, seq_len), D_FF_PAD)
        ffn_state = up.reshape(n_seq, seq_len, 2 * D_FF)[:, seq_len - 2:]
        y2, lw16["w_down"] = matmul_ksplit_cast(act, lw["w_down"], layer, CAST_COLS, DOWN_K)
    else:
        hist_a, hist_b = _split_hist(ffn_hist)
        act, tail_a, tail_b = ffn_up(h2, lw["w_up"], 0,
                                     _pad_cols(cw[:, :D_FF], D_FF_PAD), _pad_cols(cw[:, D_FF:], D_FF_PAD),
                                     _pad_cols(bias[:, :D_FF], D_FF_PAD), _pad_cols(bias[:, D_FF:], D_FF_PAD),
                                     hist_a, hist_b, n_seq, seq_len, bmm)
        ffn_state = _join_hist(tail_a[:, HALO - 2:], tail_b[:, HALO - 2:])
        y2 = matmul_ksplit(act, lw["w_down"], 0, bmm, MM_COLS, DOWN_K)
    if next_pre is None:
        x = residual(grp, x, y2, small["g_post_ffn"], mod_op, 5)
        h_next = None
    else:
        x, h_next = residual(grp, x, y2, small["g_post_ffn"], mod_op, 5, next_pre, next_mod_op, 0, 1)

    k_b = z[:, BK_BLK * HEAD_DIM:BV_BLK * HEAD_DIM].reshape(n_seq, seq_len, N_HEADS, HEAD_DIM)
    v_b = z[:, BV_BLK * HEAD_DIM:CQ_BLK * HEAD_DIM].reshape(n_seq, seq_len, N_HEADS, HEAD_DIM)
    logf = logf_t.T.reshape(n_seq, seq_len, N_HEADS)
    return x, h_next, (k_b, v_b, logf, ret_state, conv_state, v_tail, ffn_state), lw16


def kernel(x_prompt, x_sample, cache_k, cache_v, cache_logf, state_ret, state_conv, state_ffn_conv,
           page_table, c_prompt, c_sample, w_ada, b_ada, g_pre_mix, g_post_mix, g_pre_ffn, g_post_ffn,
           w_in, b_forget, a_ln_g, a_ws, a_bs, ret_gn_g, w_sc_conv, w_branch, w_merge, b_merge, w_out,
           w_up, w_ffn_conv, b_ffn_conv, w_down):
    depth = w_in.shape[0]
    bp, lp, _ = x_prompt.shape
    bs, ls, _ = x_sample.shape
    grp_p = Group(bp, lp, row_block=ROW_BLOCK, mm_block=MM_ROWS)
    grp_s = Group(bs, ls, row_block=bs * ls, mm_block=bs * ls)

    c_all = jnp.pad(jnp.concatenate([c_prompt, c_sample], axis=0), ((0, 16 - bp - bs), (0, 0)))
    mod = ada_modulation(c_all, w_ada, b_ada)

    tabs_p = retention_tables(jnp.arange(lp), math.gcd(lp, CHUNK))
    tabs_s = retention_tables(PAST_LEN + jnp.arange(ls), math.gcd(ls, CHUNK))

    n_pool = cache_k.shape[1]
    cache_k2 = cache_k.reshape(depth, n_pool, PAGE_SIZE * N_HEADS, HEAD_DIM)
    cache_v2 = cache_v.reshape(depth, n_pool, PAGE_SIZE * N_HEADS, HEAD_DIM)
    cache_lp_t = cache_logf.transpose(0, 1, 3, 2)

    xp = x_prompt.reshape(grp_p.tokens, D_MODEL)
    xs = x_sample.reshape(grp_s.tokens, D_MODEL)
    mods_p = [grp_p.mod_operand(mod[l, :bp]) for l in range(depth)]
    mods_s = [grp_s.mod_operand(mod[l, bp:bp + bs]) for l in range(depth)]
    hp = prenorm(grp_p, xp, g_pre_mix[0], mods_p[0], 0, 1)
    hs = prenorm(grp_s, xs, g_pre_mix[0], mods_s[0], 0, 1)

    zeros_ret = jnp.zeros((bp, N_HEADS, HEAD_DIM, HEAD_DIM), F32)
    zeros_conv = jnp.zeros((bp, CONV_WIDTH - 1, WIDTH), F32)
    zeros_ffn = jnp.zeros((bp, CONV_WIDTH - 1, 2 * D_FF), F32)
    eye = jnp.eye(N_HEADS, dtype=F32)

    w_in_t = jnp.swapaxes(w_in, 1, 2)
    lw32 = dict(f32=True, w_in_t=w_in_t, w_merge=w_merge, b_merge=b_merge, w_branch=w_branch,
                w_out=w_out, w_up=w_up, w_down=w_down)

    st_p, st_s = [], []
    for l in range(depth):
        wf_t = w_in_t[l, FORGET_COL:FORGET_COL + N_HEADS]
        small = dict(wf_t=jnp.pad(wf_t, ((0, 16 - N_HEADS), (0, 0))),
                     b_forget=b_forget[l], a_ln_g=a_ln_g[l], a_ws=a_ws[l], a_bs=a_bs[l],
                     ret_gn_g=ret_gn_g[l], w_sc_conv=w_sc_conv[l],
                     w_ffn_conv=w_ffn_conv[l], b_ffn_conv=b_ffn_conv[l],
                     g_post_mix=g_post_mix[l], g_pre_ffn=g_pre_ffn[l], g_post_ffn=g_post_ffn[l])
        next_pre = g_pre_mix[l + 1] if l + 1 < depth else None

        def attend_p(z, cum_t):
            return fox_prompt(z, cum_t, bp, lp)

        def attend_s(z, cum_t, l=l):
            zs = z.reshape(bs, ls, Z_COLS)
            q = zs[..., BQ_BLK * HEAD_DIM:BK_BLK * HEAD_DIM].reshape(bs, ls, N_HEADS, HEAD_DIM)
            q = q.transpose(0, 2, 1, 3) * HEAD_DIM ** -0.5
            qbd = (q[:, :, :, None, :] * eye[None, :, None, :, None]).reshape(bs, N_HEADS * ls, WIDTH)
            k_new = zs[..., BK_BLK * HEAD_DIM:BV_BLK * HEAD_DIM]
            v_new = zs[..., BV_BLK * HEAD_DIM:CQ_BLK * HEAD_DIM]
            cum = cum_t.reshape(N_HEADS, bs, ls).transpose(1, 0, 2)
            fcol = cum.reshape(bs, N_HEADS * ls, 1)
            frow = jnp.broadcast_to(cum[:, :, None, :], (bs, N_HEADS, ls, ls)).reshape(bs, N_HEADS * ls, ls)
            o = fox_sample(page_table, qbd.astype(BF16), k_new, v_new, fcol, frow,
                           cache_k2, cache_v2, cache_lp_t, l)
            return o.reshape(bs * ls, WIDTH)

        xs, hs, ss, lw16 = mixer_and_ffn(grp_s, l, xs, hs, mods_s[l], lw32, small, tabs_s, attend_s,
                                         state_ret[l], state_conv[l], state_ffn_conv[l], next_pre,
                                         mods_s[l + 1] if l + 1 < depth else None)
        xp, hp, sp, _ = mixer_and_ffn(grp_p, l, xp, hp, mods_p[l], lw16, small, tabs_p, attend_p,
                                      zeros_ret, zeros_conv, zeros_ffn, next_pre,
                                      mods_p[l + 1] if l + 1 < depth else None)
        st_p.append(sp)
        st_s.append(ss)

    def stack(states, idx):
        return jnp.stack([s[idx] for s in states], axis=0)

    return (xp.reshape(bp, lp, D_MODEL), xs.reshape(bs, ls, D_MODEL),
            stack(st_p, 0), stack(st_p, 1), stack(st_p, 2),
            stack(st_s, 0), stack(st_s, 1), stack(st_s, 2),
            stack(st_p, 3), stack(st_s, 3), stack(st_p, 4), stack(st_s, 4),
            stack(st_p, 5), stack(st_s, 5), stack(st_p, 6), stack(st_s, 6))
```

```python
import functools
import math

import jax
import jax.numpy as jnp
from jax import lax
from jax.experimental import pallas as pl
from jax.experimental.pallas import tpu as pltpu

D_MODEL = 4096
HEAD_DIM = 128
N_HEADS = 8
WIDTH = N_HEADS * HEAD_DIM
CHUNK = 128
PAST_LEN = 16384
PAGE_SIZE = 128
ROPE_BASE = 10000.0
CONV_WIDTH = 3
D_FF = 11008
D_FF_PAD = 11264
N_MOD = 6
EPS = 1e-6
NEG_INF = -1e30
Z_COLS = 12 * WIDTH
FORGET_COL = 5 * WIDTH

BQ_BLK, BK_BLK, BV_BLK = 16, 24, 32
CQ_BLK, CK_BLK, CV_BLK, CG_BLK = 40, 48, 56, 64

VMEM_LIMIT_BYTES = 56 * 1024 * 1024

MM_ROWS = 1024
ROW_BLOCK = 256
MM_COLS = 1024
MERGE_COLS = 1024
CAST_COLS = 512
DOWN_K = D_FF_PAD // 4
ADA_COLS = 512
LOGF_TILE = 512
FOX_BLOCK = 512
FOX_PAGES = 16
CONV_ROWS = 512
RET_UNROLL = 16

BF16 = jnp.bfloat16
F32 = jnp.float32


def _params(*semantics):
    return pltpu.CompilerParams(dimension_semantics=semantics, vmem_limit_bytes=VMEM_LIMIT_BYTES)


def _ada_kernel(c_ref, w_ref, b_ref, o_ref):
    c = c_ref[...]
    s = (c * jax.nn.sigmoid(c)).astype(BF16)
    o_ref[...] = jnp.dot(s, w_ref[...].astype(BF16), preferred_element_type=F32) + b_ref[...]


def ada_modulation(c_all, w_ada, b_ada):
    depth, d, n = w_ada.shape
    rows = c_all.shape[0]
    bn = ADA_COLS
    return pl.pallas_call(
        _ada_kernel,
        grid=(depth, n // bn),
        in_specs=[
            pl.BlockSpec((rows, d), lambda l, j: (0, 0)),
            pl.BlockSpec((None, d, bn), lambda l, j: (l, 0, j)),
            pl.BlockSpec((None, 1, bn), lambda l, j: (l, 0, j)),
        ],
        out_specs=pl.BlockSpec((None, rows, bn), lambda l, j: (l, 0, j)),
        out_shape=jax.ShapeDtypeStruct((depth, rows, n), F32),
        compiler_params=_params("arbitrary", "arbitrary"),
        name="ada_modulation",
    )(c_all, w_ada, b_ada.reshape(depth, 1, n))


class Group:
    def __init__(self, n_seq, seq_len, row_block, mm_block):
        self.n_seq = n_seq
        self.seq_len = seq_len
        self.tokens = n_seq * seq_len
        self.row_block = row_block
        self.mm_block = mm_block
        self.per_seq_rows = row_block <= seq_len

    def mod_operand(self, mod):
        if self.per_seq_rows:
            return mod.reshape(self.n_seq, N_MOD, 1, D_MODEL)
        tok = jnp.repeat(mod.reshape(self.n_seq, N_MOD, D_MODEL), self.seq_len, axis=0)
        return tok.transpose(1, 0, 2)

    def mod_spec(self, idx):
        if self.per_seq_rows:
            per = self.seq_len // self.row_block
            return pl.BlockSpec((None, None, 1, D_MODEL), lambda i: (i // per, idx, 0, 0))
        return pl.BlockSpec((None, self.row_block, D_MODEL), lambda i: (idx, i, 0))


def _rms(x, g):
    return x * lax.rsqrt(jnp.mean(x * x, axis=-1, keepdims=True) + EPS) * g


def _prenorm_kernel(x_ref, g_ref, sc_ref, sh_ref, h_ref):
    h_ref[...] = (_rms(x_ref[...], g_ref[...]) * (1.0 + sc_ref[...]) + sh_ref[...]).astype(h_ref.dtype)


def prenorm(grp, x, g, mod_op, sh_idx, sc_idx):
    br = grp.row_block
    row = pl.BlockSpec((br, D_MODEL), lambda i: (i, 0))
    vec = pl.BlockSpec((1, D_MODEL), lambda i: (0, 0))
    return pl.pallas_call(
        _prenorm_kernel,
        grid=(grp.tokens // br,),
        in_specs=[row, vec, grp.mod_spec(sc_idx), grp.mod_spec(sh_idx)],
        out_specs=row,
        out_shape=jax.ShapeDtypeStruct((grp.tokens, D_MODEL), BF16),
        compiler_params=_params("parallel"),
        name="prenorm",
    )(x, g.reshape(1, D_MODEL), mod_op, mod_op)


def _residual_kernel(x_ref, y_ref, gpost_ref, gate_ref, xo_ref):
    xo_ref[...] = x_ref[...] + gate_ref[...] * _rms(y_ref[...], gpost_ref[...])


def _residual_prenorm_kernel(x_ref, y_ref, gpost_ref, gate_ref, gpre_ref, sc_ref, sh_ref, xo_ref, h_ref):
    xn = x_ref[...] + gate_ref[...] * _rms(y_ref[...], gpost_ref[...])
    xo_ref[...] = xn
    h_ref[...] = (_rms(xn, gpre_ref[...]) * (1.0 + sc_ref[...]) + sh_ref[...]).astype(h_ref.dtype)


def residual(grp, x, y, g_post, mod_op, gate_idx, g_pre=None, pre_mod_op=None, sh_idx=None, sc_idx=None):
    br = grp.row_block
    row = pl.BlockSpec((br, D_MODEL), lambda i: (i, 0))
    vec = pl.BlockSpec((1, D_MODEL), lambda i: (0, 0))
    x_shape = jax.ShapeDtypeStruct((grp.tokens, D_MODEL), F32)
    if g_pre is None:
        return pl.pallas_call(
            _residual_kernel,
            grid=(grp.tokens // br,),
            in_specs=[row, row, vec, grp.mod_spec(gate_idx)],
            out_specs=row,
            out_shape=x_shape,
            compiler_params=_params("parallel"),
            name="residual",
        )(x, y, g_post.reshape(1, D_MODEL), mod_op)
    return pl.pallas_call(
        _residual_prenorm_kernel,
        grid=(grp.tokens // br,),
        in_specs=[row, row, vec, grp.mod_spec(gate_idx), vec, grp.mod_spec(sc_idx), grp.mod_spec(sh_idx)],
        out_specs=[row, row],
        out_shape=[x_shape, jax.ShapeDtypeStruct((grp.tokens, D_MODEL), BF16)],
        compiler_params=_params("parallel"),
        name="residual_prenorm",
    )(x, y, g_post.reshape(1, D_MODEL), mod_op, g_pre.reshape(1, D_MODEL), pre_mod_op, pre_mod_op)


def _mm_kernel(x_ref, w_ref, o_ref):
    o_ref[...] = jnp.dot(x_ref[...].astype(BF16), w_ref[...],
                         preferred_element_type=F32).astype(o_ref.dtype)


def matmul(x, w, layer, out_dtype, bm, bn):
    m, k = x.shape
    n = w.shape[2]
    return pl.pallas_call(
        _mm_kernel,
        grid=(m // bm, n // bn),
        in_specs=[pl.BlockSpec((bm, k), lambda i, j: (i, 0)),
                  pl.BlockSpec((None, k, bn), lambda i, j: (layer, 0, j))],
        out_specs=pl.BlockSpec((bm, bn), lambda i, j: (i, j)),
        out_shape=jax.ShapeDtypeStruct((m, n), out_dtype),
        compiler_params=_params("parallel", "arbitrary"),
        name="matmul",
    )(x, w)


def _mm_ksplit_kernel(x_ref, w_ref, o_ref):
    @pl.when(pl.program_id(2) == 0)
    def _():
        o_ref[...] = jnp.zeros_like(o_ref)

    o_ref[...] += jnp.dot(x_ref[...].astype(BF16), w_ref[...], preferred_element_type=F32)


def matmul_ksplit(x, w, layer, bm, bn, bk):
    m, k = x.shape
    n = w.shape[2]
    return pl.pallas_call(
        _mm_ksplit_kernel,
        grid=(m // bm, n // bn, k // bk),
        in_specs=[pl.BlockSpec((bm, bk), lambda i, j, kk: (i, kk)),
                  pl.BlockSpec((None, bk, bn), lambda i, j, kk: (layer, kk, j))],
        out_specs=pl.BlockSpec((bm, bn), lambda i, j, kk: (i, j)),
        out_shape=jax.ShapeDtypeStruct((m, n), F32),
        compiler_params=_params("parallel", "parallel", "arbitrary"),
        name="matmul_ksplit",
    )(x, w)


def _mm_cast_kernel(x_ref, w_ref, o_ref, w16_ref):
    w = w_ref[...].astype(BF16)
    w16_ref[...] = w
    o_ref[...] = jnp.dot(x_ref[...].astype(BF16), w, preferred_element_type=F32)


def matmul_cast(x, w, layer, bn):
    m, k = x.shape
    n = w.shape[2]
    return pl.pallas_call(
        _mm_cast_kernel,
        grid=(n // bn,),
        in_specs=[pl.BlockSpec((m, k), lambda j: (0, 0)),
                  pl.BlockSpec((None, k, bn), lambda j: (layer, 0, j))],
        out_specs=[pl.BlockSpec((m, bn), lambda j: (0, j)),
                   pl.BlockSpec((None, k, bn), lambda j: (0, 0, j))],
        out_shape=[jax.ShapeDtypeStruct((m, n), F32), jax.ShapeDtypeStruct((1, k, n), BF16)],
        compiler_params=_params("parallel"),
        name="matmul_cast",
    )(x, w)


def _mm_ksplit_cast_kernel(x_ref, w_ref, o_ref, w16_ref, *, bk, k_valid):
    kk = pl.program_id(1)
    row = kk * bk + lax.broadcasted_iota(jnp.int32, w_ref.shape, 0)
    w = jnp.where(row < k_valid, w_ref[...], 0.0).astype(BF16)
    w16_ref[...] = w

    @pl.when(kk == 0)
    def _():
        o_ref[...] = jnp.zeros_like(o_ref)

    o_ref[...] += jnp.dot(x_ref[...].astype(BF16), w, preferred_element_type=F32)


def matmul_ksplit_cast(x, w, layer, bn, bk):
    m, k_pad = x.shape
    k_valid, n = w.shape[1:]
    return pl.pallas_call(
        functools.partial(_mm_ksplit_cast_kernel, bk=bk, k_valid=k_valid),
        grid=(n // bn, k_pad // bk),
        in_specs=[pl.BlockSpec((m, bk), lambda j, kk: (0, kk)),
                  pl.BlockSpec((None, bk, bn), lambda j, kk: (layer, kk, j))],
        out_specs=[pl.BlockSpec((m, bn), lambda j, kk: (0, j)),
                   pl.BlockSpec((None, bk, bn), lambda j, kk: (0, kk, j))],
        out_shape=[jax.ShapeDtypeStruct((m, n), F32), jax.ShapeDtypeStruct((1, k_pad, n), BF16)],
        compiler_params=_params("parallel", "arbitrary"),
        name="matmul_ksplit_cast",
    )(x, w)


def _logf_kernel(h_ref, wf_ref, bf_ref, logf_ref, cum_ref, carry_ref, *, tile, seq_len, cw):
    i = pl.program_id(0)
    logits = lax.dot_general(wf_ref[...].astype(BF16), h_ref[...], (((1,), (1,)), ((), ())),
                             preferred_element_type=F32)[:N_HEADS]
    x = logits + bf_ref[...]
    logf = jnp.minimum(x, 0.0) - jnp.log1p(jnp.exp(-jnp.abs(x)))
    logf_ref[...] = logf

    seg = min(seq_len, cw)
    src = lax.broadcasted_iota(jnp.int32, (cw, cw), 0)
    dst = lax.broadcasted_iota(jnp.int32, (cw, cw), 1)
    tri = ((src <= dst) & (src // seg == dst // seg)).astype(F32)

    if seq_len > tile:
        @pl.when((i * tile) % seq_len == 0)
        def _():
            carry_ref[...] = jnp.zeros_like(carry_ref)
        carry = carry_ref[...]
    else:
        carry = jnp.zeros((N_HEADS, 1), F32)

    for c in range(tile // cw):
        part = jnp.dot(logf[:, c * cw:(c + 1) * cw], tri, precision=lax.Precision.HIGHEST,
                       preferred_element_type=F32)
        cum = part + carry
        cum_ref[:, c * cw:(c + 1) * cw] = cum
        if seq_len > cw:
            carry = cum[:, cw - 1:cw]
    if seq_len > tile:
        carry_ref[...] = carry


def forget_logits(grp, h, wf_t, b_forget):
    tile = min(LOGF_TILE, grp.tokens)
    cw = min(128, tile)
    kern = functools.partial(_logf_kernel, tile=tile, seq_len=grp.seq_len, cw=cw)
    out = jax.ShapeDtypeStruct((N_HEADS, grp.tokens), F32)
    return pl.pallas_call(
        kern,
        grid=(grp.tokens // tile,),
        in_specs=[pl.BlockSpec((tile, D_MODEL), lambda i: (i, 0)),
                  pl.BlockSpec((16, D_MODEL), lambda i: (0, 0)),
                  pl.BlockSpec((N_HEADS, 1), lambda i: (0, 0))],
        out_specs=[pl.BlockSpec((N_HEADS, tile), lambda i: (0, i))] * 2,
        out_shape=[out, out],
        scratch_shapes=[pltpu.VMEM((N_HEADS, 1), F32)],
        compiler_params=_params("arbitrary"),
        name="forget_logits",
    )(h, wf_t, b_forget.reshape(N_HEADS, 1))


def _gating_kernel(z_ref, lng_ref, ws_ref, bs_ref, o_ref, vt_ref, *, rows, mm_dtype):
    z = jax.nn.gelu(z_ref[...])
    u = z[:, :WIDTH]
    v = z[:, WIDTH:]
    vc = v - jnp.mean(v, axis=-1, keepdims=True)
    vn = vc * lax.rsqrt(jnp.mean(vc * vc, axis=-1, keepdims=True) + EPS) * lng_ref[...]
    vt_ref[...] = vn
    t_idx = lax.broadcasted_iota(jnp.int32, (rows, rows), 0)
    s_idx = lax.broadcasted_iota(jnp.int32, (rows, rows), 1)
    causal = s_idx <= t_idx
    for g in range(N_HEADS):
        cols = slice(g * HEAD_DIM, (g + 1) * HEAD_DIM)
        ws = jnp.where(causal, ws_ref[g], 0.0).astype(mm_dtype)
        mixed = jnp.dot(ws, vn[:, cols].astype(mm_dtype), preferred_element_type=F32) + bs_ref[g]
        o_ref[:, cols] = (u[:, cols] * mixed).astype(o_ref.dtype)


def chunk_gating(z, ln_g, ws, bs_b, n_seq, n_chunks, rows, out_dtype):
    tokens = z.shape[0]
    kern = functools.partial(_gating_kernel, rows=rows, mm_dtype=BF16 if rows >= 16 else F32)
    return pl.pallas_call(
        kern,
        grid=(n_seq, n_chunks),
        in_specs=[pl.BlockSpec((rows, 2 * WIDTH), lambda b, n: (b * n_chunks + n, 0)),
                  pl.BlockSpec((1, WIDTH), lambda b, n: (0, 0)),
                  pl.BlockSpec((N_HEADS, rows, rows), lambda b, n: (0, 0, 0)),
                  pl.BlockSpec((N_HEADS, rows, HEAD_DIM), lambda b, n: (0, 0, 0))],
        out_specs=[pl.BlockSpec((rows, WIDTH), lambda b, n: (b * n_chunks + n, 0)),
                   pl.BlockSpec((None, rows, WIDTH), lambda b, n: (b, 0, 0))],
        out_shape=[jax.ShapeDtypeStruct((tokens, WIDTH), out_dtype),
                   jax.ShapeDtypeStruct((n_seq, rows, WIDTH), F32)],
        compiler_params=_params("arbitrary", "arbitrary"),
        name="chunk_gating",
    )(z, ln_g.reshape(1, WIDTH), ws, bs_b)


def _fox_prompt_kernel(q_ref, k_ref, v_ref, fq_ref, fk_ref, o_ref, *, blk):
    qi = pl.program_id(2)
    q = (q_ref[...] * HEAD_DIM ** -0.5).astype(BF16)
    fq = fq_ref[...]
    q_pos = qi * blk + lax.broadcasted_iota(jnp.int32, (blk, blk), 0)
    k_off = lax.broadcasted_iota(jnp.int32, (blk, blk), 1)

    def step(kb, carry, masked):
        m, l, acc = carry
        ks = pl.multiple_of(kb * blk, blk)
        k = k_ref[pl.ds(ks, blk), :].astype(BF16)
        v = v_ref[pl.ds(ks, blk), :].astype(BF16)
        s = lax.dot_general(q, k, (((1,), (1,)), ((), ())), preferred_element_type=F32)
        s = s + fq - fk_ref[kb]
        if masked:
            s = jnp.where(ks + k_off <= q_pos, s, NEG_INF)
        m_new = jnp.maximum(m, jnp.max(s, axis=-1, keepdims=True))
        alpha = jnp.exp(m - m_new)
        p = jnp.exp(s - m_new)
        l = alpha * l + jnp.sum(p, axis=-1, keepdims=True)
        acc = alpha * acc + jnp.dot(p.astype(BF16), v, preferred_element_type=F32)
        return m_new, l, acc

    init = (jnp.full((blk, 1), NEG_INF, F32), jnp.zeros((blk, 1), F32), jnp.zeros((blk, HEAD_DIM), F32))
    carry = lax.fori_loop(0, qi, functools.partial(step, masked=False), init)
    _, l, acc = step(qi, carry, masked=True)
    o_ref[...] = (acc / l).astype(o_ref.dtype)


def fox_prompt(z, cum_t, n_seq, seq_len):
    blk = FOX_BLOCK
    nq = seq_len // blk
    tokens = z.shape[0]
    fq = cum_t.reshape(N_HEADS, tokens, 1)
    fk = cum_t.reshape(N_HEADS, tokens // blk, 1, blk)
    return pl.pallas_call(
        functools.partial(_fox_prompt_kernel, blk=blk),
        grid=(n_seq, N_HEADS, nq),
        in_specs=[pl.BlockSpec((blk, HEAD_DIM), lambda b, h, i: (b * nq + i, BQ_BLK + h)),
                  pl.BlockSpec((seq_len, HEAD_DIM), lambda b, h, i: (b, BK_BLK + h)),
                  pl.BlockSpec((seq_len, HEAD_DIM), lambda b, h, i: (b, BV_BLK + h)),
                  pl.BlockSpec((None, blk, 1), lambda b, h, i: (h, b * nq + i, 0)),
                  pl.BlockSpec((None, nq, 1, blk), lambda b, h, i: (h, b, 0, 0))],
        out_specs=pl.BlockSpec((blk, HEAD_DIM), lambda b, h, i: (b * nq + i, h)),
        out_shape=jax.ShapeDtypeStruct((tokens, WIDTH), BF16),
        compiler_params=_params("parallel", "parallel", "arbitrary"),
        name="fox_prompt",
    )(z, z, z, fq, fk)


def _fox_sample_kernel(pt_ref, qbd_ref, kn_ref, vn_ref, fcol_ref, frow_ref, *rest, pages, n_new):
    page_refs = rest[:3 * pages]
    o_ref, kbuf_ref, vbuf_ref, m_ref, l_ref, acc_ref, carry_ref = rest[3 * pages:]
    del pt_ref
    j = pl.program_id(1)
    rows = N_HEADS * n_new
    qbd = qbd_ref[...]
    fcol = fcol_ref[...]

    def update(s, v):
        m_old = m_ref[...]
        m_new = jnp.maximum(m_old, jnp.max(s, axis=-1, keepdims=True))
        alpha = jnp.exp(m_old - m_new)
        p = jnp.exp(s - m_new)
        l_ref[...] = alpha * l_ref[...] + jnp.sum(p, axis=-1, keepdims=True)
        acc_ref[...] = alpha * acc_ref[...] + jnp.dot(p.astype(BF16), v, preferred_element_type=F32)
        m_ref[...] = m_new

    @pl.when(j == 0)
    def _():
        m_ref[...] = jnp.full_like(m_ref, NEG_INF)
        l_ref[...] = jnp.zeros_like(l_ref)
        acc_ref[...] = jnp.zeros_like(acc_ref)
        carry_ref[...] = jnp.zeros_like(carry_ref)
        kn = kn_ref[...].astype(BF16)
        s = lax.dot_general(qbd, kn, (((1,), (1,)), ((), ())), preferred_element_type=F32)
        s = s + fcol - frow_ref[...]
        t_idx = lax.broadcasted_iota(jnp.int32, (rows, n_new), 0) % n_new
        s_idx = lax.broadcasted_iota(jnp.int32, (rows, n_new), 1)
        s = jnp.where(s_idx <= t_idx, s, NEG_INF)
        update(s, vn_ref[...].astype(BF16))

    lane = lax.broadcasted_iota(jnp.int32, (N_HEADS, PAGE_SIZE), 1)
    later = carry_ref[...]
    bias_parts = []
    for p in range(pages):
        k_ref, v_ref, lp_ref = page_refs[3 * p:3 * p + 3]
        key_rows = slice(p * PAGE_SIZE, (p + 1) * PAGE_SIZE)
        for h in range(N_HEADS):
            head_rows = pl.ds(h, PAGE_SIZE, stride=N_HEADS)
            cols = slice(h * HEAD_DIM, (h + 1) * HEAD_DIM)
            kbuf_ref[key_rows, cols] = k_ref[head_rows, :].astype(BF16)
            vbuf_ref[key_rows, cols] = v_ref[head_rows, :].astype(BF16)
        lp = lp_ref[...]
        suf = lp
        step = 1
        while step < PAGE_SIZE:
            shifted = pltpu.roll(suf, PAGE_SIZE - step, axis=1)
            suf = suf + jnp.where(lane + step < PAGE_SIZE, shifted, 0.0)
            step *= 2
        bias_parts.append(suf - lp + later)
        later = later + suf[:, 0:1]
    carry_ref[...] = later
    n_keys = pages * PAGE_SIZE
    bias = jnp.concatenate(bias_parts, axis=1)
    bias = jnp.broadcast_to(bias[:, None, :], (N_HEADS, n_new, n_keys)).reshape(rows, n_keys)
    s = lax.dot_general(qbd, kbuf_ref[...], (((1,), (1,)), ((), ())), preferred_element_type=F32)
    update(s + fcol + bias, vbuf_ref[...])

    @pl.when(j == pl.num_programs(1) - 1)
    def _():
        inv = 1.0 / l_ref[...]
        for h in range(N_HEADS):
            r = slice(h * n_new, (h + 1) * n_new)
            c = slice(h * HEAD_DIM, (h + 1) * HEAD_DIM)
            o_ref[:, c] = acc_ref[r, c] * inv[r]


def fox_sample(page_table, qbd, k_new, v_new, fcol, frow, cache_k, cache_v, cache_lp_t, layer):
    n_seq, n_pages = page_table.shape
    n_new = k_new.shape[1]
    rows = N_HEADS * n_new
    pages = FOX_PAGES
    steps = n_pages // pages

    def page_map(p):
        def index(b, j, pt):
            return (layer, pt[b, n_pages - 1 - (j * pages + p)], 0, 0)
        return index

    page_specs = []
    page_args = []
    for p in range(pages):
        page_specs += [pl.BlockSpec((None, None, PAGE_SIZE * N_HEADS, HEAD_DIM), page_map(p)),
                       pl.BlockSpec((None, None, PAGE_SIZE * N_HEADS, HEAD_DIM), page_map(p)),
                       pl.BlockSpec((None, None, N_HEADS, PAGE_SIZE), page_map(p))]
        page_args += [cache_k, cache_v, cache_lp_t]

    def per_seq(shape):
        return pl.BlockSpec((None,) + shape, lambda b, j, pt: (b, 0, 0))

    grid_spec = pltpu.PrefetchScalarGridSpec(
        num_scalar_prefetch=1,
        grid=(n_seq, steps),
        in_specs=[per_seq((rows, WIDTH)), per_seq((n_new, WIDTH)), per_seq((n_new, WIDTH)),
                  per_seq((rows, 1)), per_seq((rows, n_new))] + page_specs,
        out_specs=per_seq((n_new, WIDTH)),
        scratch_shapes=[pltpu.VMEM((pages * PAGE_SIZE, WIDTH), BF16), pltpu.VMEM((pages * PAGE_SIZE, WIDTH), BF16),
                        pltpu.VMEM((rows, 1), F32), pltpu.VMEM((rows, 1), F32),
                        pltpu.VMEM((rows, WIDTH), F32), pltpu.VMEM((N_HEADS, 1), F32)],
    )
    return pl.pallas_call(
        functools.partial(_fox_sample_kernel, pages=pages, n_new=n_new),
        grid_spec=grid_spec,
        out_shape=jax.ShapeDtypeStruct((n_seq, n_new, WIDTH), F32),
        compiler_params=_params("arbitrary", "arbitrary"),
        name="fox_sample",
    )(page_table, qbd, k_new, v_new, fcol, frow, *page_args)


def _retention_kernel(q_ref, k_ref, v_ref, g_ref, cos_ref, sin_ref, dmat_ref, qdec_ref, kdec_ref,
                      cdec_ref, gn_ref, s0_ref, o_ref, s_ref, *, chunk, n_chunks, mm_dtype):
    dmat = dmat_ref[...]
    qdec = qdec_ref[...]
    kdec = kdec_ref[...]
    cdec = cdec_ref[...]
    gn = gn_ref[...]

    def rope(x, cos, sin):
        return x * cos + pltpu.roll(x, HEAD_DIM // 2, axis=1) * sin

    def body(i, state):
        r = pl.multiple_of(i * chunk, chunk)
        rows = pl.ds(r, chunk)
        cos = cos_ref[rows, :]
        sin = sin_ref[rows, :]
        q = rope(q_ref[rows, :], cos, sin)
        k = rope(k_ref[rows, :], cos, sin) * HEAD_DIM ** -0.5
        v = v_ref[rows, :].astype(mm_dtype)
        inner = lax.dot_general(q.astype(mm_dtype), k.astype(mm_dtype), (((1,), (1,)), ((), ())),
                                preferred_element_type=F32) * dmat
        o = (jnp.dot(inner.astype(mm_dtype), v, preferred_element_type=F32)
             + jnp.dot((q * qdec).astype(mm_dtype), state.astype(mm_dtype), preferred_element_type=F32))
        kd_t = (k * kdec).T.astype(mm_dtype)
        state = state * cdec + jnp.dot(kd_t, v, preferred_element_type=F32)
        oc = o - jnp.mean(o, axis=-1, keepdims=True)
        y = oc * lax.rsqrt(jnp.mean(oc * oc, axis=-1, keepdims=True) + EPS) * gn
        gate = g_ref[rows, :]
        o_ref[rows, :] = (gate * jax.nn.sigmoid(gate) * y).astype(o_ref.dtype)
        return state

    s_ref[...] = lax.fori_loop(0, n_chunks, body, s0_ref[...], unroll=min(RET_UNROLL, n_chunks))


def retention(z, s0, tabs, gn_g, n_seq, seq_len, out_dtype):
    cos, sin, dmat, qdec, kdec, cdec = tabs
    chunk = dmat.shape[-1]
    tokens = z.shape[0]

    def col(blk):
        return pl.BlockSpec((seq_len, HEAD_DIM), lambda b, h: (b, blk + h))

    def per_head(shape):
        return pl.BlockSpec((None,) + shape, lambda b, h: (h, 0, 0))

    table = pl.BlockSpec((seq_len, HEAD_DIM), lambda b, h: (0, 0))
    state = pl.BlockSpec((None, None, HEAD_DIM, HEAD_DIM), lambda b, h: (b, h, 0, 0))
    kern = functools.partial(_retention_kernel, chunk=chunk, n_chunks=seq_len // chunk,
                             mm_dtype=BF16 if chunk >= 16 else F32)
    return pl.pallas_call(
        kern,
        grid=(n_seq, N_HEADS),
        in_specs=[col(CQ_BLK), col(CK_BLK), col(CV_BLK), col(CG_BLK), table, table,
                  per_head((chunk, chunk)), per_head((chunk, HEAD_DIM)), per_head((chunk, HEAD_DIM)),
                  per_head((1, HEAD_DIM)), pl.BlockSpec((1, HEAD_DIM), lambda b, h: (0, h)), state],
        out_specs=[pl.BlockSpec((seq_len, HEAD_DIM), lambda b, h: (b, h)), state],
        out_shape=[jax.ShapeDtypeStruct((tokens, WIDTH), out_dtype),
                   jax.ShapeDtypeStruct((n_seq, N_HEADS, HEAD_DIM, HEAD_DIM), F32)],
        compiler_params=_params("parallel", "parallel"),
        name="retention",
    )(z, z, z, z, cos, sin, dmat, qdec, kdec, cdec, gn_g.reshape(1, WIDTH), s0)


def retention_tables(pos, chunk):
    half = HEAD_DIM // 2
    inv = ROPE_BASE ** (-jnp.arange(half, dtype=F32) / half)
    ang = pos.astype(F32)[:, None] * inv[None, :]
    cos = jnp.concatenate([jnp.cos(ang), jnp.cos(ang)], axis=-1)
    sin = jnp.concatenate([-jnp.sin(ang), jnp.sin(ang)], axis=-1)
    lg = jnp.log1p(-jnp.exp2(-5.0 - jnp.arange(N_HEADS, dtype=F32)))
    idx = jnp.arange(chunk, dtype=F32)
    diff = idx[:, None] - idx[None, :]
    dmat = jnp.where(diff >= 0, jnp.exp(jnp.maximum(diff, 0.0)[None] * lg[:, None, None]), 0.0)
    qdec = jnp.exp((idx[None, :] + 1.0) * lg[:, None])
    kdec = jnp.exp((chunk - 1.0 - idx)[None, :] * lg[:, None])
    cdec = jnp.exp(chunk * lg)
    lanes = (N_HEADS, chunk, HEAD_DIM)
    return (cos, sin, dmat, jnp.broadcast_to(qdec[:, :, None], lanes),
            jnp.broadcast_to(kdec[:, :, None], lanes),
            jnp.broadcast_to(cdec[:, None, None], (N_HEADS, 1, HEAD_DIM)))


HALO = 8


def _conv3(stage_ref, w_ref, rows):
    return (w_ref[2:3, :] * stage_ref[HALO:HALO + rows, :]
            + w_ref[1:2, :] * stage_ref[HALO - 1:HALO - 1 + rows, :]
            + w_ref[0:1, :] * stage_ref[HALO - 2:HALO - 2 + rows, :])


def _short_conv_kernel(bg_ref, cg_ref, hd_ref, w_ref, hist_ref, o_ref, hist_out_ref, stage_ref, *, rows):
    i = pl.program_id(1)

    @pl.when(i == 0)
    def _():
        stage_ref[HALO - 2:HALO, :] = hist_ref[...]

    stage_ref[HALO:HALO + rows, :] = cg_ref[...] * hd_ref[...]
    y = _conv3(stage_ref, w_ref, rows)
    o_ref[...] = (bg_ref[...] * y).astype(o_ref.dtype)
    tail = stage_ref[rows:rows + HALO, :]
    stage_ref[0:HALO, :] = tail
    hist_out_ref[...] = tail[HALO - 2:, :]


def short_conv(z, w_conv, hist, n_seq, seq_len, rows, out_dtype):
    tokens = z.shape[0]
    per = seq_len // rows

    def col(blk):
        return pl.BlockSpec((rows, WIDTH), lambda b, i: (b * per + i, blk))

    hist_spec = pl.BlockSpec((None, CONV_WIDTH - 1, WIDTH), lambda b, i: (b, 0, 0))
    return pl.pallas_call(
        functools.partial(_short_conv_kernel, rows=rows),
        grid=(n_seq, per),
        in_specs=[col(9), col(10), col(11), pl.BlockSpec((CONV_WIDTH, WIDTH), lambda b, i: (0, 0)), hist_spec],
        out_specs=[pl.BlockSpec((rows, WIDTH), lambda b, i: (b * per + i, 0)), hist_spec],
        out_shape=[jax.ShapeDtypeStruct((tokens, WIDTH), out_dtype),
                   jax.ShapeDtypeStruct((n_seq, CONV_WIDTH - 1, WIDTH), F32)],
        scratch_shapes=[pltpu.VMEM((HALO + rows, WIDTH), F32)],
        compiler_params=_params("arbitrary", "arbitrary"),
        name="short_conv",
    )(z, z, z, w_conv, hist)


def _merge_kernel(h_ref, wm_ref, bm_ref, o_ref, wb_ref, out_ref, acc_ref):
    n = pl.program_id(2)
    gate = jnp.dot(h_ref[...], wm_ref[...], preferred_element_type=F32) + bm_ref[...]
    proj = jnp.dot(o_ref[...].astype(BF16), wb_ref[...], preferred_element_type=F32)
    term = jax.nn.sigmoid(gate) * proj

    @pl.when(n == 0)
    def _():
        acc_ref[...] = term

    @pl.when(n > 0)
    def _():
        acc_ref[...] += term

    @pl.when(n == pl.num_programs(2) - 1)
    def _():
        out_ref[...] = acc_ref[...].astype(out_ref.dtype)


def _merge_cast_kernel(h_ref, wm_ref, bm_ref, o_ref, wb_ref, out_ref, wm16_ref, wb16_ref, acc_ref):
    n = pl.program_id(1)
    wm = wm_ref[...].astype(BF16)
    wb = wb_ref[...].astype(BF16)
    wm16_ref[...] = wm
    wb16_ref[...] = wb
    gate = jnp.dot(h_ref[...], wm, preferred_element_type=F32) + bm_ref[...]
    proj = jnp.dot(o_ref[...].astype(BF16), wb, preferred_element_type=F32)
    term = jax.nn.sigmoid(gate) * proj

    @pl.when(n == 0)
    def _():
        acc_ref[...] = term

    @pl.when(n > 0)
    def _():
        acc_ref[...] += term

    @pl.when(n == pl.num_programs(1) - 1)
    def _():
        out_ref[...] = acc_ref[...].astype(out_ref.dtype)


def gated_merge_cast(h, branches, w_merge, b_merge, w_branch, layer, bn):
    nb, tokens, _ = branches.shape
    depth = w_merge.shape[0]
    return pl.pallas_call(
        _merge_cast_kernel,
        grid=(D_MODEL // bn, nb),
        in_specs=[pl.BlockSpec((tokens, D_MODEL), lambda j, n: (0, 0)),
                  pl.BlockSpec((None, None, D_MODEL, bn), lambda j, n: (layer, n, 0, j)),
                  pl.BlockSpec((None, None, 1, bn), lambda j, n: (layer, n, 0, j)),
                  pl.BlockSpec((None, tokens, WIDTH), lambda j, n: (n, 0, 0)),
                  pl.BlockSpec((None, None, WIDTH, bn), lambda j, n: (layer, n, 0, j))],
        out_specs=[pl.BlockSpec((tokens, bn), lambda j, n: (0, j)),
                   pl.BlockSpec((None, None, D_MODEL, bn), lambda j, n: (0, n, 0, j)),
                   pl.BlockSpec((None, None, WIDTH, bn), lambda j, n: (0, n, 0, j))],
        out_shape=[jax.ShapeDtypeStruct((tokens, D_MODEL), BF16),
                   jax.ShapeDtypeStruct((1, nb, D_MODEL, D_MODEL), BF16),
                   jax.ShapeDtypeStruct((1, nb, WIDTH, D_MODEL), BF16)],
        scratch_shapes=[pltpu.VMEM((tokens, bn), F32)],
        compiler_params=_params("arbitrary", "arbitrary"),
        name="gated_merge_cast",
    )(h, w_merge, b_merge.reshape(depth, nb, 1, D_MODEL), branches, w_branch)


def gated_merge(h, branches, w_merge, b_merge, w_branch, layer, bm, bn):
    nb, tokens, _ = branches.shape
    depth = b_merge.shape[0]
    return pl.pallas_call(
        _merge_kernel,
        grid=(tokens // bm, D_MODEL // bn, nb),
        in_specs=[pl.BlockSpec((bm, D_MODEL), lambda i, j, n: (i, 0), pipeline_mode=pl.Buffered(1)),
                  pl.BlockSpec((None, None, D_MODEL, bn), lambda i, j, n: (0, n, 0, j)),
                  pl.BlockSpec((None, None, 1, bn), lambda i, j, n: (layer, n, 0, j)),
                  pl.BlockSpec((None, bm, WIDTH), lambda i, j, n: (n, i, 0)),
                  pl.BlockSpec((None, None, WIDTH, bn), lambda i, j, n: (0, n, 0, j))],
        out_specs=pl.BlockSpec((bm, bn), lambda i, j, n: (i, j)),
        out_shape=jax.ShapeDtypeStruct((tokens, D_MODEL), BF16),
        scratch_shapes=[pltpu.VMEM((bm, bn), F32)],
        compiler_params=_params("parallel", "arbitrary", "arbitrary"),
        name="gated_merge",
    )(h, w_merge, b_merge.reshape(depth, nb, 1, D_MODEL), branches, w_branch)


FF_BLK = 256
FF_HALF_BLKS = D_FF // FF_BLK


def _ffn_up_kernel(h_ref, wa0_ref, wa1_ref, wb0_ref, wb1_ref, cwa_ref, cwb_ref, ba_ref, bb_ref,
                   ha_ref, hb_ref, act_ref, ta_ref, tb_ref, sa_ref, sb_ref, ca_ref, cb_ref,
                   *, rows, per_seq):
    i = pl.program_id(0)
    j = pl.program_id(1)
    first = (i % per_seq) == 0

    @pl.when(first)
    def _():
        sa_ref[HALO - 2:HALO, :] = ha_ref[...]
        sb_ref[HALO - 2:HALO, :] = hb_ref[...]

    @pl.when(jnp.logical_not(first))
    def _():
        sa_ref[0:HALO, :] = ca_ref[j]
        sb_ref[0:HALO, :] = cb_ref[j]

    h = h_ref[...]
    cur = slice(HALO, HALO + rows)
    sa_ref[cur, :FF_BLK] = jnp.dot(h, wa0_ref[...], preferred_element_type=F32)
    sb_ref[cur, :FF_BLK] = jnp.dot(h, wb0_ref[...], preferred_element_type=F32)
    sa_ref[cur, FF_BLK:] = jnp.dot(h, wa1_ref[...], preferred_element_type=F32)
    sb_ref[cur, FF_BLK:] = jnp.dot(h, wb1_ref[...], preferred_element_type=F32)
    a = _conv3(sa_ref, cwa_ref, rows) + ba_ref[...]
    b = _conv3(sb_ref, cwb_ref, rows) + bb_ref[...]
    act_ref[...] = (a * jax.nn.sigmoid(a) * b).astype(act_ref.dtype)
    tail_a = sa_ref[rows:rows + HALO, :]
    tail_b = sb_ref[rows:rows + HALO, :]
    ca_ref[j] = tail_a
    cb_ref[j] = tail_b
    ta_ref[...] = tail_a
    tb_ref[...] = tail_b


def ffn_up(h, w_up, layer, cwa, cwb, ba, bb, hist_a, hist_b, n_seq, seq_len, bm):
    tokens = h.shape[0]
    per_seq = seq_len // bm
    bn = 2 * FF_BLK
    nj = D_FF_PAD // bn
    last_blk = 2 * FF_HALF_BLKS - 1

    def wspec(first_blk, sub):
        return pl.BlockSpec((None, D_MODEL, FF_BLK),
                            lambda i, j: (layer, 0, jnp.minimum(first_blk + 2 * j + sub, last_blk)))

    cspec = pl.BlockSpec((CONV_WIDTH, bn), lambda i, j: (0, j))
    bspec = pl.BlockSpec((1, bn), lambda i, j: (0, j))
    hspec = pl.BlockSpec((None, CONV_WIDTH - 1, bn), lambda i, j: (i // per_seq, 0, j))
    tspec = pl.BlockSpec((None, HALO, bn), lambda i, j: (i, 0, j))
    tail_shape = jax.ShapeDtypeStruct((tokens // bm, HALO, D_FF_PAD), F32)
    act, tail_a, tail_b = pl.pallas_call(
        functools.partial(_ffn_up_kernel, rows=bm, per_seq=per_seq),
        grid=(tokens // bm, nj),
        in_specs=[pl.BlockSpec((bm, D_MODEL), lambda i, j: (i, 0)),
                  wspec(0, 0), wspec(0, 1), wspec(FF_HALF_BLKS, 0), wspec(FF_HALF_BLKS, 1),
                  cspec, cspec, bspec, bspec, hspec, hspec],
        out_specs=[pl.BlockSpec((bm, bn), lambda i, j: (i, j)), tspec, tspec],
        out_shape=[jax.ShapeDtypeStruct((tokens, D_FF_PAD), BF16), tail_shape, tail_shape],
        scratch_shapes=[pltpu.VMEM((HALO + bm, bn), F32), pltpu.VMEM((HALO + bm, bn), F32),
                        pltpu.VMEM((nj, HALO, bn), F32), pltpu.VMEM((nj, HALO, bn), F32)],
        compiler_params=_params("arbitrary", "arbitrary"),
        name="ffn_up",
    )(h, w_up, w_up, w_up, w_up, cwa, cwb, ba, bb, hist_a, hist_b)
    return act, tail_a[per_seq - 1::per_seq], tail_b[per_seq - 1::per_seq]


def _ffn_act_kernel(ua_ref, ub_ref, cwa_ref, cwb_ref, ba_ref, bb_ref, ha_ref, hb_ref, act_ref,
                    sa_ref, sb_ref, *, rows):
    sa_ref[HALO - 2:HALO, :] = ha_ref[...]
    sb_ref[HALO - 2:HALO, :] = hb_ref[...]
    sa_ref[HALO:HALO + rows, :] = ua_ref[...]
    sb_ref[HALO:HALO + rows, :] = ub_ref[...]
    a = _conv3(sa_ref, cwa_ref, rows) + ba_ref[...]
    b = _conv3(sb_ref, cwb_ref, rows) + bb_ref[...]
    act_ref[...] = (a * jax.nn.sigmoid(a) * b).astype(act_ref.dtype)


def ffn_act(up, w_conv, bias, hist, n_seq, seq_len):
    tokens = up.shape[0]
    bn = D_FF // 2
    cspec = [pl.BlockSpec((CONV_WIDTH, bn), lambda b, j, o=o: (0, o + j)) for o in (0, 2)]
    bspec = [pl.BlockSpec((1, bn), lambda b, j, o=o: (0, o + j)) for o in (0, 2)]
    hspec = [pl.BlockSpec((None, CONV_WIDTH - 1, bn), lambda b, j, o=o: (b, 0, o + j)) for o in (0, 2)]
    uspec = [pl.BlockSpec((seq_len, bn), lambda b, j, o=o: (b, o + j)) for o in (0, 2)]
    return pl.pallas_call(
        functools.partial(_ffn_act_kernel, rows=seq_len),
        grid=(n_seq, 2),
        in_specs=uspec + cspec + bspec + hspec,
        out_specs=pl.BlockSpec((seq_len, bn), lambda b, j: (b, j)),
        out_shape=jax.ShapeDtypeStruct((tokens, D_FF), F32),
        scratch_shapes=[pltpu.VMEM((HALO + seq_len, bn), F32), pltpu.VMEM((HALO + seq_len, bn), F32)],
        compiler_params=_params("parallel", "parallel"),
        name="ffn_act",
    )(up, up, w_conv, w_conv, bias, bias, hist, hist)


WZ_K = 1024


def _in_proj_cast_kernel(x_ref, a_ref, b_ref, z_ref, wz_ref):
    j = pl.program_id(0)
    kb = pl.program_id(1)

    def emit(w_t):
        w = w_t.T.astype(BF16)
        wz_ref[...] = w
        part = jnp.dot(x_ref[...], w, preferred_element_type=F32)

        @pl.when(kb == 0)
        def _():
            z_ref[...] = part

        @pl.when(kb > 0)
        def _():
            z_ref[...] += part

    @pl.when(j < FORGET_COL // WIDTH)
    def _():
        emit(a_ref[...])

    @pl.when(j >= FORGET_COL // WIDTH)
    def _():
        emit(jnp.concatenate([a_ref[N_HEADS:, :], b_ref[...]], axis=0))


def in_proj_cast(x, w_in_t, layer):
    m, d = x.shape
    sub = WIDTH // N_HEADS
    return pl.pallas_call(
        _in_proj_cast_kernel,
        grid=(Z_COLS // WIDTH, d // WZ_K),
        in_specs=[pl.BlockSpec((m, WZ_K), lambda j, kb: (0, kb)),
                  pl.BlockSpec((None, WIDTH, WZ_K), lambda j, kb: (layer, j, kb)),
                  pl.BlockSpec((None, N_HEADS, WZ_K), lambda j, kb: (layer, sub * (j + 1), kb))],
        out_specs=[pl.BlockSpec((m, WIDTH), lambda j, kb: (0, j)),
                   pl.BlockSpec((None, WZ_K, WIDTH), lambda j, kb: (0, kb, j))],
        out_shape=[jax.ShapeDtypeStruct((m, Z_COLS), F32), jax.ShapeDtypeStruct((1, d, Z_COLS), BF16)],
        compiler_params=_params("parallel", "arbitrary"),
        name="in_proj_cast",
    )(x, w_in_t, w_in_t)


def _pad_cols(a, n):
    return jnp.pad(a, ((0, 0), (0, n - a.shape[1])))


def _split_hist(hist):
    pad = ((0, 0), (0, 0), (0, D_FF_PAD - D_FF))
    return jnp.pad(hist[..., :D_FF], pad), jnp.pad(hist[..., D_FF:], pad)


def _join_hist(tail_a, tail_b):
    return jnp.concatenate([tail_a[..., :D_FF], tail_b[..., :D_FF]], axis=-1)


def mixer_and_ffn(grp, layer, x, h, mod_op, lw, small, tabs, attend, ret_s0, conv_hist, ffn_hist, next_pre,
                  next_mod_op):
    n_seq, seq_len, tokens = grp.n_seq, grp.seq_len, grp.tokens
    bmm = grp.mm_block
    act_dtype = BF16 if seq_len >= 16 else F32
    cast = "f32" in lw
    lw16 = dict(b_merge=lw["b_merge"]) if cast else lw

    if cast:
        z, lw16["wz"] = in_proj_cast(h, lw["w_in_t"], layer)
    else:
        z = matmul(h, lw["wz"], 0, F32, bmm, MM_COLS)
    logf_t, cum_t = forget_logits(grp, h, small["wf_t"], small["b_forget"])

    chunk_rows = min(seq_len, CHUNK)
    n_chunks = seq_len // chunk_rows
    o_a, v_tail = chunk_gating(z, small["a_ln_g"], small["a_ws"][:, :chunk_rows, :chunk_rows],
                               jnp.broadcast_to(small["a_bs"][:, :chunk_rows, None],
                                                (N_HEADS, chunk_rows, HEAD_DIM)),
                               n_seq, n_chunks, chunk_rows, act_dtype)
    o_b = attend(z, cum_t)
    o_c, ret_state = retention(z, ret_s0, tabs, small["ret_gn_g"], n_seq, seq_len, act_dtype)
    conv_rows = min(seq_len, CONV_ROWS)
    o_d, conv_state = short_conv(z, small["w_sc_conv"], conv_hist, n_seq, seq_len, conv_rows, act_dtype)

    branches = jnp.stack([o_a, o_b.astype(act_dtype), o_c, o_d], axis=0)
    if cast:
        merged, lw16["w_merge"], lw16["w_branch"] = gated_merge_cast(
            h, branches, lw["w_merge"], lw["b_merge"], lw["w_branch"], layer, CAST_COLS)
        y, lw16["w_out"] = matmul_cast(merged, lw["w_out"], layer, CAST_COLS)
    else:
        merged = gated_merge(h, branches, lw["w_merge"], lw["b_merge"], lw["w_branch"], layer, bmm, MERGE_COLS)
        y = matmul(merged, lw["w_out"], 0, F32, bmm, MM_COLS)
    x, h2 = residual(grp, x, y, small["g_post_mix"], mod_op, 2, small["g_pre_ffn"], mod_op, 3, 4)

    cw = small["w_ffn_conv"]
    bias = small["b_ffn_conv"].reshape(1, 2 * D_FF)
    if cast:
        up, lw16["w_up"] = matmul_cast(h2, lw["w_up"], layer, CAST_COLS)
        act = _pad_cols(ffn_act(up, cw, bias, ffn_hist, n_seq, seq_len), D_FF_PAD)
        ffn_state = up.reshape(n_seq, seq_len, 2 * D_FF)[:, seq_len - 2:]
        y2, lw16["w_down"] = matmul_ksplit_cast(act, lw["w_down"], layer, CAST_COLS, DOWN_K)
    else:
        hist_a, hist_b = _split_hist(ffn_hist)
        act, tail_a, tail_b = ffn_up(h2, lw["w_up"], 0,
                                     _pad_cols(cw[:, :D_FF], D_FF_PAD), _pad_cols(cw[:, D_FF:], D_FF_PAD),
                                     _pad_cols(bias[:, :D_FF], D_FF_PAD), _pad_cols(bias[:, D_FF:], D_FF_PAD),
                                     hist_a, hist_b, n_seq, seq_len, bmm)
        ffn_state = _join_hist(tail_a[:, HALO - 2:], tail_b[:, HALO - 2:])
        y2 = matmul_ksplit(act, lw["w_down"], 0, bmm, MM_COLS, DOWN_K)
    if next_pre is None:
        x = residual(grp, x, y2, small["g_post_ffn"], mod_op, 5)
        h_next = None
    else:
        x, h_next = residual(grp, x, y2, small["g_post_ffn"], mod_op, 5, next_pre, next_mod_op, 0, 1)

    k_b = z[:, BK_BLK * HEAD_DIM:BV_BLK * HEAD_DIM].reshape(n_seq, seq_len, N_HEADS, HEAD_DIM)
    v_b = z[:, BV_BLK * HEAD_DIM:CQ_BLK * HEAD_DIM].reshape(n_seq, seq_len, N_HEADS, HEAD_DIM)
    logf = logf_t.T.reshape(n_seq, seq_len, N_HEADS)
    return x, h_next, (k_b, v_b, logf, ret_state, conv_state, v_tail, ffn_state), lw16


def kernel(x_prompt, x_sample, cache_k, cache_v, cache_logf, state_ret, state_conv, state_ffn_conv,
           page_table, c_prompt, c_sample, w_ada, b_ada, g_pre_mix, g_post_mix, g_pre_ffn, g_post_ffn,
           w_in, b_forget, a_ln_g, a_ws, a_bs, ret_gn_g, w_sc_conv, w_branch, w_merge, b_merge, w_out,
           w_up, w_ffn_conv, b_ffn_conv, w_down):
    depth = w_in.shape[0]
    bp, lp, _ = x_prompt.shape
    bs, ls, _ = x_sample.shape
    grp_p = Group(bp, lp, row_block=ROW_BLOCK, mm_block=MM_ROWS)
    grp_s = Group(bs, ls, row_block=bs * ls, mm_block=bs * ls)

    c_all = jnp.pad(jnp.concatenate([c_prompt, c_sample], axis=0), ((0, 16 - bp - bs), (0, 0)))
    mod = ada_modulation(c_all, w_ada, b_ada)

    tabs_p = retention_tables(jnp.arange(lp), math.gcd(lp, CHUNK))
    tabs_s = retention_tables(PAST_LEN + jnp.arange(ls), math.gcd(ls, CHUNK))

    n_pool = cache_k.shape[1]
    cache_k2 = cache_k.reshape(depth, n_pool, PAGE_SIZE * N_HEADS, HEAD_DIM)
    cache_v2 = cache_v.reshape(depth, n_pool, PAGE_SIZE * N_HEADS, HEAD_DIM)
    cache_lp_t = cache_logf.transpose(0, 1, 3, 2)

    xp = x_prompt.reshape(grp_p.tokens, D_MODEL)
    xs = x_sample.reshape(grp_s.tokens, D_MODEL)
    mods_p = [grp_p.mod_operand(mod[l, :bp]) for l in range(depth)]
    mods_s = [grp_s.mod_operand(mod[l, bp:bp + bs]) for l in range(depth)]
    hp = prenorm(grp_p, xp, g_pre_mix[0], mods_p[0], 0, 1)
    hs = prenorm(grp_s, xs, g_pre_mix[0], mods_s[0], 0, 1)

    zeros_ret = jnp.zeros((bp, N_HEADS, HEAD_DIM, HEAD_DIM), F32)
    zeros_conv = jnp.zeros((bp, CONV_WIDTH - 1, WIDTH), F32)
    zeros_ffn = jnp.zeros((bp, CONV_WIDTH - 1, 2 * D_FF), F32)
    eye = jnp.eye(N_HEADS, dtype=F32)

    w_in_t = jnp.swapaxes(w_in, 1, 2)
    lw32 = dict(f32=True, w_in_t=w_in_t, w_merge=w_merge, b_merge=b_merge, w_branch=w_branch,
                w_out=w_out, w_up=w_up, w_down=w_down)

    st_p, st_s = [], []
    for l in range(depth):
        wf_t = w_in_t[l, FORGET_COL:FORGET_COL + N_HEADS]
        small = dict(wf_t=jnp.pad(wf_t, ((0, 16 - N_HEADS), (0, 0))),
                     b_forget=b_forget[l], a_ln_g=a_ln_g[l], a_ws=a_ws[l], a_bs=a_bs[l],
                     ret_gn_g=ret_gn_g[l], w_sc_conv=w_sc_conv[l],
                     w_ffn_conv=w_ffn_conv[l], b_ffn_conv=b_ffn_conv[l],
                     g_post_mix=g_post_mix[l], g_pre_ffn=g_pre_ffn[l], g_post_ffn=g_post_ffn[l])
        next_pre = g_pre_mix[l + 1] if l + 1 < depth else None

        def attend_p(z, cum_t):
            return fox_prompt(z, cum_t, bp, lp)

        def attend_s(z, cum_t, l=l):
            zs = z.reshape(bs, ls, Z_COLS)
            q = zs[..., BQ_BLK * HEAD_DIM:BK_BLK * HEAD_DIM].reshape(bs, ls, N_HEADS, HEAD_DIM)
            q = q.transpose(0, 2, 1, 3) * HEAD_DIM ** -0.5
            qbd = (q[:, :, :, None, :] * eye[None, :, None, :, None]).reshape(bs, N_HEADS * ls, WIDTH)
            k_new = zs[..., BK_BLK * HEAD_DIM:BV_BLK * HEAD_DIM]
            v_new = zs[..., BV_BLK * HEAD_DIM:CQ_BLK * HEAD_DIM]
            cum = cum_t.reshape(N_HEADS, bs, ls).transpose(1, 0, 2)
            fcol = cum.reshape(bs, N_HEADS * ls, 1)
            frow = jnp.broadcast_to(cum[:, :, None, :], (bs, N_HEADS, ls, ls)).reshape(bs, N_HEADS * ls, ls)
            o = fox_sample(page_table, qbd.astype(BF16), k_new, v_new, fcol, frow,
                           cache_k2, cache_v2, cache_lp_t, l)
            return o.reshape(bs * ls, WIDTH)

        xs, hs, ss, lw16 = mixer_and_ffn(grp_s, l, xs, hs, mods_s[l], lw32, small, tabs_s, attend_s,
                                         state_ret[l], state_conv[l], state_ffn_conv[l], next_pre,
                                         mods_s[l + 1] if l + 1 < depth else None)
        xp, hp, sp, _ = mixer_and_ffn(grp_p, l, xp, hp, mods_p[l], lw16, small, tabs_p, attend_p,
                                      zeros_ret, zeros_conv, zeros_ffn, next_pre,
                                      mods_p[l + 1] if l + 1 < depth else None)
        st_p.append(sp)
        st_s.append(ss)

    def stack(states, idx):
        return jnp.stack([s[idx] for s in states], axis=0)

    return (xp.reshape(bp, lp, D_MODEL), xs.reshape(bs, ls, D_MODEL),
            stack(st_p, 0), stack(st_p, 1), stack(st_p, 2),
            stack(st_s, 0), stack(st_s, 1), stack(st_s, 2),
            stack(st_p, 3), stack(st_s, 3), stack(st_p, 4), stack(st_s, 4),
            stack(st_p, 5), stack(st_s, 5), stack(st_p, 6), stack(st_s, 6))
```
